```python
import math
import jax, jax.numpy as jnp
from jax import lax
import numpy as np

D_MODEL = 1024
BATCH = 16
SEQ = 2048
DEPTH = 2

NSA_HEADS = 8
NSA_KV_HEADS = 2
NSA_GROUP = NSA_HEADS // NSA_KV_HEADS
NSA_HEAD_DIM = 64
NSA_BRANCHES = 3
CMP_BLOCK = 32
CMP_STRIDE = 16
CMP_HIDDEN = 128
SEL_BLOCK = 64
SEL_TOPN = 8
WINDOW = 512
NSA_Q_BLOCK = 64
FORCE_BONUS = 1000.0
GLA_HEADS = 4
GLA_DK = 64
GLA_DV = 128
GLA_GATE_RANK = 16
GLA_TAU = 16.0
GLA_CHUNK = 64
MLA_HEADS = 16
MLA_NOPE = 64
MLA_ROPE = 32
MLA_V = 64
MLA_Q_LORA = 384
MLA_KV_LORA = 256
ROPE_THETA = 10000.0
ATTN_Q_BLOCK = 128
NORM_EPS = 1e-5
NEG_INF = -1e30
DEEPNORM_ALPHA = (2 * DEPTH) ** 0.25
DEEPNORM_BETA = (8 * DEPTH) ** -0.25
N_EVEN = (DEPTH + 1) // 2
N_ODD = DEPTH // 2

NSA_Q_W = NSA_HEADS * NSA_HEAD_DIM
NSA_KV_W = NSA_BRANCHES * 2 * NSA_KV_HEADS * NSA_HEAD_DIM
NSA_GATE_W = NSA_BRANCHES * NSA_HEADS
GLA_QK_W = GLA_HEADS * GLA_DK
GLA_V_W = GLA_HEADS * GLA_DV
EVEN_SPLITS = (NSA_Q_W, NSA_KV_W, NSA_GATE_W, NSA_Q_W, GLA_QK_W, GLA_QK_W, GLA_V_W, GLA_GATE_RANK, GLA_V_W)
EVEN_IN_W = 3368
EVEN_OUT_W = NSA_Q_W + GLA_V_W
MLA_Z_W = MLA_HEADS * MLA_V
ODD_SPLITS = (MLA_Q_LORA, MLA_KV_LORA, MLA_ROPE, MLA_Z_W)
ODD_IN_W = 1696

kernel_name = "nsa_gla_mla_deepnorm_hybrid"


def _split(h, sizes):
    out, off = [], 0
    for s in sizes:
        out.append(h[..., off:off + s])
        off += s
    return out


def _layernorm(x, g, b):
    xf = x.astype(jnp.float32)
    mu = jnp.mean(xf, -1, keepdims=True)
    var = jnp.mean(jnp.square(xf - mu), -1, keepdims=True)
    return ((xf - mu) * lax.rsqrt(var + NORM_EPS) * g + b).astype(x.dtype)


def _rmsnorm(x, g):
    xf = x.astype(jnp.float32)
    y = xf * lax.rsqrt(jnp.mean(jnp.square(xf), -1, keepdims=True) + NORM_EPS)
    return (y * g).astype(x.dtype)


def _alibi_slopes(n):
    return jnp.exp2(-(8.0 / n) * jnp.arange(1, n + 1, dtype=jnp.float32))


def _masked_softmax(s, mask):
    p = jax.nn.softmax(jnp.where(mask, s, NEG_INF), axis=-1)
    return p * mask


def _rope(x, pos):
    d = x.shape[-1]
    half = d // 2
    freqs = jnp.exp(-math.log(ROPE_THETA) * jnp.arange(half, dtype=jnp.float32) * 2.0 / d)
    ang = pos[:, None] * freqs[None, :]
    shape = (ang.shape[0],) + (1,) * (x.ndim - 3) + (half,)
    cos = jnp.cos(ang).reshape(shape).astype(x.dtype)
    sin = jnp.sin(ang).reshape(shape).astype(x.dtype)
    x1, x2 = x[..., :half], x[..., half:]
    return jnp.concatenate([x1 * cos - x2 * sin, x2 * cos + x1 * sin], -1)


def nsa_mixer(q, kv, gate_logits, cmp_pe, cmp_w1, cmp_w2):
    in_dtype = q.dtype
    B, S, _ = q.shape
    G, R, dh = NSA_KV_HEADS, NSA_GROUP, NSA_HEAD_DIM
    q = (q * dh ** -0.5).reshape(B, S, G, R, dh).transpose(0, 2, 3, 1, 4)
    kv = kv.reshape(B, S, NSA_BRANCHES, 2, G, dh).transpose(2, 3, 0, 4, 1, 5)
    slopes = _alibi_slopes(NSA_HEADS).reshape(G, R)[:, :, None, None]
    pos = jnp.arange(S)

    n_cmp = (S - CMP_BLOCK) // CMP_STRIDE + 1
    cmp_start = jnp.arange(n_cmp) * CMP_STRIDE
    cmp_idx = cmp_start[:, None] + jnp.arange(CMP_BLOCK)[None, :]

    def compress(t, j):
        blk = t[:, :, cmp_idx] + cmp_pe[j]
        h = jax.nn.silu(blk.reshape(B, G, n_cmp, CMP_BLOCK * dh) @ cmp_w1[j])
        return h @ cmp_w2[j]

    k_cmp = compress(kv[0, 0], 0)
    v_cmp = compress(kv[0, 1], 1)
    cmp_end = cmp_start + CMP_BLOCK - 1
    dist_c = (pos[:, None] - cmp_end[None, :]).astype(jnp.float32)
    s = jnp.einsum('bgrtd,bgnd->bgrtn', q, k_cmp).astype(jnp.float32) - slopes * dist_c
    p_cmp = _masked_softmax(s, dist_c >= 0)
    o_cmp = jnp.einsum('bgrtn,bgnd->bgrtd', p_cmp.astype(v_cmp.dtype), v_cmp)

    n_sel = S // SEL_BLOCK
    n_top = min(SEL_TOPN, n_sel)
    sel_start = jnp.arange(n_sel) * SEL_BLOCK
    overlap = jnp.clip(
        jnp.minimum(cmp_start[:, None] + CMP_BLOCK, sel_start[None, :] + SEL_BLOCK)
        - jnp.maximum(cmp_start[:, None], sel_start[None, :]), 0).astype(jnp.float32) / CMP_BLOCK
    imp = jnp.einsum('bgrtn,nj->bgtj', p_cmp, overlap)
    cur = (pos // SEL_BLOCK)[:, None]
    jj = jnp.arange(n_sel)[None, :]
    forced = (jj == 0) | (jj == cur) | (jj == cur - 1)
    imp = jnp.where(jj > cur, -1.0, imp + jnp.where(forced, FORCE_BONUS, 0.0))
    _, sel_idx = lax.top_k(imp, n_top)

    k_sel = kv[1, 0].reshape(B, G, n_sel, SEL_BLOCK, dh)
    v_sel = kv[1, 1].reshape(B, G, n_sel, SEL_BLOCK, dh)
    pad = ((0, 0), (0, 0), (WINDOW, 0), (0, 0))
    k_win = jnp.pad(kv[2, 0], pad)
    v_win = jnp.pad(kv[2, 1], pad)

    qb_len = NSA_Q_BLOCK
    n_qb = S // qb_len
    q_blocks = q.reshape(B, G, R, n_qb, qb_len, dh).transpose(3, 0, 1, 2, 4, 5)
    idx_blocks = sel_idx.reshape(B, G, n_qb, qb_len, n_top).transpose(2, 0, 1, 3, 4)
    starts = jnp.arange(n_qb) * qb_len
    bi = jnp.arange(B)[:, None, None, None]
    gi = jnp.arange(G)[None, :, None, None]
    n_keys = n_top * SEL_BLOCK

    def block(args):
        qb, ib, start = args
        tq = start + jnp.arange(qb_len)
        ks = k_sel[bi, gi, ib].reshape(B, G, qb_len, n_keys, dh)
        vs = v_sel[bi, gi, ib].reshape(B, G, qb_len, n_keys, dh)
        kpos = (ib[..., None] * SEL_BLOCK + jnp.arange(SEL_BLOCK)).reshape(B, G, qb_len, n_keys)
        d_s = (tq[:, None] - kpos).astype(jnp.float32)[:, :, None]
        s_s = jnp.einsum('bgrqd,bgqkd->bgrqk', qb, ks).astype(jnp.float32) - slopes * d_s
        p_s = _masked_softmax(s_s, d_s >= 0)
        o_s = jnp.einsum('bgrqk,bgqkd->bgrqd', p_s.astype(vs.dtype), vs)
        kw = lax.dynamic_slice_in_dim(k_win, start, qb_len + WINDOW, axis=2)
        vw = lax.dynamic_slice_in_dim(v_win, start, qb_len + WINDOW, axis=2)
        kp = start - WINDOW + jnp.arange(qb_len + WINDOW)
        d_w = tq[:, None] - kp[None, :]
        mask_w = (d_w >= 0) & (d_w < WINDOW) & (kp[None, :] >= 0)
        s_w = jnp.einsum('bgrqd,bgkd->bgrqk', qb, kw).astype(jnp.float32) - slopes * d_w.astype(jnp.float32)
        p_w = _masked_softmax(s_w, mask_w)
        o_w = jnp.einsum('bgrqk,bgkd->bgrqd', p_w.astype(vw.dtype), vw)
        return o_s, o_w

    o_sel, o_win = lax.map(block, (q_blocks, idx_blocks, starts))
    o_sel = o_sel.transpose(1, 2, 3, 0, 4, 5).reshape(B, G, R, S, dh)
    o_win = o_win.transpose(1, 2, 3, 0, 4, 5).reshape(B, G, R, S, dh)

    g = jax.nn.sigmoid(gate_logits.astype(jnp.float32)).reshape(B, S, NSA_BRANCHES, G, R)
    g = g.transpose(2, 0, 3, 4, 1)[..., None]
    o = g[0] * o_cmp + g[1] * o_sel + g[2] * o_win
    return o.transpose(0, 3, 1, 2, 4).reshape(B, S, NSA_Q_W).astype(in_dtype)


def gla_mixer(q, k, v, gate_low, w_gate, b_gate, norm_g):
    in_dtype = q.dtype
    B, S, _ = q.shape
    H, C = GLA_HEADS, GLA_CHUNK
    n = S // C

    def heads(t, d):
        return t.astype(jnp.float32).reshape(B, n, C, H, d).transpose(0, 3, 1, 2, 4)

    q = heads(q, GLA_DK) * GLA_DK ** -0.5
    k = heads(k, GLA_DK)
    v = heads(v, GLA_DV)
    log_a = jax.nn.log_sigmoid((gate_low @ w_gate + b_gate).astype(jnp.float32)) / GLA_TAU
    b = jnp.cumsum(heads(log_a, GLA_DK), axis=3)
    b_last = b[:, :, :, -1:]
    q_d = q * jnp.exp(b)
    k_d = k * jnp.exp(-b)
    causal = jnp.tril(jnp.ones((C, C), dtype=bool))
    a = jnp.where(causal, jnp.einsum('bhncd,bhnsd->bhncs', q_d, k_d), 0.0)
    o_intra = jnp.einsum('bhncs,bhnse->bhnce', a, v)
    upd = jnp.einsum('bhnsd,bhnse->nbhde', k * jnp.exp(b_last - b), v)
    decay = jnp.exp(b_last[:, :, :, 0]).transpose(2, 0, 1, 3)

    def step(state, inp):
        dec, u = inp
        return dec[..., None] * state + u, state

    _, states = lax.scan(step, jnp.zeros((B, H, GLA_DK, GLA_DV), jnp.float32), (decay, upd))
    o_inter = jnp.einsum('bhncd,nbhde->bhnce', q_d, states)
    o = _rmsnorm(o_intra + o_inter, norm_g)
    return o.transpose(0, 2, 3, 1, 4).reshape(B, S, GLA_V_W).astype(in_dtype)


def mla_mixer(c_q, c_kv, k_rope, q_norm, w_uq, kv_norm, w_ukv):
    B, S, _ = c_q.shape
    H = MLA_HEADS
    pos_f = jnp.arange(S, dtype=jnp.float32)
    q = (_rmsnorm(c_q, q_norm) @ w_uq).reshape(B, S, H, MLA_NOPE + MLA_ROPE)
    q_nope = q[..., :MLA_NOPE]
    q_rope = _rope(q[..., MLA_NOPE:], pos_f)
    kv = (_rmsnorm(c_kv, kv_norm) @ w_ukv).reshape(B, S, H, MLA_NOPE + MLA_V)
    k_nope, v = kv[..., :MLA_NOPE], kv[..., MLA_NOPE:]
    k_rope = _rope(k_rope, pos_f)
    scale = (MLA_NOPE + MLA_ROPE) ** -0.5
    n_qb = S // ATTN_Q_BLOCK
    qn_b = (q_nope * scale).reshape(B, n_qb, ATTN_Q_BLOCK, H, MLA_NOPE).transpose(1, 0, 2, 3, 4)
    qr_b = (q_rope * scale).reshape(B, n_qb, ATTN_Q_BLOCK, H, MLA_ROPE).transpose(1, 0, 2, 3, 4)
    starts = jnp.arange(n_qb) * ATTN_Q_BLOCK
    kpos = jnp.arange(S)

    def block(args):
        qn, qr, start = args
        tq = start + jnp.arange(ATTN_Q_BLOCK)
        s = (jnp.einsum('bqhd,bkhd->bhqk', qn, k_nope)
             + jnp.einsum('bqhd,bkd->bhqk', qr, k_rope)).astype(jnp.float32)
        p = _masked_softmax(s, kpos[None, :] <= tq[:, None])
        return jnp.einsum('bhqk,bkhd->bqhd', p.astype(v.dtype), v)

    o = lax.map(block, (qn_b, qr_b, starts))
    return o.transpose(1, 0, 2, 3, 4).reshape(B, S, MLA_Z_W)


def even_layer(x, w_in, cmp_pe, cmp_w1, cmp_w2, gla_w_gate, gla_b_gate, gla_norm, w_out, ln_g, ln_b):
    h = x @ w_in
    nq, nkv, ng, nz, gq, gk, gv, ga, gz = _split(h, EVEN_SPLITS)
    o_nsa = nsa_mixer(nq, nkv, ng, cmp_pe, cmp_w1, cmp_w2) * jax.nn.silu(nz)
    o_gla = gla_mixer(gq, gk, gv, ga, gla_w_gate, gla_b_gate, gla_norm) * jax.nn.silu(gz)
    y = jnp.concatenate([o_nsa, o_gla], axis=-1) @ w_out
    return _layernorm(DEEPNORM_ALPHA * x + y, ln_g, ln_b)


def odd_layer(x, w_in, q_norm, w_uq, kv_norm, w_ukv, w_out, ln_g, ln_b):
    h = x @ w_in
    cq, ckv, kr, z = _split(h, ODD_SPLITS)
    o = mla_mixer(cq, ckv, kr, q_norm, w_uq, kv_norm, w_ukv) * jax.nn.silu(z)
    y = o @ w_out
    return _layernorm(DEEPNORM_ALPHA * x + y, ln_g, ln_b)


def setup_inputs(seed: int = 0) -> dict:
    key = jax.random.key(seed)
    ks = jax.random.split(key, 19)

    def nrm(k, shape, scale):
        return jax.random.normal(k, shape, jnp.float32) * scale

    E, O = N_EVEN, N_ODD
    return {
        "x": nrm(ks[0], (BATCH, SEQ, D_MODEL), 1.0),
        "e_w_in": nrm(ks[1], (E, D_MODEL, EVEN_IN_W), D_MODEL ** -0.5),
        "e_cmp_pe": nrm(ks[2], (E, 2, CMP_BLOCK, NSA_HEAD_DIM), 0.02),
        "e_cmp_w1": nrm(ks[3], (E, 2, CMP_BLOCK * NSA_HEAD_DIM, CMP_HIDDEN), (CMP_BLOCK * NSA_HEAD_DIM) ** -0.5),
        "e_cmp_w2": nrm(ks[4], (E, 2, CMP_HIDDEN, NSA_HEAD_DIM), CMP_HIDDEN ** -0.5),
        "e_gla_w_gate": nrm(ks[5], (E, GLA_GATE_RANK, GLA_QK_W), GLA_GATE_RANK ** -0.5),
        "e_gla_b_gate": nrm(ks[6], (E, GLA_QK_W), 0.1),
        "e_gla_norm": 1.0 + nrm(ks[7], (E, GLA_DV), 0.02),
        "e_w_out": nrm(ks[8], (E, EVEN_OUT_W, D_MODEL), EVEN_OUT_W ** -0.5 * DEEPNORM_BETA),
        "e_ln_g": 1.0 + nrm(ks[9], (E, D_MODEL), 0.02),
        "e_ln_b": nrm(ks[10], (E, D_MODEL), 0.02),
        "o_w_in": nrm(ks[11], (O, D_MODEL, ODD_IN_W), D_MODEL ** -0.5),
        "o_q_norm": 1.0 + nrm(ks[12], (O, MLA_Q_LORA), 0.02),
        "o_w_uq": nrm(ks[13], (O, MLA_Q_LORA, MLA_HEADS * (MLA_NOPE + MLA_ROPE)), MLA_Q_LORA ** -0.5),
        "o_kv_norm": 1.0 + nrm(ks[14], (O, MLA_KV_LORA), 0.02),
        "o_w_ukv": nrm(ks[15], (O, MLA_KV_LORA, MLA_HEADS * (MLA_NOPE + MLA_V)), MLA_KV_LORA ** -0.5),
        "o_w_out": nrm(ks[16], (O, MLA_Z_W, D_MODEL), MLA_Z_W ** -0.5 * DEEPNORM_BETA),
        "o_ln_g": 1.0 + nrm(ks[17], (O, D_MODEL), 0.02),
        "o_ln_b": nrm(ks[18], (O, D_MODEL), 0.02),
    }


def reference(x, e_w_in, e_cmp_pe, e_cmp_w1, e_cmp_w2, e_gla_w_gate, e_gla_b_gate, e_gla_norm,
              e_w_out, e_ln_g, e_ln_b, o_w_in, o_q_norm, o_w_uq, o_kv_norm, o_w_ukv, o_w_out,
              o_ln_g, o_ln_b):
    for layer in range(DEPTH):
        i = layer // 2
        if layer % 2 == 0:
            x = even_layer(x, e_w_in[i], e_cmp_pe[i], e_cmp_w1[i], e_cmp_w2[i], e_gla_w_gate[i],
                           e_gla_b_gate[i], e_gla_norm[i], e_w_out[i], e_ln_g[i], e_ln_b[i])
        else:
            x = odd_layer(x, o_w_in[i], o_q_norm[i], o_w_uq[i], o_kv_norm[i], o_w_ukv[i],
                          o_w_out[i], o_ln_g[i], o_ln_b[i])
    return x
```

```python
import functools
import math

import jax
import jax.numpy as jnp
import numpy as np
from jax import lax
from jax.experimental import pallas as pl
from jax.experimental.pallas import tpu as pltpu

F32 = jnp.float32
BF16 = jnp.bfloat16

D_MODEL = 1024
SEQ = 2048
DEPTH = 2
NSA_HEADS = 8
NSA_KV_HEADS = 2
NSA_GROUP = 4
NSA_HEAD_DIM = 64
CMP_BLOCK = 32
CMP_STRIDE = 16
CMP_HIDDEN = 128
N_CMP = (SEQ - CMP_BLOCK) // CMP_STRIDE + 1
N_CMP_PAD = 128
SEL_BLOCK = 64
N_SEL = SEQ // SEL_BLOCK
SEL_TOPN = 8
WINDOW = 512
FORCE_BONUS = 1000.0
GLA_HEADS = 4
GLA_DK = 64
GLA_DV = 128
GLA_GATE_RANK = 16
GLA_TAU = 16.0
GLA_CHUNK = 64
MLA_HEADS = 16
MLA_NOPE = 64
MLA_ROPE = 32
MLA_V = 64
MLA_Q_LORA = 384
MLA_KV_LORA = 256
ROPE_THETA = 10000.0
NORM_EPS = 1e-5
NEG_INF = -1e30
DEEPNORM_ALPHA = (2 * DEPTH) ** 0.25

LANES = 128
MASK_BIG = 2.0 ** 100
VMEM_LIMIT = 56 * 1024 * 1024

_E_OFF = np.cumsum([0, 512, 768, 24, 512, 256, 256, 512, 16, 512])
_O_OFF = np.cumsum([0, MLA_Q_LORA, MLA_KV_LORA, MLA_ROPE, MLA_HEADS * MLA_V])


def _cparams(sem):
    return pltpu.CompilerParams(dimension_semantics=sem, vmem_limit_bytes=VMEM_LIMIT)


def _silu(x):
    return x * jax.nn.sigmoid(x)


def _dot(a, b):
    return jnp.dot(a, b, preferred_element_type=F32)


def _dot_nt(a, b):
    return lax.dot_general(a, b, (((1,), (1,)), ((), ())), preferred_element_type=F32)


_A_SEGS = (("q", 512, BF16), ("kvc", 256, F32), ("kv", 1024, BF16), ("ng", 128, F32),
           ("nz", 512, F32), ("gqk", 512, F32), ("gv", 512, F32), ("ga", 128, F32),
           ("gz", 512, F32))
_A_OFF = np.cumsum([0] + [s[1] for s in _A_SEGS])


def _even_in_weight(w):
    o = _E_OFF
    z64 = jnp.zeros((D_MODEL, 64), F32)
    nkv = w[:, o[1]:o[2]]

    def kvcol(br, j, g):
        c = ((br * 2 + j) * NSA_KV_HEADS + g) * 64
        return nkv[:, c:c + 64]

    kv_blocks = []
    for br in (1, 2):
        for j in (0, 1):
            for g in (0, 1):
                kv_blocks += [kvcol(br, j, g), z64]
    ng = jnp.pad(w[:, o[2]:o[3]], ((0, 0), (0, LANES - 24)))
    ga = jnp.pad(w[:, o[7]:o[8]], ((0, 0), (0, LANES - GLA_GATE_RANK)))
    cols = [w[:, o[0]:o[1]] * (NSA_HEAD_DIM ** -0.5), nkv[:, :256]] + kv_blocks + [
        ng, w[:, o[3]:o[4]], w[:, o[4]:o[6]], w[:, o[6]:o[7]], ga, w[:, o[8]:o[9]]]
    return jnp.concatenate(cols, axis=1).astype(BF16)


def _nsa_kv_table():
    pos = np.arange(SEQ)
    k_al = np.zeros((SEQ, LANES), np.float32)
    k_al[:, 96] = 1.0
    k_al[:, 97] = 1.0
    k_al[:, 98] = pos // SEL_BLOCK
    k_al[:, 99] = pos % SEL_BLOCK
    k_sel = k_al.copy()
    k_sel[pos, 64 + pos // SEL_BLOCK] = 1.0
    v_one = np.zeros((SEQ, LANES), np.float32)
    v_one[:, 64] = 1.0
    tab = np.concatenate([k_sel, k_sel, v_one, v_one, k_al, k_al, v_one, v_one], axis=1)
    return jnp.asarray(tab)


def _even_in_kernel(x_ref, w_ref, tab_ref, *out_refs):
    xb = x_ref[...].astype(BF16)
    for (name, width, dt), off, o_ref in zip(_A_SEGS, _A_OFF[:-1], out_refs):
        r = _dot(xb, w_ref[:, off:off + width])
        if name == "kv":
            r = r + tab_ref[...]
        o_ref[...] = r.astype(dt)


def _even_in_proj(x2, w_bf, tab, tm=256):
    T = x2.shape[0]
    n_w = w_bf.shape[1]
    s_tiles = SEQ // tm
    out_shape = [jax.ShapeDtypeStruct((T, wd), dt) for _, wd, dt in _A_SEGS]
    out_specs = [pl.BlockSpec((tm, wd), lambda i: (i, 0)) for _, wd, _ in _A_SEGS]
    return pl.pallas_call(
        _even_in_kernel,
        grid=(T // tm,),
        in_specs=[pl.BlockSpec((tm, D_MODEL), lambda i: (i, 0)),
                  pl.BlockSpec((D_MODEL, n_w), lambda i: (0, 0)),
                  pl.BlockSpec((tm, 1024), lambda i: (i % s_tiles, 0))],
        out_specs=out_specs,
        out_shape=out_shape,
        compiler_params=_cparams(("parallel",)),
        name="even_in_proj",
    )(x2, w_bf, tab)


def _compress_kernel(x_ref, pe_ref, w1_ref, w2_ref, o_ref):
    w1_lo = w1_ref[0, :1024, :]
    w1_hi = w1_ref[0, 1024:, :]
    w2 = w2_ref[0]
    for g in range(NSA_KV_HEADS):
        xg = x_ref[0, 0, g]
        a = _dot((xg + pe_ref[0, 0:1, :]).astype(BF16), w1_lo)
        b = _dot((xg + pe_ref[0, 1:2, :]).astype(BF16), w1_hi)
        h = a + pltpu.roll(b, N_CMP_PAD - 1, 0)
        o_ref[0, 0, g] = _dot(_silu(h).astype(BF16), w2)


def _compress(xc, pe2, w1, w2):
    B = xc.shape[0]
    return pl.pallas_call(
        _compress_kernel,
        grid=(B, 2),
        in_specs=[pl.BlockSpec((1, 1, NSA_KV_HEADS, N_CMP_PAD, 1024), lambda b, j: (b, j, 0, 0, 0)),
                  pl.BlockSpec((1, 2, 1024), lambda b, j: (j, 0, 0)),
                  pl.BlockSpec((1, 2048, CMP_HIDDEN), lambda b, j: (j, 0, 0)),
                  pl.BlockSpec((1, CMP_HIDDEN, NSA_HEAD_DIM), lambda b, j: (j, 0, 0))],
        out_specs=pl.BlockSpec((1, 1, NSA_KV_HEADS, N_CMP_PAD, NSA_HEAD_DIM),
                               lambda b, j: (b, j, 0, 0, 0)),
        out_shape=jax.ShapeDtypeStruct((B, 2, NSA_KV_HEADS, N_CMP_PAD, NSA_HEAD_DIM), F32),
        compiler_params=_cparams(("parallel", "parallel")),
        name="nsa_compress",
    )(xc, pe2, w1, w2)


def _alibi_slopes_np():
    return np.exp2(-(8.0 / NSA_HEADS) * np.arange(1, NSA_HEADS + 1)).astype(np.float32)


def _overlap_t():
    cs = np.arange(N_CMP) * CMP_STRIDE
    ss = np.arange(N_SEL) * SEL_BLOCK
    ov = np.clip(np.minimum(cs[:, None] + CMP_BLOCK, ss[None, :] + SEL_BLOCK)
                 - np.maximum(cs[:, None], ss[None, :]), 0, None).astype(np.float32) / CMP_BLOCK
    ovt = np.zeros((N_SEL, N_CMP_PAD), np.float32)
    ovt[:, :N_CMP] = ov.T
    return jnp.asarray(ovt)


def _q_alibi_table():
    slopes = _alibi_slopes_np().reshape(NSA_KV_HEADS, NSA_GROUP)
    pos = np.arange(SEQ)
    tab = np.zeros((NSA_KV_HEADS, SEQ, LANES), np.float32)
    for g in range(NSA_KV_HEADS):
        for r in range(NSA_GROUP):
            m = slopes[g, r]
            tab[g, :, r * 32 + 0] = -m * SEL_BLOCK * (pos // SEL_BLOCK)
            tab[g, :, r * 32 + 1] = -m * (pos % SEL_BLOCK)
            tab[g, :, r * 32 + 2] = m * SEL_BLOCK
            tab[g, :, r * 32 + 3] = m
    return jnp.asarray(tab)


def _head_select_mats():
    m = np.zeros((NSA_GROUP, NSA_HEAD_DIM, 256), np.float32)
    for r in range(NSA_GROUP):
        m[r, np.arange(64), r * 64 + np.arange(64)] = 1.0
    return jnp.asarray(m, dtype=BF16)


def _cmp_attn_kernel(slopes_ref, q_ref, kc_ref, vct_ref, ovt_ref, qal_ref, hs_ref,
                     ocmp_ref, qaug_ref, *, tq):
    g = pl.program_id(1)
    i = pl.program_id(2)
    q = q_ref[...]
    kc = kc_ref[0, 0].astype(BF16)
    vct = vct_ref[0, 0].astype(BF16)
    n_idx = lax.broadcasted_iota(jnp.int32, (N_CMP_PAD, tq), 0)
    t_idx = i * tq + lax.broadcasted_iota(jnp.int32, (N_CMP_PAD, tq), 1)
    dist = t_idx - (n_idx * CMP_STRIDE + (CMP_BLOCK - 1))
    visible = (dist >= 0) & (n_idx < N_CMP)
    distf = dist.astype(F32)
    p_sum = jnp.zeros((N_CMP_PAD, tq), F32)
    o_parts = []
    for r in range(NSA_GROUP):
        hs = hs_ref[r]
        k_ext = _dot(kc, hs).astype(BF16)
        s = _dot_nt(k_ext, q)
        s = s - slopes_ref[g * NSA_GROUP + r] * distf
        s = jnp.where(visible, s, NEG_INF)
        m = jnp.max(s, axis=0, keepdims=True)
        e = jnp.where(visible, jnp.exp(s - m), 0.0)
        l = jnp.sum(e, axis=0, keepdims=True)
        p = e / jnp.where(l > 0.0, l, 1.0)
        p_sum = p_sum + p
        o_parts.append(_dot(vct, p.astype(BF16)))
    ocmp_ref[...] = jnp.concatenate(o_parts, axis=0).T.astype(ocmp_ref.dtype)

    imp = jnp.dot(ovt_ref[...], p_sum, preferred_element_type=F32,
                  precision=lax.Precision.HIGHEST)
    j_idx = lax.broadcasted_iota(jnp.int32, (N_SEL, tq), 0)
    cur = (i * tq + lax.broadcasted_iota(jnp.int32, (N_SEL, tq), 1)) // SEL_BLOCK
    forced = (j_idx == 0) | (j_idx == cur) | (j_idx == cur - 1)
    imp = jnp.where(j_idx > cur, -1.0, imp + jnp.where(forced, FORCE_BONUS, 0.0))
    rank = jnp.zeros((N_SEL, tq), F32)
    for k in range(N_SEL):
        row = imp[k:k + 1, :]
        ge = jnp.where(row >= imp, 1.0, 0.0)
        gt = jnp.where(row > imp, 1.0, 0.0)
        rank = rank + jnp.where(j_idx > k, ge, gt)
    sel_bias = jnp.where(rank < float(SEL_TOPN), 0.0, -MASK_BIG)

    qal_t = qal_ref[0].T
    for r in range(NSA_GROUP):
        q_t = _dot_nt(hs_ref[r], q)
        aug_t = jnp.concatenate([q_t, sel_bias, qal_t[r * 32:(r + 1) * 32, :]], axis=0)
        qaug_ref[0, 0, r] = aug_t.T.astype(BF16)


def _cmp_attn(q, kc, vct, ovt, qal, hs, slopes, B, tq=256):
    nq = SEQ // tq
    kern = functools.partial(_cmp_attn_kernel, tq=tq)
    return pl.pallas_call(
        kern,
        grid_spec=pltpu.PrefetchScalarGridSpec(
            num_scalar_prefetch=1,
            grid=(B, NSA_KV_HEADS, nq),
            in_specs=[pl.BlockSpec((tq, 256), lambda b, g, i, s: (b * nq + i, g)),
                      pl.BlockSpec((1, 1, N_CMP_PAD, NSA_HEAD_DIM), lambda b, g, i, s: (b, g, 0, 0)),
                      pl.BlockSpec((1, 1, NSA_HEAD_DIM, N_CMP_PAD), lambda b, g, i, s: (b, g, 0, 0)),
                      pl.BlockSpec((N_SEL, N_CMP_PAD), lambda b, g, i, s: (0, 0)),
                      pl.BlockSpec((1, tq, LANES), lambda b, g, i, s: (g, i, 0)),
                      pl.BlockSpec((NSA_GROUP, NSA_HEAD_DIM, 256), lambda b, g, i, s: (0, 0, 0))],
            out_specs=[pl.BlockSpec((tq, 256), lambda b, g, i, s: (b * nq + i, g)),
                       pl.BlockSpec((1, 1, NSA_GROUP, tq, LANES), lambda b, g, i, s: (b, g, 0, i, 0))]),
        out_shape=[jax.ShapeDtypeStruct((B * SEQ, 512), BF16),
                   jax.ShapeDtypeStruct((B, NSA_KV_HEADS, NSA_GROUP, SEQ, LANES), BF16)],
        compiler_params=_cparams(("parallel", "parallel", "parallel")),
        name="nsa_cmp_attn_select",
    )(slopes, q, kc, vct, ovt, qal, hs)


def _place_mats():
    m = np.zeros((NSA_GROUP, LANES, 256), np.float32)
    for r in range(NSA_GROUP):
        m[r, np.arange(64), r * 64 + np.arange(64)] = 1.0
    return jnp.asarray(m, dtype=BF16)


def _flash_step(q, k, v, bias, m_ref, acc_ref):
    s = _dot_nt(q, k)
    if bias is not None:
        s = s + bias
    m_old = m_ref[...]
    m_new = jnp.maximum(m_old, jnp.max(s, axis=1, keepdims=True))
    alpha = jnp.exp(m_old - m_new)
    p = jnp.exp(s - m_new)
    acc_ref[...] = alpha * acc_ref[...] + _dot(p.astype(BF16), v)
    m_ref[...] = m_new


def _flash_finish(acc_ref, place_ref, o_ref, tq):
    acc = acc_ref[...]
    l = acc[:, NSA_HEAD_DIM:NSA_HEAD_DIM + 1]
    o = (acc / l).astype(BF16)
    out = jnp.zeros((tq, 256), F32)
    for r in range(NSA_GROUP):
        out = out + _dot(o[r * tq:(r + 1) * tq, :], place_ref[r])
    o_ref[...] = out.astype(o_ref.dtype)


def _sel_win_kernel(qaug_ref, ks_ref, vs_ref, kw_ref, vw_ref, place_ref,
                    osel_ref, owin_ref, m_ref, acc_ref, *, tq):
    i = pl.program_id(2)
    rows = NSA_GROUP * tq
    q = qaug_ref[0, 0].reshape(rows, LANES)
    a_idx = lax.broadcasted_iota(jnp.int32, (rows, tq), 0) % tq
    b_idx = lax.broadcasted_iota(jnp.int32, (rows, tq), 1)
    causal_bias = jnp.where(a_idx >= b_idx, 0.0, -MASK_BIG)
    tail_bias = jnp.where(a_idx < b_idx, 0.0, -MASK_BIG)

    def reset():
        m_ref[...] = jnp.full((rows, 1), -jnp.inf, F32)
        acc_ref[...] = jnp.zeros((rows, LANES), F32)

    reset()

    def sel_body(j, carry):
        off = pl.multiple_of(j * tq, tq)
        _flash_step(q, ks_ref[pl.ds(off, tq), :], vs_ref[pl.ds(off, tq), :], None, m_ref, acc_ref)
        return carry

    lax.fori_loop(0, i, sel_body, 0)
    off_d = pl.multiple_of(i * tq, tq)
    _flash_step(q, ks_ref[pl.ds(off_d, tq), :], vs_ref[pl.ds(off_d, tq), :], causal_bias,
                m_ref, acc_ref)
    _flash_finish(acc_ref, place_ref, osel_ref, tq)

    reset()
    n_back = WINDOW // tq
    _flash_step(q, kw_ref[pl.ds(off_d, tq), :], vw_ref[pl.ds(off_d, tq), :], causal_bias,
                m_ref, acc_ref)
    for d in range(1, n_back + 1):
        @pl.when(i >= d)
        def _():
            off = pl.multiple_of((i - d) * tq, tq)
            _flash_step(q, kw_ref[pl.ds(off, tq), :], vw_ref[pl.ds(off, tq), :],
                        tail_bias if d == n_back else None, m_ref, acc_ref)
    _flash_finish(acc_ref, place_ref, owin_ref, tq)


def _sel_win_attn(qaug, kv, place, B, tq=128):
    nq = SEQ // tq
    kern = functools.partial(_sel_win_kernel, tq=tq)
    rows = NSA_GROUP * tq

    def kv_spec(col):
        return pl.BlockSpec((SEQ, LANES), lambda b, g, i: (b, col + g))

    return pl.pallas_call(
        kern,
        grid=(B, NSA_KV_HEADS, nq),
        in_specs=[pl.BlockSpec((1, 1, NSA_GROUP, tq, LANES), lambda b, g, i: (b, g, 0, i, 0)),
                  kv_spec(0), kv_spec(2), kv_spec(4), kv_spec(6),
                  pl.BlockSpec((NSA_GROUP, LANES, 256), lambda b, g, i: (0, 0, 0))],
        out_specs=[pl.BlockSpec((tq, 256), lambda b, g, i: (b * nq + i, g)),
                   pl.BlockSpec((tq, 256), lambda b, g, i: (b * nq + i, g))],
        out_shape=[jax.ShapeDtypeStruct((B * SEQ, 512), BF16),
                   jax.ShapeDtypeStruct((B * SEQ, 512), BF16)],
        scratch_shapes=[pltpu.VMEM((rows, 1), F32), pltpu.VMEM((rows, LANES), F32)],
        compiler_params=_cparams(("parallel", "parallel", "arbitrary")),
        name="nsa_sel_win_attn",
    )(qaug, kv, kv, kv, kv, place)


def _gla_kernel(qk_ref, v_ref, ga_ref, gz_ref, wg_ref, bg_ref, ng_ref, o_ref, st_ref):
    C = GLA_CHUNK
    n_chunks = SEQ // C
    wg = wg_ref[...]
    bg = bg_ref[...]
    norm_g = ng_ref[...]
    r_idx = lax.broadcasted_iota(jnp.int32, (C, C), 0)
    c_idx = lax.broadcasted_iota(jnp.int32, (C, C), 1)
    causal = r_idx >= c_idx
    tri = jnp.where(causal, 1.0, 0.0).astype(F32)
    lane = lax.broadcasted_iota(jnp.int32, (C, GLA_HEADS * GLA_DK), 1)
    st_ref[...] = jnp.zeros(st_ref.shape, F32)

    def chunk(n, carry):
        off = pl.multiple_of(n * C, C)
        q = qk_ref[pl.ds(off, C), 0:256] * (GLA_DK ** -0.5)
        k = qk_ref[pl.ds(off, C), 256:512]
        ga = ga_ref[pl.ds(off, C), :].astype(BF16)
        x = _dot(ga, wg) + bg
        log_a = -(jnp.maximum(-x, 0.0) + jnp.log1p(jnp.exp(-jnp.abs(x)))) / GLA_TAU
        b = jnp.dot(tri, log_a, preferred_element_type=F32, precision=lax.Precision.HIGHEST)
        b_last = b[C - 1:C, :]
        q_d = q * jnp.exp(b)
        k_d = (k * jnp.exp(-b)).astype(BF16)
        k_u = (k * jnp.exp(b_last - b)).astype(BF16)
        dec = jnp.exp(b_last)
        for h in range(GLA_HEADS):
            in_head = (lane >= h * GLA_DK) & (lane < (h + 1) * GLA_DK)
            q_h = jnp.where(in_head, q_d, 0.0).astype(BF16)
            v_h = v_ref[pl.ds(off, C), h * GLA_DV:(h + 1) * GLA_DV].astype(BF16)
            a = jnp.where(causal, _dot_nt(q_h, k_d), 0.0)
            o = _dot(a.astype(BF16), v_h)
            st = st_ref[h]
            o = o + _dot_nt(q_h, st.astype(BF16))
            upd = lax.dot_general(v_h, k_u, (((0,), (0,)), ((), ())), preferred_element_type=F32)
            st_ref[h] = st * dec + upd
            y = o * lax.rsqrt(jnp.mean(o * o, axis=-1, keepdims=True) + NORM_EPS) * norm_g
            y = y * _silu(gz_ref[pl.ds(off, C), h * GLA_DV:(h + 1) * GLA_DV])
            o_ref[pl.ds(off, C), h * GLA_DV:(h + 1) * GLA_DV] = y.astype(o_ref.dtype)
        return carry

    lax.fori_loop(0, n_chunks, chunk, 0)


def _gla(gqk, gv, ga, gz, wg, bg, norm_g, B):
    row = lambda b: (b, 0)
    const = lambda b: (0, 0)
    return pl.pallas_call(
        _gla_kernel,
        grid=(B,),
        in_specs=[pl.BlockSpec((SEQ, 512), row), pl.BlockSpec((SEQ, 512), row),
                  pl.BlockSpec((SEQ, LANES), row), pl.BlockSpec((SEQ, 512), row),
                  pl.BlockSpec((LANES, 256), const), pl.BlockSpec((1, 256), const),
                  pl.BlockSpec((1, GLA_DV), const)],
        out_specs=pl.BlockSpec((SEQ, 512), row),
        out_shape=jax.ShapeDtypeStruct((B * SEQ, 512), BF16),
        scratch_shapes=[pltpu.VMEM((GLA_HEADS, GLA_DV, GLA_HEADS * GLA_DK), F32)],
        compiler_params=_cparams(("parallel",)),
        name="gla_chunked",
    )(gqk, gv, ga, gz, wg, bg, norm_g)


def _deepnorm_ln(x, y, g, b):
    r = DEEPNORM_ALPHA * x + y
    mu = jnp.mean(r, axis=-1, keepdims=True)
    d = r - mu
    var = jnp.mean(d * d, axis=-1, keepdims=True)
    return d * lax.rsqrt(var + NORM_EPS) * g + b


def _gate_expand_mat():
    e = np.zeros((LANES, 3 * 512), np.float32)
    for br in range(3):
        for h in range(NSA_HEADS):
            e[br * 8 + h, br * 512 + h * 64: br * 512 + (h + 1) * 64] = 1.0
    return jnp.asarray(e)


def _even_out_kernel(ocmp_ref, osel_ref, owin_ref, ng_ref, nz_ref, ogla_ref, x_ref,
                     e_ref, w_ref, g_ref, b_ref, o_ref):
    sig = jax.nn.sigmoid(ng_ref[...])
    gates = jnp.dot(sig, e_ref[...], preferred_element_type=F32, precision=lax.Precision.HIGHEST)
    o_nsa = (gates[:, 0:512] * ocmp_ref[...].astype(F32)
             + gates[:, 512:1024] * osel_ref[...].astype(F32)
             + gates[:, 1024:1536] * owin_ref[...].astype(F32))
    o_nsa = (o_nsa * _silu(nz_ref[...])).astype(BF16)
    y = _dot(o_nsa, w_ref[0:512, :]) + _dot(ogla_ref[...], w_ref[512:1024, :])
    o_ref[...] = _deepnorm_ln(x_ref[...], y, g_ref[...], b_ref[...])


def _even_out(ocmp, osel, owin, ng, nz, ogla, x2, e_mat, w_bf, ln_g, ln_b, tm=256):
    T = x2.shape[0]
    row = lambda i: (i, 0)
    const = lambda i: (0, 0)
    return pl.pallas_call(
        _even_out_kernel,
        grid=(T // tm,),
        in_specs=[pl.BlockSpec((tm, 512), row), pl.BlockSpec((tm, 512), row),
                  pl.BlockSpec((tm, 512), row), pl.BlockSpec((tm, LANES), row),
                  pl.BlockSpec((tm, 512), row), pl.BlockSpec((tm, 512), row),
                  pl.BlockSpec((tm, D_MODEL), row),
                  pl.BlockSpec((LANES, 1536), const), pl.BlockSpec((1024, D_MODEL), const),
                  pl.BlockSpec((1, D_MODEL), const), pl.BlockSpec((1, D_MODEL), const)],
        out_specs=pl.BlockSpec((tm, D_MODEL), row),
        out_shape=jax.ShapeDtypeStruct((T, D_MODEL), F32),
        compiler_params=_cparams(("parallel",)),
        name="even_out_proj_ln",
    )(ocmp, osel, owin, ng, nz, ogla, x2, e_mat, w_bf, ln_g, ln_b)


def _odd_out_kernel(o_ref_in, z_ref, x_ref, w_ref, g_ref, b_ref, o_ref):
    o = (o_ref_in[...].astype(F32) * _silu(z_ref[...])).astype(BF16)
    y = _dot(o, w_ref[...])
    o_ref[...] = _deepnorm_ln(x_ref[...], y, g_ref[...], b_ref[...])


def _odd_out(o, z, x2, w_bf, ln_g, ln_b, tm=256):
    T = x2.shape[0]
    row = lambda i: (i, 0)
    const = lambda i: (0, 0)
    return pl.pallas_call(
        _odd_out_kernel,
        grid=(T // tm,),
        in_specs=[pl.BlockSpec((tm, D_MODEL), row), pl.BlockSpec((tm, D_MODEL), row),
                  pl.BlockSpec((tm, D_MODEL), row), pl.BlockSpec((D_MODEL, D_MODEL), const),
                  pl.BlockSpec((1, D_MODEL), const), pl.BlockSpec((1, D_MODEL), const)],
        out_specs=pl.BlockSpec((tm, D_MODEL), row),
        out_shape=jax.ShapeDtypeStruct((T, D_MODEL), F32),
        compiler_params=_cparams(("parallel",)),
        name="odd_out_proj_ln",
    )(o, z, x2, w_bf, ln_g, ln_b)


_HEAD_W = LANES
_QK_W = MLA_HEADS * _HEAD_W


def _rot_cols(w):
    half = MLA_ROPE // 2
    return jnp.concatenate([-w[..., half:], w[..., :half]], axis=-1)


def _odd_weights(w_in, w_uq, w_ukv):
    o = _O_OFF
    kr = w_in[:, o[2]:o[3]]
    w1 = jnp.concatenate([w_in[:, o[0]:o[2]], kr, _rot_cols(kr),
                          jnp.zeros((D_MODEL, LANES - 2 * MLA_ROPE), F32),
                          w_in[:, o[3]:o[4]]], axis=1).astype(BF16)
    uq = w_uq.reshape(MLA_Q_LORA, MLA_HEADS, MLA_NOPE + MLA_ROPE)
    zq = jnp.zeros((MLA_Q_LORA, MLA_HEADS, _HEAD_W - MLA_NOPE - MLA_ROPE), F32)
    wq_a = jnp.concatenate([uq, zq], axis=-1).reshape(MLA_Q_LORA, _QK_W).astype(BF16)
    wq_b = jnp.concatenate([jnp.zeros((MLA_Q_LORA, MLA_HEADS, MLA_NOPE), F32),
                            _rot_cols(uq[..., MLA_NOPE:]), zq], axis=-1)
    wq_b = wq_b.reshape(MLA_Q_LORA, _QK_W).astype(BF16)
    ukv = w_ukv.reshape(MLA_KV_LORA, MLA_HEADS, MLA_NOPE + MLA_V)
    wk = jnp.concatenate([ukv[..., :MLA_NOPE],
                          jnp.zeros((MLA_KV_LORA, MLA_HEADS, _HEAD_W - MLA_NOPE), F32)], axis=-1)
    wk = wk.reshape(MLA_KV_LORA, _QK_W).astype(BF16)
    wv = ukv[..., MLA_NOPE:].reshape(MLA_KV_LORA, MLA_HEADS * MLA_V).astype(BF16)
    return w1, wq_a, wq_b, wk, wv


def _rope_tables():
    half = MLA_ROPE // 2
    freqs = jnp.exp(-math.log(ROPE_THETA) * jnp.arange(half, dtype=F32) * 2.0 / MLA_ROPE)
    ang = jnp.arange(SEQ, dtype=F32)[:, None] * freqs[None, :]
    cos = jnp.concatenate([jnp.cos(ang), jnp.cos(ang)], axis=1)
    sin = jnp.concatenate([jnp.sin(ang), jnp.sin(ang)], axis=1)
    scale = (MLA_NOPE + MLA_ROPE) ** -0.5
    pad = jnp.zeros((SEQ, LANES - MLA_NOPE - MLA_ROPE), F32)
    qc = jnp.concatenate([jnp.full((SEQ, MLA_NOPE), scale, F32), cos * scale, pad], axis=1)
    qs = jnp.concatenate([jnp.zeros((SEQ, MLA_NOPE), F32), sin * scale, pad], axis=1)
    kcs = jnp.concatenate([cos, sin, jnp.zeros((SEQ, LANES - 2 * MLA_ROPE), F32)], axis=1)
    place = np.zeros((LANES, _QK_W), np.float32)
    for h in range(MLA_HEADS):
        for c in range(MLA_ROPE):
            place[c, h * _HEAD_W + MLA_NOPE + c] = 1.0
            place[MLA_ROPE + c, h * _HEAD_W + MLA_NOPE + c] = 1.0
    return qc, qs, kcs, jnp.asarray(place, dtype=BF16)


def _rms(x, g):
    return x * lax.rsqrt(jnp.mean(x * x, axis=-1, keepdims=True) + NORM_EPS) * g


def _odd_in_kernel(x_ref, w1_ref, wqa_ref, wqb_ref, wk_ref, wv_ref, place_ref,
                   qn_ref, kn_ref, qc_ref, qs_ref, kcs_ref,
                   q_ref, k_ref, v_ref, z_ref):
    xb = x_ref[...].astype(BF16)
    c_q = _dot(xb, w1_ref[:, 0:MLA_Q_LORA])
    c_kv = _dot(xb, w1_ref[:, MLA_Q_LORA:640])
    kr = _dot(xb, w1_ref[:, 640:768])
    z_ref[...] = _dot(xb, w1_ref[:, 768:1792])
    cqn = _rms(c_q, qn_ref[...]).astype(BF16)
    ckvn = _rms(c_kv, kn_ref[...]).astype(BF16)
    qa = _dot(cqn, wqa_ref[...])
    qb = _dot(cqn, wqb_ref[...])
    qc = qc_ref[...]
    qs = qs_ref[...]
    for h in range(MLA_HEADS):
        sl = slice(h * _HEAD_W, (h + 1) * _HEAD_W)
        q_ref[:, sl] = (qa[:, sl] * qc + qb[:, sl] * qs).astype(BF16)
    kr_r = (kr * kcs_ref[...]).astype(BF16)
    k_ref[...] = (_dot(ckvn, wk_ref[...]) + _dot(kr_r, place_ref[...])).astype(BF16)
    v_ref[...] = _dot(ckvn, wv_ref[...]).astype(BF16)


def _odd_in_proj(x2, w1, wqa, wqb, wk, wv, place, qn, kn, qc, qs, kcs, tm=256):
    T = x2.shape[0]
    s_tiles = SEQ // tm
    row = lambda i: (i, 0)
    const = lambda i: (0, 0)
    pos = lambda i: (i % s_tiles, 0)

    def full(a):
        return pl.BlockSpec(a.shape, const)

    return pl.pallas_call(
        _odd_in_kernel,
        grid=(T // tm,),
        in_specs=[pl.BlockSpec((tm, D_MODEL), row), full(w1), full(wqa), full(wqb), full(wk),
                  full(wv), full(place), full(qn), full(kn),
                  pl.BlockSpec((tm, LANES), pos), pl.BlockSpec((tm, LANES), pos),
                  pl.BlockSpec((tm, LANES), pos)],
        out_specs=[pl.BlockSpec((tm, _QK_W), row), pl.BlockSpec((tm, _QK_W), row),
                   pl.BlockSpec((tm, 1024), row), pl.BlockSpec((tm, 1024), row)],
        out_shape=[jax.ShapeDtypeStruct((T, _QK_W), BF16), jax.ShapeDtypeStruct((T, _QK_W), BF16),
                   jax.ShapeDtypeStruct((T, 1024), BF16), jax.ShapeDtypeStruct((T, 1024), F32)],
        compiler_params=_cparams(("parallel",)),
        name="odd_in_proj",
    )(x2, w1, wqa, wqb, wk, wv, place, qn, kn, qc, qs, kcs)


def _mla_kernel(q_ref, k_ref, v_ref, o_ref, m_ref, l_ref, acc_ref, *, tq):
    i = pl.program_id(2)
    a_idx = lax.broadcasted_iota(jnp.int32, (tq, tq), 0)
    b_idx = lax.broadcasted_iota(jnp.int32, (tq, tq), 1)
    causal = a_idx >= b_idx
    m_ref[...] = jnp.full(m_ref.shape, -jnp.inf, F32)
    l_ref[...] = jnp.zeros(l_ref.shape, F32)
    acc_ref[...] = jnp.zeros(acc_ref.shape, F32)

    def step(off, masked):
        v = v_ref[pl.ds(off, tq), :]
        for h in range(2):
            q = q_ref[:, h * _HEAD_W:(h + 1) * _HEAD_W]
            k = k_ref[pl.ds(off, tq), h * _HEAD_W:(h + 1) * _HEAD_W]
            s = _dot_nt(q, k)
            if masked:
                s = jnp.where(causal, s, NEG_INF)
            m_old = m_ref[h]
            m_new = jnp.maximum(m_old, jnp.max(s, axis=1, keepdims=True))
            alpha = jnp.exp(m_old - m_new)
            p = jnp.exp(s - m_new)
            l_ref[h] = alpha * l_ref[h] + jnp.sum(p, axis=1, keepdims=True)
            acc_ref[h] = alpha * acc_ref[h] + _dot(p.astype(BF16), v)
            m_ref[h] = m_new

    def body(j, carry):
        step(pl.multiple_of(j * tq, tq), False)
        return carry

    lax.fori_loop(0, i, body, 0)
    step(pl.multiple_of(i * tq, tq), True)
    lane = lax.broadcasted_iota(jnp.int32, (tq, LANES), 1)
    o0 = acc_ref[0] / l_ref[0]
    o1 = acc_ref[1] / l_ref[1]
    o_ref[...] = jnp.where(lane < MLA_V, o0, o1).astype(o_ref.dtype)


def _mla_attn(q, k, v, B, tq=256):
    nq = SEQ // tq
    kern = functools.partial(_mla_kernel, tq=tq)
    return pl.pallas_call(
        kern,
        grid=(B, MLA_HEADS // 2, nq),
        in_specs=[pl.BlockSpec((tq, 2 * _HEAD_W), lambda b, h, i: (b * nq + i, h)),
                  pl.BlockSpec((SEQ, 2 * _HEAD_W), lambda b, h, i: (b, h)),
                  pl.BlockSpec((SEQ, LANES), lambda b, h, i: (b, h))],
        out_specs=pl.BlockSpec((tq, LANES), lambda b, h, i: (b * nq + i, h)),
        out_shape=jax.ShapeDtypeStruct((B * SEQ, MLA_HEADS * MLA_V), BF16),
        scratch_shapes=[pltpu.VMEM((2, tq, 1), F32), pltpu.VMEM((2, tq, 1), F32),
                        pltpu.VMEM((2, tq, LANES), F32)],
        compiler_params=_cparams(("parallel", "parallel", "arbitrary")),
        name="mla_flash_attn",
    )(q, k, v)


def _even_layer(x2, B, w_in, cmp_pe, cmp_w1, cmp_w2, gla_w_gate, gla_b_gate, gla_norm,
                w_out, ln_g, ln_b):
    T = x2.shape[0]
    q, kvc, kv, ng, nz, gqk, gv, ga, gz = _even_in_proj(x2, _even_in_weight(w_in), _nsa_kv_table())

    xc = kvc.reshape(B, N_CMP_PAD, CMP_STRIDE, 2, NSA_KV_HEADS, NSA_HEAD_DIM)
    xc = xc.transpose(0, 3, 4, 1, 2, 5).reshape(B, 2, NSA_KV_HEADS, N_CMP_PAD, CMP_STRIDE * NSA_HEAD_DIM)
    pe2 = cmp_pe.reshape(2, 2, CMP_STRIDE * NSA_HEAD_DIM)
    kvcmp = _compress(xc, pe2, cmp_w1.astype(BF16), cmp_w2.astype(BF16))
    kc = kvcmp[:, 0]
    vct = jnp.swapaxes(kvcmp[:, 1], -1, -2)

    slopes = jnp.asarray(_alibi_slopes_np())
    ocmp, qaug = _cmp_attn(q, kc, vct, _overlap_t(), _q_alibi_table(), _head_select_mats(),
                           slopes, B)
    kv2 = kv.reshape(B * SEQ, 1024)
    osel, owin = _sel_win_attn(qaug, kv2, _place_mats(), B)

    wg = jnp.pad(gla_w_gate, ((0, LANES - GLA_GATE_RANK), (0, 0))).astype(BF16)
    ogla = _gla(gqk, gv, ga, gz, wg, gla_b_gate.reshape(1, -1), gla_norm.reshape(1, -1), B)

    return _even_out(ocmp, osel, owin, ng, nz, ogla, x2, _gate_expand_mat(), w_out.astype(BF16),
                     ln_g.reshape(1, -1), ln_b.reshape(1, -1))


def _odd_layer(x2, B, w_in, q_norm, w_uq, kv_norm, w_ukv, w_out, ln_g, ln_b):
    w1, wqa, wqb, wk, wv = _odd_weights(w_in, w_uq, w_ukv)
    qc, qs, kcs, place = _rope_tables()
    q, k, v, z = _odd_in_proj(x2, w1, wqa, wqb, wk, wv, place, q_norm.reshape(1, -1),
                              kv_norm.reshape(1, -1), qc, qs, kcs)
    o = _mla_attn(q, k, v, B)
    return _odd_out(o, z, x2, w_out.astype(BF16), ln_g.reshape(1, -1), ln_b.reshape(1, -1))


def kernel(x, e_w_in, e_cmp_pe, e_cmp_w1, e_cmp_w2, e_gla_w_gate, e_gla_b_gate, e_gla_norm,
           e_w_out, e_ln_g, e_ln_b, o_w_in, o_q_norm, o_w_uq, o_kv_norm, o_w_ukv, o_w_out,
           o_ln_g, o_ln_b):
    B, S, D = x.shape
    x2 = x.reshape(B * S, D)
    for layer in range(DEPTH):
        i = layer // 2
        if layer % 2 == 0:
            x2 = _even_layer(x2, B, e_w_in[i], e_cmp_pe[i], e_cmp_w1[i], e_cmp_w2[i],
                             e_gla_w_gate[i], e_gla_b_gate[i], e_gla_norm[i], e_w_out[i],
                             e_ln_g[i], e_ln_b[i])
        else:
            x2 = _odd_layer(x2, B, o_w_in[i], o_q_norm[i], o_w_uq[i], o_kv_norm[i], o_w_ukv[i],
                            o_w_out[i], o_ln_g[i], o_ln_b[i])
    return x2.reshape(B, S, D)
```

```python
import functools
import math

import jax
import jax.numpy as jnp
import numpy as np
from jax import lax
from jax.experimental import pallas as pl
from jax.experimental.pallas import tpu as pltpu

F32 = jnp.float32
BF16 = jnp.bfloat16

D_MODEL = 1024
SEQ = 2048
DEPTH = 2
NSA_HEADS = 8
NSA_KV_HEADS = 2
NSA_GROUP = 4
NSA_HEAD_DIM = 64
CMP_BLOCK = 32
CMP_STRIDE = 16
CMP_HIDDEN = 128
N_CMP = (SEQ - CMP_BLOCK) // CMP_STRIDE + 1
N_CMP_PAD = 128
SEL_BLOCK = 64
N_SEL = SEQ // SEL_BLOCK
SEL_TOPN = 8
WINDOW = 512
FORCE_BONUS = 1000.0
GLA_HEADS = 4
GLA_DK = 64
GLA_DV = 128
GLA_GATE_RANK = 16
GLA_TAU = 16.0
GLA_CHUNK = 64
MLA_HEADS = 16
MLA_NOPE = 64
MLA_ROPE = 32
MLA_V = 64
MLA_Q_LORA = 384
MLA_KV_LORA = 256
ROPE_THETA = 10000.0
NORM_EPS = 1e-5
NEG_INF = -1e30
DEEPNORM_ALPHA = (2 * DEPTH) ** 0.25

LANES = 128
MASK_BIG = 2.0 ** 100
VMEM_LIMIT = 56 * 1024 * 1024

_E_OFF = np.cumsum([0, 512, 768, 24, 512, 256, 256, 512, 16, 512])
_O_OFF = np.cumsum([0, MLA_Q_LORA, MLA_KV_LORA, MLA_ROPE, MLA_HEADS * MLA_V])


def _cparams(sem):
    return pltpu.CompilerParams(dimension_semantics=sem, vmem_limit_bytes=VMEM_LIMIT)


def _silu(x):
    return x * jax.nn.sigmoid(x)


def _dot(a, b):
    return jnp.dot(a, b, preferred_element_type=F32)


def _dot_nt(a, b):
    return lax.dot_general(a, b, (((1,), (1,)), ((), ())), preferred_element_type=F32)


_A_SEGS = (("q", 512, BF16), ("kvc", 256, F32), ("kv", 1024, BF16), ("ng", 128, F32),
           ("nz", 512, F32), ("gqk", 512, F32), ("gv", 512, F32), ("ga", 128, F32),
           ("gz", 512, F32))
_A_OFF = np.cumsum([0] + [s[1] for s in _A_SEGS])


def _even_in_weight(w):
    o = _E_OFF
    z64 = jnp.zeros((D_MODEL, 64), F32)
    nkv = w[:, o[1]:o[2]]

    def kvcol(br, j, g):
        c = ((br * 2 + j) * NSA_KV_HEADS + g) * 64
        return nkv[:, c:c + 64]

    kv_blocks = []
    for br in (1, 2):
        for j in (0, 1):
            for g in (0, 1):
                kv_blocks += [kvcol(br, j, g), z64]
    ng = jnp.pad(w[:, o[2]:o[3]], ((0, 0), (0, LANES - 24)))
    ga = jnp.pad(w[:, o[7]:o[8]], ((0, 0), (0, LANES - GLA_GATE_RANK)))
    cols = [w[:, o[0]:o[1]] * (NSA_HEAD_DIM ** -0.5), nkv[:, :256]] + kv_blocks + [
        ng, w[:, o[3]:o[4]], w[:, o[4]:o[6]], w[:, o[6]:o[7]], ga, w[:, o[8]:o[9]]]
    return jnp.concatenate(cols, axis=1).astype(BF16)


def _nsa_kv_table():
    pos = np.arange(SEQ)
    k_al = np.zeros((SEQ, LANES), np.float32)
    k_al[:, 96] = 1.0
    k_al[:, 97] = 1.0
    k_al[:, 98] = pos // SEL_BLOCK
    k_al[:, 99] = pos % SEL_BLOCK
    k_sel = k_al.copy()
    k_sel[pos, 64 + pos // SEL_BLOCK] = 1.0
    v_one = np.zeros((SEQ, LANES), np.float32)
    v_one[:, 64] = 1.0
    tab = np.concatenate([k_sel, k_sel, v_one, v_one, k_al, k_al, v_one, v_one], axis=1)
    return jnp.asarray(tab)


def _even_in_kernel(x_ref, w_ref, tab_ref, *out_refs):
    xb = x_ref[...].astype(BF16)
    for (name, width, dt), off, o_ref in zip(_A_SEGS, _A_OFF[:-1], out_refs):
        r = _dot(xb, w_ref[:, off:off + width])
        if name == "kv":
            r = r + tab_ref[...]
        o_ref[...] = r.astype(dt)


def _even_in_proj(x2, w_bf, tab, tm=256):
    T = x2.shape[0]
    n_w = w_bf.shape[1]
    s_tiles = SEQ // tm
    out_shape = [jax.ShapeDtypeStruct((T, wd), dt) for _, wd, dt in _A_SEGS]
    out_specs = [pl.BlockSpec((tm, wd), lambda i: (i, 0)) for _, wd, _ in _A_SEGS]
    return pl.pallas_call(
        _even_in_kernel,
        grid=(T // tm,),
        in_specs=[pl.BlockSpec((tm, D_MODEL), lambda i: (i, 0)),
                  pl.BlockSpec((D_MODEL, n_w), lambda i: (0, 0)),
                  pl.BlockSpec((tm, 1024), lambda i: (i % s_tiles, 0))],
        out_specs=out_specs,
        out_shape=out_shape,
        compiler_params=_cparams(("parallel",)),
        name="even_in_proj",
    )(x2, w_bf, tab)


def _compress_kernel(x_ref, pe_ref, w1_ref, w2_ref, o_ref):
    w1_lo = w1_ref[0, :1024, :]
    w1_hi = w1_ref[0, 1024:, :]
    w2 = w2_ref[0]
    for g in range(NSA_KV_HEADS):
        xg = x_ref[0, 0, g]
        a = _dot((xg + pe_ref[0, 0:1, :]).astype(BF16), w1_lo)
        b = _dot((xg + pe_ref[0, 1:2, :]).astype(BF16), w1_hi)
        h = a + pltpu.roll(b, N_CMP_PAD - 1, 0)
        o_ref[0, 0, g] = _dot(_silu(h).astype(BF16), w2)


def _compress(xc, pe2, w1, w2):
    B = xc.shape[0]
    return pl.pallas_call(
        _compress_kernel,
        grid=(B, 2),
        in_specs=[pl.BlockSpec((1, 1, NSA_KV_HEADS, N_CMP_PAD, 1024), lambda b, j: (b, j, 0, 0, 0)),
                  pl.BlockSpec((1, 2, 1024), lambda b, j: (j, 0, 0)),
                  pl.BlockSpec((1, 2048, CMP_HIDDEN), lambda b, j: (j, 0, 0)),
                  pl.BlockSpec((1, CMP_HIDDEN, NSA_HEAD_DIM), lambda b, j: (j, 0, 0))],
        out_specs=pl.BlockSpec((1, 1, NSA_KV_HEADS, N_CMP_PAD, NSA_HEAD_DIM),
                               lambda b, j: (b, j, 0, 0, 0)),
        out_shape=jax.ShapeDtypeStruct((B, 2, NSA_KV_HEADS, N_CMP_PAD, NSA_HEAD_DIM), F32),
        compiler_params=_cparams(("parallel", "parallel")),
        name="nsa_compress",
    )(xc, pe2, w1, w2)


def _alibi_slopes_np():
    return np.exp2(-(8.0 / NSA_HEADS) * np.arange(1, NSA_HEADS + 1)).astype(np.float32)


def _overlap_t():
    cs = np.arange(N_CMP) * CMP_STRIDE
    ss = np.arange(N_SEL) * SEL_BLOCK
    ov = np.clip(np.minimum(cs[:, None] + CMP_BLOCK, ss[None, :] + SEL_BLOCK)
                 - np.maximum(cs[:, None], ss[None, :]), 0, None).astype(np.float32) / CMP_BLOCK
    ovt = np.zeros((N_SEL, N_CMP_PAD), np.float32)
    ovt[:, :N_CMP] = ov.T
    return jnp.asarray(ovt)


def _q_alibi_table():
    slopes = _alibi_slopes_np().reshape(NSA_KV_HEADS, NSA_GROUP)
    pos = np.arange(SEQ)
    tab = np.zeros((NSA_KV_HEADS, SEQ, LANES), np.float32)
    for g in range(NSA_KV_HEADS):
        for r in range(NSA_GROUP):
            m = slopes[g, r]
            tab[g, :, r * 32 + 0] = -m * SEL_BLOCK * (pos // SEL_BLOCK)
            tab[g, :, r * 32 + 1] = -m * (pos % SEL_BLOCK)
            tab[g, :, r * 32 + 2] = m * SEL_BLOCK
            tab[g, :, r * 32 + 3] = m
    return jnp.asarray(tab)


def _head_select_mats():
    m = np.zeros((NSA_GROUP, NSA_HEAD_DIM, 256), np.float32)
    for r in range(NSA_GROUP):
        m[r, np.arange(64), r * 64 + np.arange(64)] = 1.0
    return jnp.asarray(m, dtype=BF16)


def _cmp_attn_kernel(slopes_ref, q_ref, kc_ref, vct_ref, ovt_ref, qal_ref, hs_ref,
                     ocmp_ref, qaug_ref, *, tq):
    g = pl.program_id(1)
    i = pl.program_id(2)
    q = q_ref[...]
    kc = kc_ref[0, 0].astype(BF16)
    vct = vct_ref[0, 0].astype(BF16)
    n_idx = lax.broadcasted_iota(jnp.int32, (N_CMP_PAD, tq), 0)
    t_idx = i * tq + lax.broadcasted_iota(jnp.int32, (N_CMP_PAD, tq), 1)
    dist = t_idx - (n_idx * CMP_STRIDE + (CMP_BLOCK - 1))
    visible = (dist >= 0) & (n_idx < N_CMP)
    distf = dist.astype(F32)
    p_sum = jnp.zeros((N_CMP_PAD, tq), F32)
    o_parts = []
    for r in range(NSA_GROUP):
        hs = hs_ref[r]
        k_ext = _dot(kc, hs).astype(BF16)
        s = _dot_nt(k_ext, q)
        s = s - slopes_ref[g * NSA_GROUP + r] * distf
        s = jnp.where(visible, s, NEG_INF)
        m = jnp.max(s, axis=0, keepdims=True)
        e = jnp.where(visible, jnp.exp(s - m), 0.0)
        l = jnp.sum(e, axis=0, keepdims=True)
        p = e / jnp.where(l > 0.0, l, 1.0)
        p_sum = p_sum + p
        o_parts.append(_dot(vct, p.astype(BF16)))
    ocmp_ref[...] = jnp.concatenate(o_parts, axis=0).T.astype(ocmp_ref.dtype)

    imp = jnp.dot(ovt_ref[...], p_sum, preferred_element_type=F32,
                  precision=lax.Precision.HIGHEST)
    j_idx = lax.broadcasted_iota(jnp.int32, (N_SEL, tq), 0)
    cur = (i * tq + lax.broadcasted_iota(jnp.int32, (N_SEL, tq), 1)) // SEL_BLOCK
    forced = (j_idx == 0) | (j_idx == cur) | (j_idx == cur - 1)
    imp = jnp.where(j_idx > cur, -1.0, imp + jnp.where(forced, FORCE_BONUS, 0.0))
    rank = jnp.zeros((N_SEL, tq), F32)
    for k in range(N_SEL):
        row = imp[k:k + 1, :]
        ge = jnp.where(row >= imp, 1.0, 0.0)
        gt = jnp.where(row > imp, 1.0, 0.0)
        rank = rank + jnp.where(j_idx > k, ge, gt)
    sel_bias = jnp.where(rank < float(SEL_TOPN), 0.0, -MASK_BIG)

    qal_t = qal_ref[0].T
    for r in range(NSA_GROUP):
        q_t = _dot_nt(hs_ref[r], q)
        aug_t = jnp.concatenate([q_t, sel_bias, qal_t[r * 32:(r + 1) * 32, :]], axis=0)
        qaug_ref[0, 0, r] = aug_t.T.astype(BF16)


def _cmp_attn(q, kc, vct, ovt, qal, hs, slopes, B, tq=256):
    nq = SEQ // tq
    kern = functools.partial(_cmp_attn_kernel, tq=tq)
    return pl.pallas_call(
        kern,
        grid_spec=pltpu.PrefetchScalarGridSpec(
            num_scalar_prefetch=1,
            grid=(B, NSA_KV_HEADS, nq),
            in_specs=[pl.BlockSpec((tq, 256), lambda b, g, i, s: (b * nq + i, g)),
                      pl.BlockSpec((1, 1, N_CMP_PAD, NSA_HEAD_DIM), lambda b, g, i, s: (b, g, 0, 0)),
                      pl.BlockSpec((1, 1, NSA_HEAD_DIM, N_CMP_PAD), lambda b, g, i, s: (b, g, 0, 0)),
                      pl.BlockSpec((N_SEL, N_CMP_PAD), lambda b, g, i, s: (0, 0)),
                      pl.BlockSpec((1, tq, LANES), lambda b, g, i, s: (g, i, 0)),
                      pl.BlockSpec((NSA_GROUP, NSA_HEAD_DIM, 256), lambda b, g, i, s: (0, 0, 0))],
            out_specs=[pl.BlockSpec((tq, 256), lambda b, g, i, s: (b * nq + i, g)),
                       pl.BlockSpec((1, 1, NSA_GROUP, tq, LANES), lambda b, g, i, s: (b, g, 0, i, 0))]),
        out_shape=[jax.ShapeDtypeStruct((B * SEQ, 512), BF16),
                   jax.ShapeDtypeStruct((B, NSA_KV_HEADS, NSA_GROUP, SEQ, LANES), BF16)],
        compiler_params=_cparams(("parallel", "parallel", "parallel")),
        name="nsa_cmp_attn_select",
    )(slopes, q, kc, vct, ovt, qal, hs)


def _place_mats():
    m = np.zeros((NSA_GROUP, LANES, 256), np.float32)
    for r in range(NSA_GROUP):
        m[r, np.arange(64), r * 64 + np.arange(64)] = 1.0
    return jnp.asarray(m, dtype=BF16)


def _sel_win_kernel(qaug_ref, ks_ref, vs_ref, kw_ref, vw_ref, place_ref,
                    osel_ref, owin_ref, s_ref, bias_ref, *, tq):
    nq = SEQ // tq
    n_back = WINDOW // tq
    a_idx = lax.broadcasted_iota(jnp.int32, (tq, tq), 0)
    b_idx = lax.broadcasted_iota(jnp.int32, (tq, tq), 1)
    bias_ref[0] = jnp.where(a_idx >= b_idx, 0.0, -MASK_BIG)
    bias_ref[1] = jnp.where(a_idx < b_idx, 0.0, -MASK_BIG)

    def attend(q, k_ref, v_ref, tiles):
        mx = None
        for n, (j, bias) in enumerate(tiles):
            s = _dot_nt(q, k_ref[j * tq:(j + 1) * tq, :])
            if bias is not None:
                s = s + jnp.concatenate([bias_ref[bias]] * NSA_GROUP, axis=0)
            s_ref[:, n * tq:(n + 1) * tq] = s
            t = s[:, 0:LANES]
            for c in range(1, tq // LANES):
                t = jnp.maximum(t, s[:, c * LANES:(c + 1) * LANES])
            mx = t if mx is None else jnp.maximum(mx, t)
        m = jnp.max(mx, axis=1, keepdims=True)
        acc = None
        for n, (j, _) in enumerate(tiles):
            p = jnp.exp(s_ref[:, n * tq:(n + 1) * tq] - m).astype(BF16)
            pv = _dot(p, v_ref[j * tq:(j + 1) * tq, :])
            acc = pv if acc is None else acc + pv
        o = (acc / acc[:, NSA_HEAD_DIM:NSA_HEAD_DIM + 1]).astype(BF16)
        out = jnp.zeros((tq, 256), F32)
        for r in range(NSA_GROUP):
            out = out + _dot(o[r * tq:(r + 1) * tq, :], place_ref[r])
        return out

    for i in range(nq):
        rows = slice(i * tq, (i + 1) * tq)
        q = jnp.concatenate([qaug_ref[0, 0, r, rows, :] for r in range(NSA_GROUP)], axis=0)
        sel_tiles = [(j, None) for j in range(i)] + [(i, 0)]
        osel_ref[rows, :] = attend(q, ks_ref, vs_ref, sel_tiles).astype(osel_ref.dtype)
        win_tiles = [(i - d, 1 if d == n_back else None) for d in range(n_back, 0, -1) if i >= d]
        win_tiles.append((i, 0))
        owin_ref[rows, :] = attend(q, kw_ref, vw_ref, win_tiles).astype(owin_ref.dtype)


def _sel_win_attn(qaug, kv, place, B, tq=256):
    kern = functools.partial(_sel_win_kernel, tq=tq)
    rows = NSA_GROUP * tq

    def kv_spec(col):
        return pl.BlockSpec((SEQ, LANES), lambda b, g: (b, col + g))

    out_spec = pl.BlockSpec((SEQ, 256), lambda b, g: (b, g))
    return pl.pallas_call(
        kern,
        grid=(B, NSA_KV_HEADS),
        in_specs=[pl.BlockSpec((1, 1, NSA_GROUP, SEQ, LANES), lambda b, g: (b, g, 0, 0, 0)),
                  kv_spec(0), kv_spec(2), kv_spec(4), kv_spec(6),
                  pl.BlockSpec((NSA_GROUP, LANES, 256), lambda b, g: (0, 0, 0))],
        out_specs=[out_spec, out_spec],
        out_shape=[jax.ShapeDtypeStruct((B * SEQ, 512), BF16),
                   jax.ShapeDtypeStruct((B * SEQ, 512), BF16)],
        scratch_shapes=[pltpu.VMEM((rows, SEQ), F32), pltpu.VMEM((2, tq, tq), F32)],
        compiler_params=_cparams(("parallel", "parallel")),
        name="nsa_sel_win_attn",
    )(qaug, kv, kv, kv, kv, place)


def _gla_kernel(qk_ref, v_ref, ga_ref, gz_ref, wg_ref, bg_ref, ng_ref, o_ref, st_ref):
    C = GLA_CHUNK
    n_chunks = SEQ // C
    wg = wg_ref[...]
    bg = bg_ref[...]
    norm_g = ng_ref[...]
    r_idx = lax.broadcasted_iota(jnp.int32, (C, C), 0)
    c_idx = lax.broadcasted_iota(jnp.int32, (C, C), 1)
    causal = r_idx >= c_idx
    tri = jnp.where(causal, 1.0, 0.0).astype(F32)
    lane = lax.broadcasted_iota(jnp.int32, (C, GLA_HEADS * GLA_DK), 1)
    st_ref[...] = jnp.zeros(st_ref.shape, F32)

    def chunk(n, carry):
        off = pl.multiple_of(n * C, C)
        q = qk_ref[pl.ds(off, C), 0:256] * (GLA_DK ** -0.5)
        k = qk_ref[pl.ds(off, C), 256:512]
        ga = ga_ref[pl.ds(off, C), :].astype(BF16)
        x = _dot(ga, wg) + bg
        log_a = -(jnp.maximum(-x, 0.0) + jnp.log1p(jnp.exp(-jnp.abs(x)))) / GLA_TAU
        b = jnp.dot(tri, log_a, preferred_element_type=F32, precision=lax.Precision.HIGHEST)
        b_last = b[C - 1:C, :]
        q_d = q * jnp.exp(b)
        k_d = (k * jnp.exp(-b)).astype(BF16)
        k_u = (k * jnp.exp(b_last - b)).astype(BF16)
        dec = jnp.exp(b_last)
        for h in range(GLA_HEADS):
            in_head = (lane >= h * GLA_DK) & (lane < (h + 1) * GLA_DK)
            q_h = jnp.where(in_head, q_d, 0.0).astype(BF16)
            v_h = v_ref[pl.ds(off, C), h * GLA_DV:(h + 1) * GLA_DV].astype(BF16)
            a = jnp.where(causal, _dot_nt(q_h, k_d), 0.0)
            o = _dot(a.astype(BF16), v_h)
            st = st_ref[h]
            o = o + _dot_nt(q_h, st.astype(BF16))
            upd = lax.dot_general(v_h, k_u, (((0,), (0,)), ((), ())), preferred_element_type=F32)
            st_ref[h] = st * dec + upd
            y = o * lax.rsqrt(jnp.mean(o * o, axis=-1, keepdims=True) + NORM_EPS) * norm_g
            y = y * _silu(gz_ref[pl.ds(off, C), h * GLA_DV:(h + 1) * GLA_DV])
            o_ref[pl.ds(off, C), h * GLA_DV:(h + 1) * GLA_DV] = y.astype(o_ref.dtype)
        return carry

    lax.fori_loop(0, n_chunks, chunk, 0)


def _gla(gqk, gv, ga, gz, wg, bg, norm_g, B):
    row = lambda b: (b, 0)
    const = lambda b: (0, 0)
    return pl.pallas_call(
        _gla_kernel,
        grid=(B,),
        in_specs=[pl.BlockSpec((SEQ, 512), row), pl.BlockSpec((SEQ, 512), row),
                  pl.BlockSpec((SEQ, LANES), row), pl.BlockSpec((SEQ, 512), row),
                  pl.BlockSpec((LANES, 256), const), pl.BlockSpec((1, 256), const),
                  pl.BlockSpec((1, GLA_DV), const)],
        out_specs=pl.BlockSpec((SEQ, 512), row),
        out_shape=jax.ShapeDtypeStruct((B * SEQ, 512), BF16),
        scratch_shapes=[pltpu.VMEM((GLA_HEADS, GLA_DV, GLA_HEADS * GLA_DK), F32)],
        compiler_params=_cparams(("parallel",)),
        name="gla_chunked",
    )(gqk, gv, ga, gz, wg, bg, norm_g)


def _deepnorm_ln(x, y, g, b):
    r = DEEPNORM_ALPHA * x + y
    mu = jnp.mean(r, axis=-1, keepdims=True)
    d = r - mu
    var = jnp.mean(d * d, axis=-1, keepdims=True)
    return d * lax.rsqrt(var + NORM_EPS) * g + b


def _gate_expand_mat():
    e = np.zeros((LANES, 3 * 512), np.float32)
    for br in range(3):
        for h in range(NSA_HEADS):
            e[br * 8 + h, br * 512 + h * 64: br * 512 + (h + 1) * 64] = 1.0
    return jnp.asarray(e)


def _even_out_kernel(ocmp_ref, osel_ref, owin_ref, ng_ref, nz_ref, ogla_ref, x_ref,
                     e_ref, w_ref, g_ref, b_ref, o_ref):
    sig = jax.nn.sigmoid(ng_ref[...])
    gates = jnp.dot(sig, e_ref[...], preferred_element_type=F32, precision=lax.Precision.HIGHEST)
    o_nsa = (gates[:, 0:512] * ocmp_ref[...].astype(F32)
             + gates[:, 512:1024] * osel_ref[...].astype(F32)
             + gates[:, 1024:1536] * owin_ref[...].astype(F32))
    o_nsa = (o_nsa * _silu(nz_ref[...])).astype(BF16)
    y = _dot(o_nsa, w_ref[0:512, :]) + _dot(ogla_ref[...], w_ref[512:1024, :])
    o_ref[...] = _deepnorm_ln(x_ref[...], y, g_ref[...], b_ref[...])


def _even_out(ocmp, osel, owin, ng, nz, ogla, x2, e_mat, w_bf, ln_g, ln_b, tm=256):
    T = x2.shape[0]
    row = lambda i: (i, 0)
    const = lambda i: (0, 0)
    return pl.pallas_call(
        _even_out_kernel,
        grid=(T // tm,),
        in_specs=[pl.BlockSpec((tm, 512), row), pl.BlockSpec((tm, 512), row),
                  pl.BlockSpec((tm, 512), row), pl.BlockSpec((tm, LANES), row),
                  pl.BlockSpec((tm, 512), row), pl.BlockSpec((tm, 512), row),
                  pl.BlockSpec((tm, D_MODEL), row),
                  pl.BlockSpec((LANES, 1536), const), pl.BlockSpec((1024, D_MODEL), const),
                  pl.BlockSpec((1, D_MODEL), const), pl.BlockSpec((1, D_MODEL), const)],
        out_specs=pl.BlockSpec((tm, D_MODEL), row),
        out_shape=jax.ShapeDtypeStruct((T, D_MODEL), F32),
        compiler_params=_cparams(("parallel",)),
        name="even_out_proj_ln",
    )(ocmp, osel, owin, ng, nz, ogla, x2, e_mat, w_bf, ln_g, ln_b)


def _odd_out_kernel(o_ref_in, z_ref, x_ref, w_ref, g_ref, b_ref, o_ref):
    o = (o_ref_in[...].astype(F32) * _silu(z_ref[...])).astype(BF16)
    y = _dot(o, w_ref[...])
    o_ref[...] = _deepnorm_ln(x_ref[...], y, g_ref[...], b_ref[...])


def _odd_out(o, z, x2, w_bf, ln_g, ln_b, tm=256):
    T = x2.shape[0]
    row = lambda i: (i, 0)
    const = lambda i: (0, 0)
    return pl.pallas_call(
        _odd_out_kernel,
        grid=(T // tm,),
        in_specs=[pl.BlockSpec((tm, D_MODEL), row), pl.BlockSpec((tm, D_MODEL), row),
                  pl.BlockSpec((tm, D_MODEL), row), pl.BlockSpec((D_MODEL, D_MODEL), const),
                  pl.BlockSpec((1, D_MODEL), const), pl.BlockSpec((1, D_MODEL), const)],
        out_specs=pl.BlockSpec((tm, D_MODEL), row),
        out_shape=jax.ShapeDtypeStruct((T, D_MODEL), F32),
        compiler_params=_cparams(("parallel",)),
        name="odd_out_proj_ln",
    )(o, z, x2, w_bf, ln_g, ln_b)


_HEAD_W = LANES
_QK_W = MLA_HEADS * _HEAD_W


def _rot_cols(w):
    half = MLA_ROPE // 2
    return jnp.concatenate([-w[..., half:], w[..., :half]], axis=-1)


def _odd_weights(w_in, w_uq, w_ukv):
    o = _O_OFF
    kr = w_in[:, o[2]:o[3]]
    w1 = jnp.concatenate([w_in[:, o[0]:o[2]], kr, _rot_cols(kr),
                          jnp.zeros((D_MODEL, LANES - 2 * MLA_ROPE), F32),
                          w_in[:, o[3]:o[4]]], axis=1).astype(BF16)
    uq = w_uq.reshape(MLA_Q_LORA, MLA_HEADS, MLA_NOPE + MLA_ROPE)
    zq = jnp.zeros((MLA_Q_LORA, MLA_HEADS, _HEAD_W - MLA_NOPE - MLA_ROPE), F32)
    wq_a = jnp.concatenate([uq, zq], axis=-1).reshape(MLA_Q_LORA, _QK_W).astype(BF16)
    wq_b = jnp.concatenate([jnp.zeros((MLA_Q_LORA, MLA_HEADS, MLA_NOPE), F32),
                            _rot_cols(uq[..., MLA_NOPE:]), zq], axis=-1)
    wq_b = wq_b.reshape(MLA_Q_LORA, _QK_W).astype(BF16)
    ukv = w_ukv.reshape(MLA_KV_LORA, MLA_HEADS, MLA_NOPE + MLA_V)
    wk = jnp.concatenate([ukv[..., :MLA_NOPE],
                          jnp.zeros((MLA_KV_LORA, MLA_HEADS, _HEAD_W - MLA_NOPE), F32)], axis=-1)
    wk = wk.reshape(MLA_KV_LORA, _QK_W).astype(BF16)
    wv = jnp.concatenate([ukv[..., MLA_NOPE:],
                          jnp.zeros((MLA_KV_LORA, MLA_HEADS, _HEAD_W - MLA_V), F32)], axis=-1)
    wv = wv.reshape(MLA_KV_LORA, _QK_W).astype(BF16)
    return w1, wq_a, wq_b, wk, wv


def _rope_tables():
    half = MLA_ROPE // 2
    freqs = jnp.exp(-math.log(ROPE_THETA) * jnp.arange(half, dtype=F32) * 2.0 / MLA_ROPE)
    ang = jnp.arange(SEQ, dtype=F32)[:, None] * freqs[None, :]
    cos = jnp.concatenate([jnp.cos(ang), jnp.cos(ang)], axis=1)
    sin = jnp.concatenate([jnp.sin(ang), jnp.sin(ang)], axis=1)
    scale = (MLA_NOPE + MLA_ROPE) ** -0.5 * math.log2(math.e)
    pad = jnp.zeros((SEQ, LANES - MLA_NOPE - MLA_ROPE), F32)
    qc = jnp.concatenate([jnp.full((SEQ, MLA_NOPE), scale, F32), cos * scale, pad], axis=1)
    qs = jnp.concatenate([jnp.zeros((SEQ, MLA_NOPE), F32), sin * scale, pad], axis=1)
    kcs = jnp.concatenate([cos, sin, jnp.zeros((SEQ, LANES - 2 * MLA_ROPE), F32)], axis=1)
    place = np.zeros((LANES, _QK_W), np.float32)
    for h in range(MLA_HEADS):
        for c in range(MLA_ROPE):
            place[c, h * _HEAD_W + MLA_NOPE + c] = 1.0
            place[MLA_ROPE + c, h * _HEAD_W + MLA_NOPE + c] = 1.0
    v_one = np.zeros((1, _QK_W), np.float32)
    v_one[0, np.arange(MLA_HEADS) * _HEAD_W + MLA_V] = 1.0
    return qc, qs, kcs, jnp.asarray(place, dtype=BF16), jnp.asarray(v_one)


def _rms(x, g):
    return x * lax.rsqrt(jnp.mean(x * x, axis=-1, keepdims=True) + NORM_EPS) * g


def _odd_in_kernel(x_ref, w1_ref, wqa_ref, wqb_ref, wk_ref, wv_ref, place_ref,
                   qn_ref, kn_ref, vone_ref, qc_ref, qs_ref, kcs_ref,
                   q_ref, k_ref, v_ref, z_ref):
    xb = x_ref[...].astype(BF16)
    c_q = _dot(xb, w1_ref[:, 0:MLA_Q_LORA])
    c_kv = _dot(xb, w1_ref[:, MLA_Q_LORA:640])
    kr = _dot(xb, w1_ref[:, 640:768])
    z_ref[...] = _dot(xb, w1_ref[:, 768:1792])
    cqn = _rms(c_q, qn_ref[...]).astype(BF16)
    ckvn = _rms(c_kv, kn_ref[...]).astype(BF16)
    qa = _dot(cqn, wqa_ref[...])
    qb = _dot(cqn, wqb_ref[...])
    qc = qc_ref[...]
    qs = qs_ref[...]
    for h in range(MLA_HEADS):
        sl = slice(h * _HEAD_W, (h + 1) * _HEAD_W)
        q_ref[:, sl] = (qa[:, sl] * qc + qb[:, sl] * qs).astype(BF16)
    kr_r = (kr * kcs_ref[...]).astype(BF16)
    k_ref[...] = (_dot(ckvn, wk_ref[...]) + _dot(kr_r, place_ref[...])).astype(BF16)
    v_ref[...] = (_dot(ckvn, wv_ref[...]) + vone_ref[...]).astype(BF16)


def _odd_in_proj(x2, w1, wqa, wqb, wk, wv, place, qn, kn, vone, qc, qs, kcs, tm=256):
    T = x2.shape[0]
    s_tiles = SEQ // tm
    row = lambda i: (i, 0)
    const = lambda i: (0, 0)
    pos = lambda i: (i % s_tiles, 0)

    def full(a):
        return pl.BlockSpec(a.shape, const)

    return pl.pallas_call(
        _odd_in_kernel,
        grid=(T // tm,),
        in_specs=[pl.BlockSpec((tm, D_MODEL), row), full(w1), full(wqa), full(wqb), full(wk),
                  full(wv), full(place), full(qn), full(kn), full(vone),
                  pl.BlockSpec((tm, LANES), pos), pl.BlockSpec((tm, LANES), pos),
                  pl.BlockSpec((tm, LANES), pos)],
        out_specs=[pl.BlockSpec((tm, _QK_W), row), pl.BlockSpec((tm, _QK_W), row),
                   pl.BlockSpec((tm, _QK_W), row), pl.BlockSpec((tm, 1024), row)],
        out_shape=[jax.ShapeDtypeStruct((T, _QK_W), BF16), jax.ShapeDtypeStruct((T, _QK_W), BF16),
                   jax.ShapeDtypeStruct((T, _QK_W), BF16), jax.ShapeDtypeStruct((T, 1024), F32)],
        compiler_params=_cparams(("parallel",)),
        name="odd_in_proj",
    )(x2, w1, wqa, wqb, wk, wv, place, qn, kn, vone, qc, qs, kcs)


def _mla_kernel(q_ref, k_ref, v_ref, o_ref, s_ref, bias_ref, *, tq):
    nq = SEQ // tq
    a_idx = lax.broadcasted_iota(jnp.int32, (tq, tq), 0)
    b_idx = lax.broadcasted_iota(jnp.int32, (tq, tq), 1)
    bias_ref[...] = jnp.where(a_idx >= b_idx, 0.0, NEG_INF)
    lane = lax.broadcasted_iota(jnp.int32, (tq, LANES), 1)
    for i in range(nq):
        rows = slice(i * tq, (i + 1) * tq)
        outs = []
        for h in range(2):
            hs = slice(h * _HEAD_W, (h + 1) * _HEAD_W)
            q = q_ref[rows, hs]
            mx = None
            for j in range(i + 1):
                cols = slice(j * tq, (j + 1) * tq)
                s = _dot_nt(q, k_ref[cols, hs])
                if j == i:
                    s = s + bias_ref[...]
                s_ref[h, :, cols] = s
                t = s[:, 0:LANES]
                for c in range(1, tq // LANES):
                    t = jnp.maximum(t, s[:, c * LANES:(c + 1) * LANES])
                mx = t if mx is None else jnp.maximum(mx, t)
            m = jnp.max(mx, axis=1, keepdims=True)
            acc = None
            for j in range(i + 1):
                cols = slice(j * tq, (j + 1) * tq)
                p = jnp.exp2(s_ref[h, :, cols] - m).astype(BF16)
                pv = _dot(p, v_ref[cols, hs])
                acc = pv if acc is None else acc + pv
            outs.append(acc / acc[:, MLA_V:MLA_V + 1])
        o = jnp.where(lane < MLA_V, outs[0], pltpu.roll(outs[1], MLA_V, 1))
        o_ref[rows, :] = o.astype(o_ref.dtype)


def _mla_attn(q, k, v, B, tq=256):
    kern = functools.partial(_mla_kernel, tq=tq)
    pair = pl.BlockSpec((SEQ, 2 * _HEAD_W), lambda b, h: (b, h))
    return pl.pallas_call(
        kern,
        grid=(B, MLA_HEADS // 2),
        in_specs=[pair, pair, pair],
        out_specs=pl.BlockSpec((SEQ, LANES), lambda b, h: (b, h)),
        out_shape=jax.ShapeDtypeStruct((B * SEQ, MLA_HEADS * MLA_V), BF16),
        scratch_shapes=[pltpu.VMEM((2, tq, SEQ), F32), pltpu.VMEM((tq, tq), F32)],
        compiler_params=_cparams(("parallel", "parallel")),
        name="mla_flash_attn",
    )(q, k, v)


def _even_layer(x2, B, w_in, cmp_pe, cmp_w1, cmp_w2, gla_w_gate, gla_b_gate, gla_norm,
                w_out, ln_g, ln_b):
    T = x2.shape[0]
    q, kvc, kv, ng, nz, gqk, gv, ga, gz = _even_in_proj(x2, _even_in_weight(w_in), _nsa_kv_table())

    xc = kvc.reshape(B, N_CMP_PAD, CMP_STRIDE, 2, NSA_KV_HEADS, NSA_HEAD_DIM)
    xc = xc.transpose(0, 3, 4, 1, 2, 5).reshape(B, 2, NSA_KV_HEADS, N_CMP_PAD, CMP_STRIDE * NSA_HEAD_DIM)
    pe2 = cmp_pe.reshape(2, 2, CMP_STRIDE * NSA_HEAD_DIM)
    kvcmp = _compress(xc, pe2, cmp_w1.astype(BF16), cmp_w2.astype(BF16))
    kc = kvcmp[:, 0]
    vct = jnp.swapaxes(kvcmp[:, 1], -1, -2)

    slopes = jnp.asarray(_alibi_slopes_np())
    ocmp, qaug = _cmp_attn(q, kc, vct, _overlap_t(), _q_alibi_table(), _head_select_mats(),
                           slopes, B)
    kv2 = kv.reshape(B * SEQ, 1024)
    osel, owin = _sel_win_attn(qaug, kv2, _place_mats(), B)

    wg = jnp.pad(gla_w_gate, ((0, LANES - GLA_GATE_RANK), (0, 0))).astype(BF16)
    ogla = _gla(gqk, gv, ga, gz, wg, gla_b_gate.reshape(1, -1), gla_norm.reshape(1, -1), B)

    return _even_out(ocmp, osel, owin, ng, nz, ogla, x2, _gate_expand_mat(), w_out.astype(BF16),
                     ln_g.reshape(1, -1), ln_b.reshape(1, -1))


def _odd_layer(x2, B, w_in, q_norm, w_uq, kv_norm, w_ukv, w_out, ln_g, ln_b):
    w1, wqa, wqb, wk, wv = _odd_weights(w_in, w_uq, w_ukv)
    qc, qs, kcs, place, vone = _rope_tables()
    q, k, v, z = _odd_in_proj(x2, w1, wqa, wqb, wk, wv, place, q_norm.reshape(1, -1),
                              kv_norm.reshape(1, -1), vone, qc, qs, kcs)
    o = _mla_attn(q, k, v, B)
    return _odd_out(o, z, x2, w_out.astype(BF16), ln_g.reshape(1, -1), ln_b.reshape(1, -1))


def kernel(x, e_w_in, e_cmp_pe, e_cmp_w1, e_cmp_w2, e_gla_w_gate, e_gla_b_gate, e_gla_norm,
           e_w_out, e_ln_g, e_ln_b, o_w_in, o_q_norm, o_w_uq, o_kv_norm, o_w_ukv, o_w_out,
           o_ln_g, o_ln_b):
    B, S, D = x.shape
    x2 = x.reshape(B * S, D)
    for layer in range(DEPTH):
        i = layer // 2
        if layer % 2 == 0:
            x2 = _even_layer(x2, B, e_w_in[i], e_cmp_pe[i], e_cmp_w1[i], e_cmp_w2[i],
                             e_gla_w_gate[i], e_gla_b_gate[i], e_gla_norm[i], e_w_out[i],
                             e_ln_g[i], e_ln_b[i])
        else:
            x2 = _odd_layer(x2, B, o_w_in[i], o_q_norm[i], o_w_uq[i], o_kv_norm[i], o_w_ukv[i],
                            o_w_out[i], o_ln_g[i], o_ln_b[i])
    return x2.reshape(B, S, D)
```

```python
import functools
import math

import jax
import jax.numpy as jnp
import numpy as np
from jax import lax
from jax.experimental import pallas as pl
from jax.experimental.pallas import tpu as pltpu

F32 = jnp.float32
BF16 = jnp.bfloat16

D_MODEL = 1024
SEQ = 2048
DEPTH = 2
NSA_HEADS = 8
NSA_KV_HEADS = 2
NSA_GROUP = 4
NSA_HEAD_DIM = 64
CMP_BLOCK = 32
CMP_STRIDE = 16
CMP_HIDDEN = 128
N_CMP = (SEQ - CMP_BLOCK) // CMP_STRIDE + 1
N_CMP_PAD = 128
SEL_BLOCK = 64
N_SEL = SEQ // SEL_BLOCK
SEL_TOPN = 8
WINDOW = 512
FORCE_BONUS = 1000.0
GLA_HEADS = 4
GLA_DK = 64
GLA_DV = 128
GLA_GATE_RANK = 16
GLA_TAU = 16.0
GLA_CHUNK = 64
MLA_HEADS = 16
MLA_NOPE = 64
MLA_ROPE = 32
MLA_V = 64
MLA_Q_LORA = 384
MLA_KV_LORA = 256
ROPE_THETA = 10000.0
NORM_EPS = 1e-5
NEG_INF = -1e30
DEEPNORM_ALPHA = (2 * DEPTH) ** 0.25

LANES = 128
MASK_BIG = 2.0 ** 100
VMEM_LIMIT = 56 * 1024 * 1024

_E_OFF = np.cumsum([0, 512, 768, 24, 512, 256, 256, 512, 16, 512])
_O_OFF = np.cumsum([0, MLA_Q_LORA, MLA_KV_LORA, MLA_ROPE, MLA_HEADS * MLA_V])


def _cparams(sem):
    return pltpu.CompilerParams(dimension_semantics=sem, vmem_limit_bytes=VMEM_LIMIT)


def _silu(x):
    return x * jax.nn.sigmoid(x)


def _dot(a, b):
    return jnp.dot(a, b, preferred_element_type=F32)


def _dot_nt(a, b):
    return lax.dot_general(a, b, (((1,), (1,)), ((), ())), preferred_element_type=F32)


_A_SEGS = (("q", 512, BF16), ("kvc", 256, F32), ("kv", 1024, BF16), ("ng", 128, F32),
           ("nz", 512, F32), ("gqk", 512, F32), ("gv", 512, F32), ("ga", 128, F32),
           ("gz", 512, F32))
_A_OFF = np.cumsum([0] + [s[1] for s in _A_SEGS])


def _even_in_weight(w):
    o = _E_OFF
    z64 = jnp.zeros((D_MODEL, 64), F32)
    nkv = w[:, o[1]:o[2]]

    def kvcol(br, j, g):
        c = ((br * 2 + j) * NSA_KV_HEADS + g) * 64
        return nkv[:, c:c + 64]

    kv_blocks = []
    for br in (1, 2):
        for j in (0, 1):
            for g in (0, 1):
                kv_blocks += [kvcol(br, j, g), z64]
    ng = jnp.pad(w[:, o[2]:o[3]], ((0, 0), (0, LANES - 24)))
    ga = jnp.pad(w[:, o[7]:o[8]], ((0, 0), (0, LANES - GLA_GATE_RANK)))
    cols = [w[:, o[0]:o[1]] * (NSA_HEAD_DIM ** -0.5), nkv[:, :256]] + kv_blocks + [
        ng, w[:, o[3]:o[4]], w[:, o[4]:o[6]], w[:, o[6]:o[7]], ga, w[:, o[8]:o[9]]]
    return jnp.concatenate(cols, axis=1).astype(BF16)


def _nsa_kv_table():
    pos = np.arange(SEQ)
    k_al = np.zeros((SEQ, LANES), np.float32)
    k_al[:, 96] = 1.0
    k_al[:, 97] = 1.0
    k_al[:, 98] = pos // SEL_BLOCK
    k_al[:, 99] = pos % SEL_BLOCK
    k_sel = k_al.copy()
    k_sel[pos, 64 + pos // SEL_BLOCK] = 1.0
    v_one = np.zeros((SEQ, LANES), np.float32)
    v_one[:, 64] = 1.0
    tab = np.concatenate([k_sel, k_sel, v_one, v_one, k_al, k_al, v_one, v_one], axis=1)
    return jnp.asarray(tab)


def _even_in_kernel(x_ref, w_ref, tab_ref, *out_refs):
    xb = x_ref[...].astype(BF16)
    for (name, width, dt), off, o_ref in zip(_A_SEGS, _A_OFF[:-1], out_refs):
        r = _dot(xb, w_ref[:, off:off + width])
        if name == "kv":
            r = r + tab_ref[...]
        o_ref[...] = r.astype(dt)


def _even_in_proj(x2, w_bf, tab, tm=256):
    T = x2.shape[0]
    n_w = w_bf.shape[1]
    s_tiles = SEQ // tm
    out_shape = [jax.ShapeDtypeStruct((T, wd), dt) for _, wd, dt in _A_SEGS]
    out_specs = [pl.BlockSpec((tm, wd), lambda i: (i, 0)) for _, wd, _ in _A_SEGS]
    return pl.pallas_call(
        _even_in_kernel,
        grid=(T // tm,),
        in_specs=[pl.BlockSpec((tm, D_MODEL), lambda i: (i, 0)),
                  pl.BlockSpec((D_MODEL, n_w), lambda i: (0, 0)),
                  pl.BlockSpec((tm, 1024), lambda i: (i % s_tiles, 0))],
        out_specs=out_specs,
        out_shape=out_shape,
        compiler_params=_cparams(("parallel",)),
        name="even_in_proj",
    )(x2, w_bf, tab)


def _compress_weights(cmp_pe, cmp_w1):
    pe_tab = jnp.concatenate([cmp_pe[0], cmp_pe[0], cmp_pe[1], cmp_pe[1]], axis=1)
    w = cmp_w1.reshape(2, CMP_BLOCK, NSA_HEAD_DIM, CMP_HIDDEN)
    z = jnp.zeros_like(w)
    w_bd = jnp.concatenate([jnp.concatenate([w, z], axis=3), jnp.concatenate([z, w], axis=3)], axis=2)
    return pe_tab, w_bd.astype(BF16)


def _compress_kernel(xk_ref, xv_ref, pe_ref, w1_ref, w2_ref, w2t_ref, kc_ref, vct_ref):
    half = CMP_BLOCK // 2
    acc = [[None, None], [None, None]]
    for l in range(half):
        for j, x_ref in enumerate((xk_ref, xv_ref)):
            xs = x_ref[pl.ds(l, N_CMP_PAD, stride=CMP_STRIDE), :]
            for part in range(2):
                row = part * half + l
                xb = (xs + pe_ref[row:row + 1, j * LANES:(j + 1) * LANES]).astype(BF16)
                d = _dot(xb, w1_ref[j, row])
                acc[j][part] = d if acc[j][part] is None else acc[j][part] + d
    for j in range(2):
        h = acc[j][0] + pltpu.roll(acc[j][1], N_CMP_PAD - 1, 0)
        hs = _silu(h).astype(BF16)
        for g in range(NSA_KV_HEADS):
            hg = hs[:, g * CMP_HIDDEN:(g + 1) * CMP_HIDDEN]
            if j == 0:
                kc_ref[0, g] = _dot(hg, w2_ref[...])
            else:
                vct_ref[0, g] = _dot_nt(w2t_ref[...], hg)


def _compress(kvc, pe_tab, w1_bd, w2, B):
    w2k = w2[0].astype(BF16)
    w2vt = w2[1].T.astype(BF16)
    return pl.pallas_call(
        _compress_kernel,
        grid=(B,),
        in_specs=[pl.BlockSpec((SEQ, LANES), lambda b: (b, 0)),
                  pl.BlockSpec((SEQ, LANES), lambda b: (b, 1)),
                  pl.BlockSpec((CMP_BLOCK, 256), lambda b: (0, 0)),
                  pl.BlockSpec((2, CMP_BLOCK, LANES, 256), lambda b: (0, 0, 0, 0)),
                  pl.BlockSpec((CMP_HIDDEN, NSA_HEAD_DIM), lambda b: (0, 0)),
                  pl.BlockSpec((NSA_HEAD_DIM, CMP_HIDDEN), lambda b: (0, 0))],
        out_specs=[pl.BlockSpec((1, NSA_KV_HEADS, N_CMP_PAD, NSA_HEAD_DIM), lambda b: (b, 0, 0, 0)),
                   pl.BlockSpec((1, NSA_KV_HEADS, NSA_HEAD_DIM, N_CMP_PAD), lambda b: (b, 0, 0, 0))],
        out_shape=[jax.ShapeDtypeStruct((B, NSA_KV_HEADS, N_CMP_PAD, NSA_HEAD_DIM), F32),
                   jax.ShapeDtypeStruct((B, NSA_KV_HEADS, NSA_HEAD_DIM, N_CMP_PAD), F32)],
        compiler_params=_cparams(("parallel",)),
        name="nsa_compress",
    )(kvc, kvc, pe_tab, w1_bd, w2k, w2vt)


def _alibi_slopes_np():
    return np.exp2(-(8.0 / NSA_HEADS) * np.arange(1, NSA_HEADS + 1)).astype(np.float32)


def _overlap_t():
    cs = np.arange(N_CMP) * CMP_STRIDE
    ss = np.arange(N_SEL) * SEL_BLOCK
    ov = np.clip(np.minimum(cs[:, None] + CMP_BLOCK, ss[None, :] + SEL_BLOCK)
                 - np.maximum(cs[:, None], ss[None, :]), 0, None).astype(np.float32) / CMP_BLOCK
    ovt = np.zeros((N_SEL, N_CMP_PAD), np.float32)
    ovt[:, :N_CMP] = ov.T
    return jnp.asarray(ovt)


def _q_alibi_table():
    slopes = _alibi_slopes_np().reshape(NSA_KV_HEADS, NSA_GROUP)
    pos = np.arange(SEQ)
    tab = np.zeros((NSA_KV_HEADS, SEQ, LANES), np.float32)
    for g in range(NSA_KV_HEADS):
        for r in range(NSA_GROUP):
            m = slopes[g, r]
            tab[g, :, r * 32 + 0] = -m * SEL_BLOCK * (pos // SEL_BLOCK)
            tab[g, :, r * 32 + 1] = -m * (pos % SEL_BLOCK)
            tab[g, :, r * 32 + 2] = m * SEL_BLOCK
            tab[g, :, r * 32 + 3] = m
    return jnp.asarray(tab)


def _head_select_mats():
    m = np.zeros((NSA_GROUP, NSA_HEAD_DIM, 256), np.float32)
    for r in range(NSA_GROUP):
        m[r, np.arange(64), r * 64 + np.arange(64)] = 1.0
    return jnp.asarray(m, dtype=BF16)


def _cmp_bias_table():
    slopes = jnp.asarray(_alibi_slopes_np()).reshape(NSA_KV_HEADS, NSA_GROUP, 1, 1)
    n = jnp.arange(N_CMP_PAD)[:, None]
    t = jnp.arange(SEQ)[None, :]
    dist = t - (n * CMP_STRIDE + (CMP_BLOCK - 1))
    visible = (dist >= 0) & (n < N_CMP)
    return jnp.where(visible, -slopes * dist.astype(F32), NEG_INF)


def _cmp_attn_kernel(q_ref, kc_ref, vct_ref, ovt_ref, qal_ref, hs_ref, bias_ref,
                     ocmp_ref, qaug_ref, *, tq):
    kc = kc_ref[0, 0].astype(BF16)
    vct = vct_ref[0, 0].astype(BF16)
    k_ext = [_dot(kc, hs_ref[r]).astype(BF16) for r in range(NSA_GROUP)]
    ri = lax.broadcasted_iota(jnp.int32, (256, 256), 0)
    ci = lax.broadcasted_iota(jnp.int32, (256, 256), 1)
    eye = jnp.where(ri == ci, 1.0, 0.0).astype(BF16)
    sub8 = lax.broadcasted_iota(jnp.int32, (8, tq), 0)
    j_idx = lax.broadcasted_iota(jnp.int32, (N_SEL, tq), 0)
    t_lane = lax.broadcasted_iota(jnp.int32, (N_SEL, tq), 1)
    n_grp = N_SEL // 8
    for i in range(SEQ // tq):
        cols = slice(i * tq, (i + 1) * tq)
        nv = min(N_CMP_PAD, (i + 1) * tq // CMP_STRIDE)
        q = q_ref[cols, :]
        lhs = jnp.concatenate([k[:nv] for k in k_ext] + [eye], axis=0)
        res = _dot_nt(lhs, q)
        q_t = res[NSA_GROUP * nv:]
        p_sum = None
        o_parts = []
        for r in range(NSA_GROUP):
            s = res[r * nv:(r + 1) * nv] + bias_ref[0, r, 0:nv, cols]
            m = jnp.max(s, axis=0, keepdims=True)
            e = jnp.exp(s - m)
            if i == 0:
                e = jnp.where(bias_ref[0, r, 0:nv, cols] > 0.5 * NEG_INF, e, 0.0)
            l = jnp.sum(e, axis=0, keepdims=True)
            p = e / jnp.where(l > 0.0, l, 1.0)
            p_sum = p if p_sum is None else p_sum + p
            o_parts.append(_dot(vct[:, :nv], p.astype(BF16)))
        ocmp_ref[cols, :] = jnp.concatenate(o_parts, axis=0).T.astype(ocmp_ref.dtype)

        imp = jnp.dot(ovt_ref[:, 0:nv], p_sum, preferred_element_type=F32,
                      precision=lax.Precision.HIGHEST)
        cur = (i * tq + t_lane) // SEL_BLOCK
        forced = (j_idx == 0) | (j_idx == cur) | (j_idx == cur - 1)
        imp = jnp.where(j_idx > cur, -1.0, imp + jnp.where(forced, FORCE_BONUS, 0.0))
        grp = [imp[8 * a:8 * (a + 1)] for a in range(n_grp)]
        rank = [jnp.zeros((8, tq), F32) for _ in range(n_grp)]
        for k in range(N_SEL):
            row = jnp.broadcast_to(imp[k:k + 1, :], (8, tq))
            for a in range(n_grp):
                ge = jnp.where(row >= grp[a], 1.0, 0.0)
                gt = jnp.where(row > grp[a], 1.0, 0.0)
                if 8 * a > k:
                    cnt = ge
                elif 8 * a + 7 <= k:
                    cnt = gt
                else:
                    cnt = jnp.where(sub8 > k - 8 * a, ge, gt)
                rank[a] = rank[a] + cnt
        sel_bias = jnp.where(jnp.concatenate(rank, axis=0) < float(SEL_TOPN), 0.0, -MASK_BIG)

        qal_t = qal_ref[0, cols, :].T
        for r in range(NSA_GROUP):
            aug_t = jnp.concatenate([q_t[r * 64:(r + 1) * 64], sel_bias,
                                     qal_t[r * 32:(r + 1) * 32, :]], axis=0)
            qaug_ref[0, 0, r, cols, :] = aug_t.T.astype(BF16)


def _cmp_attn(q, kc, vct, ovt, qal, hs, bias, B, tq=256):
    kern = functools.partial(_cmp_attn_kernel, tq=tq)
    return pl.pallas_call(
        kern,
        grid=(NSA_KV_HEADS, B),
        in_specs=[pl.BlockSpec((SEQ, 256), lambda g, b: (b, g)),
                  pl.BlockSpec((1, 1, N_CMP_PAD, NSA_HEAD_DIM), lambda g, b: (b, g, 0, 0)),
                  pl.BlockSpec((1, 1, NSA_HEAD_DIM, N_CMP_PAD), lambda g, b: (b, g, 0, 0)),
                  pl.BlockSpec((N_SEL, N_CMP_PAD), lambda g, b: (0, 0)),
                  pl.BlockSpec((1, SEQ, LANES), lambda g, b: (g, 0, 0)),
                  pl.BlockSpec((NSA_GROUP, NSA_HEAD_DIM, 256), lambda g, b: (0, 0, 0)),
                  pl.BlockSpec((1, NSA_GROUP, N_CMP_PAD, SEQ), lambda g, b: (g, 0, 0, 0))],
        out_specs=[pl.BlockSpec((SEQ, 256), lambda g, b: (b, g)),
                   pl.BlockSpec((1, 1, NSA_GROUP, SEQ, LANES), lambda g, b: (b, g, 0, 0, 0))],
        out_shape=[jax.ShapeDtypeStruct((B * SEQ, 512), BF16),
                   jax.ShapeDtypeStruct((B, NSA_KV_HEADS, NSA_GROUP, SEQ, LANES), BF16)],
        compiler_params=_cparams(("parallel", "parallel")),
        name="nsa_cmp_attn_select",
    )(q, kc, vct, ovt, qal, hs, bias)


def _place_mats():
    m = np.zeros((NSA_GROUP, LANES, 256), np.float32)
    for r in range(NSA_GROUP):
        m[r, np.arange(64), r * 64 + np.arange(64)] = 1.0
    return jnp.asarray(m, dtype=BF16)


def _sel_win_kernel(qaug_ref, ks_ref, vs_ref, kw_ref, vw_ref, place_ref,
                    osel_ref, owin_ref, s_ref, bias_ref, *, tq):
    nq = SEQ // tq
    n_back = WINDOW // tq
    a_idx = lax.broadcasted_iota(jnp.int32, (tq, tq), 0)
    b_idx = lax.broadcasted_iota(jnp.int32, (tq, tq), 1)
    bias_ref[0] = jnp.where(a_idx >= b_idx, 0.0, -MASK_BIG)
    bias_ref[1] = jnp.where(a_idx < b_idx, 0.0, -MASK_BIG)

    def attend(q, k_ref, v_ref, tiles):
        mx = None
        for n, (j, bias) in enumerate(tiles):
            s = _dot_nt(q, k_ref[j * tq:(j + 1) * tq, :])
            if bias is not None:
                s = s + jnp.concatenate([bias_ref[bias]] * NSA_GROUP, axis=0)
            s_ref[:, n * tq:(n + 1) * tq] = s
            t = s[:, 0:LANES]
            for c in range(1, tq // LANES):
                t = jnp.maximum(t, s[:, c * LANES:(c + 1) * LANES])
            mx = t if mx is None else jnp.maximum(mx, t)
        m = jnp.max(mx, axis=1, keepdims=True)
        acc = None
        for n, (j, _) in enumerate(tiles):
            p = jnp.exp(s_ref[:, n * tq:(n + 1) * tq] - m).astype(BF16)
            pv = _dot(p, v_ref[j * tq:(j + 1) * tq, :])
            acc = pv if acc is None else acc + pv
        o = (acc / acc[:, NSA_HEAD_DIM:NSA_HEAD_DIM + 1]).astype(BF16)
        out = jnp.zeros((tq, 256), F32)
        for r in range(NSA_GROUP):
            out = out + _dot(o[r * tq:(r + 1) * tq, :], place_ref[r])
        return out

    for i in range(nq):
        rows = slice(i * tq, (i + 1) * tq)
        q = jnp.concatenate([qaug_ref[0, 0, r, rows, :] for r in range(NSA_GROUP)], axis=0)
        sel_tiles = [(j, None) for j in range(i)] + [(i, 0)]
        osel_ref[rows, :] = attend(q, ks_ref, vs_ref, sel_tiles).astype(osel_ref.dtype)
        win_tiles = [(i - d, 1 if d == n_back else None) for d in range(n_back, 0, -1) if i >= d]
        win_tiles.append((i, 0))
        owin_ref[rows, :] = attend(q, kw_ref, vw_ref, win_tiles).astype(owin_ref.dtype)


def _sel_win_attn(qaug, kv, place, B, tq=256):
    kern = functools.partial(_sel_win_kernel, tq=tq)
    rows = NSA_GROUP * tq

    def kv_spec(col):
        return pl.BlockSpec((SEQ, LANES), lambda b, g: (b, col + g))

    out_spec = pl.BlockSpec((SEQ, 256), lambda b, g: (b, g))
    return pl.pallas_call(
        kern,
        grid=(B, NSA_KV_HEADS),
        in_specs=[pl.BlockSpec((1, 1, NSA_GROUP, SEQ, LANES), lambda b, g: (b, g, 0, 0, 0)),
                  kv_spec(0), kv_spec(2), kv_spec(4), kv_spec(6),
                  pl.BlockSpec((NSA_GROUP, LANES, 256), lambda b, g: (0, 0, 0))],
        out_specs=[out_spec, out_spec],
        out_shape=[jax.ShapeDtypeStruct((B * SEQ, 512), BF16),
                   jax.ShapeDtypeStruct((B * SEQ, 512), BF16)],
        scratch_shapes=[pltpu.VMEM((rows, SEQ), F32), pltpu.VMEM((2, tq, tq), F32)],
        compiler_params=_cparams(("parallel", "parallel")),
        name="nsa_sel_win_attn",
    )(qaug, kv, kv, kv, kv, place)


_GLA_ROWS = 512


def _gla_kernel(qk_ref, v_ref, ga_ref, gz_ref, wg_ref, bg_ref, ng_ref, o_ref,
                b_ref, qd_ref, kd_ref, st_ref):
    C = GLA_CHUNK
    KW = GLA_HEADS * GLA_DK
    wg = wg_ref[...]
    bg = bg_ref[...]
    norm_g = ng_ref[...]

    rin = lax.broadcasted_iota(jnp.int32, (_GLA_ROWS, KW), 0) % C
    lane = lax.broadcasted_iota(jnp.int32, (_GLA_ROWS, KW), 1)
    for blk in range(SEQ // _GLA_ROWS):
        rows = slice(blk * _GLA_ROWS, (blk + 1) * _GLA_ROWS)
        x = _dot(ga_ref[rows, :].astype(BF16), wg) + bg
        b = -(jnp.maximum(-x, 0.0) + jnp.log1p(jnp.exp(-jnp.abs(x)))) / GLA_TAU
        sh = 1
        while sh < C:
            b = b + jnp.where(rin >= sh, pltpu.roll(b, sh, 0), 0.0)
            sh *= 2
        b_ref[rows, :] = b
        q_d = qk_ref[rows, 0:KW] * (GLA_DK ** -0.5) * jnp.exp(b)
        for h in range(GLA_HEADS):
            in_head = (lane >= h * GLA_DK) & (lane < (h + 1) * GLA_DK)
            qd_ref[h, rows, :] = jnp.where(in_head, q_d, 0.0).astype(BF16)
        kd_ref[rows, :] = (qk_ref[rows, KW:2 * KW] * jnp.exp(-b)).astype(BF16)

    r_idx = lax.broadcasted_iota(jnp.int32, (C, C), 0)
    c_idx = lax.broadcasted_iota(jnp.int32, (C, C), 1)
    causal = r_idx >= c_idx
    st_ref[...] = jnp.zeros(st_ref.shape, F32)
    for n in range(SEQ // C):
        rows = slice(n * C, (n + 1) * C)
        b = b_ref[rows, :]
        b_last = b_ref[n * C + C - 1:(n + 1) * C, :]
        k_u = (qk_ref[rows, KW:2 * KW] * jnp.exp(b_last - b)).astype(BF16)
        k_d = kd_ref[rows, :]
        v = v_ref[rows, :].astype(BF16)
        st = st_ref[...]
        st_b = st.astype(BF16)
        for h in range(GLA_HEADS):
            vs = slice(h * GLA_DV, (h + 1) * GLA_DV)
            q_h = qd_ref[h, rows, :]
            a = jnp.where(causal, _dot_nt(q_h, k_d), 0.0)
            o = _dot(a.astype(BF16), v[:, vs]) + _dot_nt(q_h, st_b[vs, :])
            y = o * lax.rsqrt(jnp.mean(o * o, axis=-1, keepdims=True) + NORM_EPS) * norm_g
            y = y * _silu(gz_ref[rows, vs])
            o_ref[rows, vs] = y.astype(o_ref.dtype)
        upd = lax.dot_general(v, k_u, (((0,), (0,)), ((), ())), preferred_element_type=F32)
        st_ref[...] = st * jnp.exp(b_last) + upd


def _gla(gqk, gv, ga, gz, wg, bg, norm_g, B):
    row = lambda b: (b, 0)
    const = lambda b: (0, 0)
    return pl.pallas_call(
        _gla_kernel,
        grid=(B,),
        in_specs=[pl.BlockSpec((SEQ, 512), row), pl.BlockSpec((SEQ, 512), row),
                  pl.BlockSpec((SEQ, LANES), row), pl.BlockSpec((SEQ, 512), row),
                  pl.BlockSpec((LANES, 256), const), pl.BlockSpec((1, 256), const),
                  pl.BlockSpec((1, GLA_DV), const)],
        out_specs=pl.BlockSpec((SEQ, 512), row),
        out_shape=jax.ShapeDtypeStruct((B * SEQ, 512), BF16),
        scratch_shapes=[pltpu.VMEM((SEQ, GLA_HEADS * GLA_DK), F32),
                        pltpu.VMEM((GLA_HEADS, SEQ, GLA_HEADS * GLA_DK), BF16),
                        pltpu.VMEM((SEQ, GLA_HEADS * GLA_DK), BF16),
                        pltpu.VMEM((GLA_HEADS * GLA_DV, GLA_HEADS * GLA_DK), F32)],
        compiler_params=_cparams(("parallel",)),
        name="gla_chunked",
    )(gqk, gv, ga, gz, wg, bg, norm_g)


def _deepnorm_ln(x, y, g, b):
    r = DEEPNORM_ALPHA * x + y
    mu = jnp.mean(r, axis=-1, keepdims=True)
    d = r - mu
    var = jnp.mean(d * d, axis=-1, keepdims=True)
    return d * lax.rsqrt(var + NORM_EPS) * g + b


def _gate_expand_mat():
    e = np.zeros((LANES, 3 * 512), np.float32)
    for br in range(3):
        for h in range(NSA_HEADS):
            e[br * 8 + h, br * 512 + h * 64: br * 512 + (h + 1) * 64] = 1.0
    return jnp.asarray(e, dtype=BF16)


def _even_out_kernel(ocmp_ref, osel_ref, owin_ref, ng_ref, nz_ref, ogla_ref, x_ref,
                     e_ref, w_ref, g_ref, b_ref, o_ref):
    sig = jax.nn.sigmoid(ng_ref[...])
    sig_hi = sig.astype(BF16)
    sig_lo = (sig - sig_hi.astype(F32)).astype(BF16)
    gates = _dot(sig_hi, e_ref[...]) + _dot(sig_lo, e_ref[...])
    o_nsa = (gates[:, 0:512] * ocmp_ref[...].astype(F32)
             + gates[:, 512:1024] * osel_ref[...].astype(F32)
             + gates[:, 1024:1536] * owin_ref[...].astype(F32))
    o_nsa = (o_nsa * _silu(nz_ref[...])).astype(BF16)
    y = _dot(o_nsa, w_ref[0:512, :]) + _dot(ogla_ref[...], w_ref[512:1024, :])
    o_ref[...] = _deepnorm_ln(x_ref[...], y, g_ref[...], b_ref[...])


def _even_out(ocmp, osel, owin, ng, nz, ogla, x2, e_mat, w_bf, ln_g, ln_b, tm=256):
    T = x2.shape[0]
    row = lambda i: (i, 0)
    const = lambda i: (0, 0)
    return pl.pallas_call(
        _even_out_kernel,
        grid=(T // tm,),
        in_specs=[pl.BlockSpec((tm, 512), row), pl.BlockSpec((tm, 512), row),
                  pl.BlockSpec((tm, 512), row), pl.BlockSpec((tm, LANES), row),
                  pl.BlockSpec((tm, 512), row), pl.BlockSpec((tm, 512), row),
                  pl.BlockSpec((tm, D_MODEL), row),
                  pl.BlockSpec((LANES, 1536), const), pl.BlockSpec((1024, D_MODEL), const),
                  pl.BlockSpec((1, D_MODEL), const), pl.BlockSpec((1, D_MODEL), const)],
        out_specs=pl.BlockSpec((tm, D_MODEL), row),
        out_shape=jax.ShapeDtypeStruct((T, D_MODEL), F32),
        compiler_params=_cparams(("parallel",)),
        name="even_out_proj_ln",
    )(ocmp, osel, owin, ng, nz, ogla, x2, e_mat, w_bf, ln_g, ln_b)


def _odd_out_kernel(o_ref_in, z_ref, x_ref, w_ref, g_ref, b_ref, o_ref):
    o = (o_ref_in[...].astype(F32) * _silu(z_ref[...])).astype(BF16)
    y = _dot(o, w_ref[...])
    o_ref[...] = _deepnorm_ln(x_ref[...], y, g_ref[...], b_ref[...])


def _odd_out(o, z, x2, w_bf, ln_g, ln_b, tm=256):
    T = x2.shape[0]
    row = lambda i: (i, 0)
    const = lambda i: (0, 0)
    return pl.pallas_call(
        _odd_out_kernel,
        grid=(T // tm,),
        in_specs=[pl.BlockSpec((tm, D_MODEL), row), pl.BlockSpec((tm, D_MODEL), row),
                  pl.BlockSpec((tm, D_MODEL), row), pl.BlockSpec((D_MODEL, D_MODEL), const),
                  pl.BlockSpec((1, D_MODEL), const), pl.BlockSpec((1, D_MODEL), const)],
        out_specs=pl.BlockSpec((tm, D_MODEL), row),
        out_shape=jax.ShapeDtypeStruct((T, D_MODEL), F32),
        compiler_params=_cparams(("parallel",)),
        name="odd_out_proj_ln",
    )(o, z, x2, w_bf, ln_g, ln_b)


_HEAD_W = LANES
_QK_W = MLA_HEADS * _HEAD_W


def _rot_cols(w):
    half = MLA_ROPE // 2
    return jnp.concatenate([-w[..., half:], w[..., :half]], axis=-1)


def _odd_weights(w_in, w_uq, w_ukv):
    o = _O_OFF
    kr = w_in[:, o[2]:o[3]]
    w1 = jnp.concatenate([w_in[:, o[0]:o[2]], kr, _rot_cols(kr),
                          jnp.zeros((D_MODEL, LANES - 2 * MLA_ROPE), F32),
                          w_in[:, o[3]:o[4]]], axis=1).astype(BF16)
    uq = w_uq.reshape(MLA_Q_LORA, MLA_HEADS, MLA_NOPE + MLA_ROPE)
    zq = jnp.zeros((MLA_Q_LORA, MLA_HEADS, _HEAD_W - MLA_NOPE - MLA_ROPE), F32)
    wq_a = jnp.concatenate([uq, zq], axis=-1).reshape(MLA_Q_LORA, _QK_W).astype(BF16)
    wq_b = jnp.concatenate([jnp.zeros((MLA_Q_LORA, MLA_HEADS, MLA_NOPE), F32),
                            _rot_cols(uq[..., MLA_NOPE:]), zq], axis=-1)
    wq_b = wq_b.reshape(MLA_Q_LORA, _QK_W).astype(BF16)
    ukv = w_ukv.reshape(MLA_KV_LORA, MLA_HEADS, MLA_NOPE + MLA_V)
    wk = jnp.concatenate([ukv[..., :MLA_NOPE],
                          jnp.zeros((MLA_KV_LORA, MLA_HEADS, _HEAD_W - MLA_NOPE), F32)], axis=-1)
    wk = wk.reshape(MLA_KV_LORA, _QK_W).astype(BF16)
    wv = jnp.concatenate([ukv[..., MLA_NOPE:],
                          jnp.zeros((MLA_KV_LORA, MLA_HEADS, _HEAD_W - MLA_V), F32)], axis=-1)
    wv = wv.reshape(MLA_KV_LORA, _QK_W).astype(BF16)
    return w1, wq_a, wq_b, wk, wv


def _rope_tables():
    half = MLA_ROPE // 2
    freqs = jnp.exp(-math.log(ROPE_THETA) * jnp.arange(half, dtype=F32) * 2.0 / MLA_ROPE)
    ang = jnp.arange(SEQ, dtype=F32)[:, None] * freqs[None, :]
    cos = jnp.concatenate([jnp.cos(ang), jnp.cos(ang)], axis=1)
    sin = jnp.concatenate([jnp.sin(ang), jnp.sin(ang)], axis=1)
    scale = (MLA_NOPE + MLA_ROPE) ** -0.5 * math.log2(math.e)
    pad = jnp.zeros((SEQ, LANES - MLA_NOPE - MLA_ROPE), F32)
    qc = jnp.concatenate([jnp.full((SEQ, MLA_NOPE), scale, F32), cos * scale, pad], axis=1)
    qs = jnp.concatenate([jnp.zeros((SEQ, MLA_NOPE), F32), sin * scale, pad], axis=1)
    kcs = jnp.concatenate([cos, sin, jnp.zeros((SEQ, LANES - 2 * MLA_ROPE), F32)], axis=1)
    place = np.zeros((LANES, _QK_W), np.float32)
    for h in range(MLA_HEADS):
        for c in range(MLA_ROPE):
            place[c, h * _HEAD_W + MLA_NOPE + c] = 1.0
            place[MLA_ROPE + c, h * _HEAD_W + MLA_NOPE + c] = 1.0
    v_one = np.zeros((1, _QK_W), np.float32)
    v_one[0, np.arange(MLA_HEADS) * _HEAD_W + MLA_V] = 1.0
    return qc, qs, kcs, jnp.asarray(place, dtype=BF16), jnp.asarray(v_one)


def _rms(x, g):
    return x * lax.rsqrt(jnp.mean(x * x, axis=-1, keepdims=True) + NORM_EPS) * g


def _odd_in_kernel(x_ref, w1_ref, wqa_ref, wqb_ref, wk_ref, wv_ref, place_ref,
                   qn_ref, kn_ref, vone_ref, qc_ref, qs_ref, kcs_ref,
                   q_ref, k_ref, v_ref, z_ref):
    xb = x_ref[...].astype(BF16)
    c_q = _dot(xb, w1_ref[:, 0:MLA_Q_LORA])
    c_kv = _dot(xb, w1_ref[:, MLA_Q_LORA:640])
    kr = _dot(xb, w1_ref[:, 640:768])
    z_ref[...] = _dot(xb, w1_ref[:, 768:1792])
    cqn = _rms(c_q, qn_ref[...]).astype(BF16)
    ckvn = _rms(c_kv, kn_ref[...]).astype(BF16)
    qa = _dot(cqn, wqa_ref[...])
    qb = _dot(cqn, wqb_ref[...])
    qc = qc_ref[...]
    qs = qs_ref[...]
    for h in range(MLA_HEADS):
        sl = slice(h * _HEAD_W, (h + 1) * _HEAD_W)
        q_ref[:, sl] = (qa[:, sl] * qc + qb[:, sl] * qs).astype(BF16)
    kr_r = (kr * kcs_ref[...]).astype(BF16)
    k_ref[...] = (_dot(ckvn, wk_ref[...]) + _dot(kr_r, place_ref[...])).astype(BF16)
    v_ref[...] = (_dot(ckvn, wv_ref[...]) + vone_ref[...]).astype(BF16)


def _odd_in_proj(x2, w1, wqa, wqb, wk, wv, place, qn, kn, vone, qc, qs, kcs, tm=256):
    T = x2.shape[0]
    s_tiles = SEQ // tm
    row = lambda i: (i, 0)
    const = lambda i: (0, 0)
    pos = lambda i: (i % s_tiles, 0)

    def full(a):
        return pl.BlockSpec(a.shape, const)

    return pl.pallas_call(
        _odd_in_kernel,
        grid=(T // tm,),
        in_specs=[pl.BlockSpec((tm, D_MODEL), row), full(w1), full(wqa), full(wqb), full(wk),
                  full(wv), full(place), full(qn), full(kn), full(vone),
                  pl.BlockSpec((tm, LANES), pos), pl.BlockSpec((tm, LANES), pos),
                  pl.BlockSpec((tm, LANES), pos)],
        out_specs=[pl.BlockSpec((tm, _QK_W), row), pl.BlockSpec((tm, _QK_W), row),
                   pl.BlockSpec((tm, _QK_W), row), pl.BlockSpec((tm, 1024), row)],
        out_shape=[jax.ShapeDtypeStruct((T, _QK_W), BF16), jax.ShapeDtypeStruct((T, _QK_W), BF16),
                   jax.ShapeDtypeStruct((T, _QK_W), BF16), jax.ShapeDtypeStruct((T, 1024), F32)],
        compiler_params=_cparams(("parallel",)),
        name="odd_in_proj",
    )(x2, w1, wqa, wqb, wk, wv, place, qn, kn, vone, qc, qs, kcs)


def _mla_kernel(q_ref, k_ref, v_ref, o_ref, s_ref, bias_ref, *, tq):
    nq = SEQ // tq
    a_idx = lax.broadcasted_iota(jnp.int32, (tq, tq), 0)
    b_idx = lax.broadcasted_iota(jnp.int32, (tq, tq), 1)
    bias_ref[...] = jnp.where(a_idx >= b_idx, 0.0, NEG_INF)
    lane = lax.broadcasted_iota(jnp.int32, (tq, LANES), 1)
    for i in range(nq):
        rows = slice(i * tq, (i + 1) * tq)
        outs = []
        for h in range(2):
            hs = slice(h * _HEAD_W, (h + 1) * _HEAD_W)
            q = q_ref[rows, hs]
            mx = None
            for j in range(i + 1):
                cols = slice(j * tq, (j + 1) * tq)
                s = _dot_nt(q, k_ref[cols, hs])
                if j == i:
                    s = s + bias_ref[...]
                s_ref[h, :, cols] = s
                t = s[:, 0:LANES]
                for c in range(1, tq // LANES):
                    t = jnp.maximum(t, s[:, c * LANES:(c + 1) * LANES])
                mx = t if mx is None else jnp.maximum(mx, t)
            m = jnp.max(mx, axis=1, keepdims=True)
            acc = None
            for j in range(i + 1):
                cols = slice(j * tq, (j + 1) * tq)
                p = jnp.exp2(s_ref[h, :, cols] - m).astype(BF16)
                pv = _dot(p, v_ref[cols, hs])
                acc = pv if acc is None else acc + pv
            outs.append(acc / acc[:, MLA_V:MLA_V + 1])
        o = jnp.where(lane < MLA_V, outs[0], pltpu.roll(outs[1], MLA_V, 1))
        o_ref[rows, :] = o.astype(o_ref.dtype)


def _mla_attn(q, k, v, B, tq=256):
    kern = functools.partial(_mla_kernel, tq=tq)
    pair = pl.BlockSpec((SEQ, 2 * _HEAD_W), lambda b, h: (b, h))
    return pl.pallas_call(
        kern,
        grid=(B, MLA_HEADS // 2),
        in_specs=[pair, pair, pair],
        out_specs=pl.BlockSpec((SEQ, LANES), lambda b, h: (b, h)),
        out_shape=jax.ShapeDtypeStruct((B * SEQ, MLA_HEADS * MLA_V), BF16),
        scratch_shapes=[pltpu.VMEM((2, tq, SEQ), F32), pltpu.VMEM((tq, tq), F32)],
        compiler_params=_cparams(("parallel", "parallel")),
        name="mla_flash_attn",
    )(q, k, v)


def _even_layer(x2, B, w_in, cmp_pe, cmp_w1, cmp_w2, gla_w_gate, gla_b_gate, gla_norm,
                w_out, ln_g, ln_b):
    T = x2.shape[0]
    q, kvc, kv, ng, nz, gqk, gv, ga, gz = _even_in_proj(x2, _even_in_weight(w_in), _nsa_kv_table())

    pe_tab, w1_bd = _compress_weights(cmp_pe, cmp_w1)
    kc, vct = _compress(kvc, pe_tab, w1_bd, cmp_w2, B)

    ocmp, qaug = _cmp_attn(q, kc, vct, _overlap_t(), _q_alibi_table(), _head_select_mats(),
                           _cmp_bias_table(), B)
    kv2 = kv.reshape(B * SEQ, 1024)
    osel, owin = _sel_win_attn(qaug, kv2, _place_mats(), B)

    wg = jnp.pad(gla_w_gate, ((0, LANES - GLA_GATE_RANK), (0, 0))).astype(BF16)
    ogla = _gla(gqk, gv, ga, gz, wg, gla_b_gate.reshape(1, -1), gla_norm.reshape(1, -1), B)

    return _even_out(ocmp, osel, owin, ng, nz, ogla, x2, _gate_expand_mat(), w_out.astype(BF16),
                     ln_g.reshape(1, -1), ln_b.reshape(1, -1))


def _odd_layer(x2, B, w_in, q_norm, w_uq, kv_norm, w_ukv, w_out, ln_g, ln_b):
    w1, wqa, wqb, wk, wv = _odd_weights(w_in, w_uq, w_ukv)
    qc, qs, kcs, place, vone = _rope_tables()
    q, k, v, z = _odd_in_proj(x2, w1, wqa, wqb, wk, wv, place, q_norm.reshape(1, -1),
                              kv_norm.reshape(1, -1), vone, qc, qs, kcs)
    o = _mla_attn(q, k, v, B)
    return _odd_out(o, z, x2, w_out.astype(BF16), ln_g.reshape(1, -1), ln_b.reshape(1, -1))


def kernel(x, e_w_in, e_cmp_pe, e_cmp_w1, e_cmp_w2, e_gla_w_gate, e_gla_b_gate, e_gla_norm,
           e_w_out, e_ln_g, e_ln_b, o_w_in, o_q_norm, o_w_uq, o_kv_norm, o_w_ukv, o_w_out,
           o_ln_g, o_ln_b):
    B, S, D = x.shape
    x2 = x.reshape(B * S, D)
    for layer in range(DEPTH):
        i = layer // 2
        if layer % 2 == 0:
            x2 = _even_layer(x2, B, e_w_in[i], e_cmp_pe[i], e_cmp_w1[i], e_cmp_w2[i],
                             e_gla_w_gate[i], e_gla_b_gate[i], e_gla_norm[i], e_w_out[i],
                             e_ln_g[i], e_ln_b[i])
        else:
            x2 = _odd_layer(x2, B, o_w_in[i], o_q_norm[i], o_w_uq[i], o_kv_norm[i], o_w_ukv[i],
                            o_w_out[i], o_ln_g[i], o_ln_b[i])
    return x2.reshape(B, S, D)
```

```python
import functools
import math

import jax
import jax.numpy as jnp
import numpy as np
from jax import lax
from jax.experimental import pallas as pl
from jax.experimental.pallas import tpu as pltpu

F32 = jnp.float32
BF16 = jnp.bfloat16

D_MODEL = 1024
SEQ = 2048
DEPTH = 2
NSA_HEADS = 8
NSA_KV_HEADS = 2
NSA_GROUP = 4
NSA_HEAD_DIM = 64
CMP_BLOCK = 32
CMP_STRIDE = 16
CMP_HIDDEN = 128
N_CMP = (SEQ - CMP_BLOCK) // CMP_STRIDE + 1
N_CMP_PAD = 128
SEL_BLOCK = 64
N_SEL = SEQ // SEL_BLOCK
SEL_TOPN = 8
WINDOW = 512
FORCE_BONUS = 1000.0
GLA_HEADS = 4
GLA_DK = 64
GLA_DV = 128
GLA_GATE_RANK = 16
GLA_TAU = 16.0
GLA_CHUNK = 64
MLA_HEADS = 16
MLA_NOPE = 64
MLA_ROPE = 32
MLA_V = 64
MLA_Q_LORA = 384
MLA_KV_LORA = 256
ROPE_THETA = 10000.0
NORM_EPS = 1e-5
NEG_INF = -1e30
DEEPNORM_ALPHA = (2 * DEPTH) ** 0.25

LANES = 128
MASK_BIG = 2.0 ** 100
VMEM_LIMIT = 56 * 1024 * 1024

_E_OFF = np.cumsum([0, 512, 768, 24, 512, 256, 256, 512, 16, 512])
_O_OFF = np.cumsum([0, MLA_Q_LORA, MLA_KV_LORA, MLA_ROPE, MLA_HEADS * MLA_V])


def _cparams(sem):
    return pltpu.CompilerParams(dimension_semantics=sem, vmem_limit_bytes=VMEM_LIMIT)


def _silu(x):
    return x * jax.nn.sigmoid(x)


def _dot(a, b):
    return jnp.dot(a, b, preferred_element_type=F32)


def _dot_nt(a, b):
    return lax.dot_general(a, b, (((1,), (1,)), ((), ())), preferred_element_type=F32)


_A_SEGS = (("q", 512, BF16), ("kvc", 256, F32), ("kv", 1024, BF16), ("ng", 128, F32),
           ("nz", 512, F32), ("gqk", 512, F32), ("gv", 512, F32), ("ga", 128, F32),
           ("gz", 512, F32))
_A_OFF = np.cumsum([0] + [s[1] for s in _A_SEGS])


def _even_in_weight(w):
    o = _E_OFF
    z64 = jnp.zeros((D_MODEL, 64), F32)
    nkv = w[:, o[1]:o[2]]

    def kvcol(br, j, g):
        c = ((br * 2 + j) * NSA_KV_HEADS + g) * 64
        return nkv[:, c:c + 64]

    kv_blocks = []
    for br in (1, 2):
        for j in (0, 1):
            for g in (0, 1):
                kv_blocks += [kvcol(br, j, g), z64]
    ng = jnp.pad(w[:, o[2]:o[3]], ((0, 0), (0, LANES - 24)))
    ga = jnp.pad(w[:, o[7]:o[8]], ((0, 0), (0, LANES - GLA_GATE_RANK)))
    cols = [w[:, o[0]:o[1]] * (NSA_HEAD_DIM ** -0.5), nkv[:, :256]] + kv_blocks + [
        ng, w[:, o[3]:o[4]], w[:, o[4]:o[6]], w[:, o[6]:o[7]], ga, w[:, o[8]:o[9]]]
    return jnp.concatenate(cols, axis=1).astype(BF16)


def _nsa_kv_table():
    pos = np.arange(SEQ)
    k_al = np.zeros((SEQ, LANES), np.float32)
    k_al[:, 96] = 1.0
    k_al[:, 97] = 1.0
    k_al[:, 98] = pos // SEL_BLOCK
    k_al[:, 99] = pos % SEL_BLOCK
    k_sel = k_al.copy()
    k_sel[pos, 64 + pos // SEL_BLOCK] = 1.0
    v_one = np.zeros((SEQ, LANES), np.float32)
    v_one[:, 64] = 1.0
    tab = np.concatenate([k_sel, k_sel, v_one, v_one, k_al, k_al, v_one, v_one], axis=1)
    return jnp.asarray(tab)


def _even_in_kernel(x_ref, w_ref, tab_ref, *out_refs):
    xb = x_ref[...].astype(BF16)
    for (name, width, dt), off, o_ref in zip(_A_SEGS, _A_OFF[:-1], out_refs):
        r = _dot(xb, w_ref[:, off:off + width])
        if name == "kv":
            r = r + tab_ref[...]
        o_ref[...] = r.astype(dt)


def _even_in_proj(x2, w_bf, tab, tm=256):
    T = x2.shape[0]
    n_w = w_bf.shape[1]
    s_tiles = SEQ // tm
    out_shape = [jax.ShapeDtypeStruct((T, wd), dt) for _, wd, dt in _A_SEGS]
    out_specs = [pl.BlockSpec((tm, wd), lambda i: (i, 0)) for _, wd, _ in _A_SEGS]
    return pl.pallas_call(
        _even_in_kernel,
        grid=(T // tm,),
        in_specs=[pl.BlockSpec((tm, D_MODEL), lambda i: (i, 0)),
                  pl.BlockSpec((D_MODEL, n_w), lambda i: (0, 0)),
                  pl.BlockSpec((tm, 1024), lambda i: (i % s_tiles, 0))],
        out_specs=out_specs,
        out_shape=out_shape,
        compiler_params=_cparams(("parallel",)),
        name="even_in_proj",
    )(x2, w_bf, tab)


def _compress_weights(cmp_pe, cmp_w1):
    pe_tab = jnp.concatenate([cmp_pe[0], cmp_pe[0], cmp_pe[1], cmp_pe[1]], axis=1)
    w = cmp_w1.reshape(2, CMP_BLOCK, NSA_HEAD_DIM, CMP_HIDDEN)
    z = jnp.zeros_like(w)
    w_bd = jnp.concatenate([jnp.concatenate([w, z], axis=3), jnp.concatenate([z, w], axis=3)], axis=2)
    return pe_tab, w_bd.astype(BF16)


def _compress_kernel(xk_ref, xv_ref, pe_ref, w1_ref, w2_ref, w2t_ref, kc_ref, vct_ref):
    half = CMP_BLOCK // 2
    acc = [[None, None], [None, None]]
    for l in range(half):
        for j, x_ref in enumerate((xk_ref, xv_ref)):
            xs = x_ref[pl.ds(l, N_CMP_PAD, stride=CMP_STRIDE), :]
            for part in range(2):
                row = part * half + l
                xb = (xs + pe_ref[row:row + 1, j * LANES:(j + 1) * LANES]).astype(BF16)
                d = _dot(xb, w1_ref[j, row])
                acc[j][part] = d if acc[j][part] is None else acc[j][part] + d
    for j in range(2):
        h = acc[j][0] + pltpu.roll(acc[j][1], N_CMP_PAD - 1, 0)
        hs = _silu(h).astype(BF16)
        for g in range(NSA_KV_HEADS):
            hg = hs[:, g * CMP_HIDDEN:(g + 1) * CMP_HIDDEN]
            if j == 0:
                kc_ref[0, g] = _dot(hg, w2_ref[...])
            else:
                vct_ref[0, g] = _dot_nt(w2t_ref[...], hg)


def _compress(kvc, pe_tab, w1_bd, w2, B):
    w2k = w2[0].astype(BF16)
    w2vt = w2[1].T.astype(BF16)
    return pl.pallas_call(
        _compress_kernel,
        grid=(B,),
        in_specs=[pl.BlockSpec((SEQ, LANES), lambda b: (b, 0)),
                  pl.BlockSpec((SEQ, LANES), lambda b: (b, 1)),
                  pl.BlockSpec((CMP_BLOCK, 256), lambda b: (0, 0)),
                  pl.BlockSpec((2, CMP_BLOCK, LANES, 256), lambda b: (0, 0, 0, 0)),
                  pl.BlockSpec((CMP_HIDDEN, NSA_HEAD_DIM), lambda b: (0, 0)),
                  pl.BlockSpec((NSA_HEAD_DIM, CMP_HIDDEN), lambda b: (0, 0))],
        out_specs=[pl.BlockSpec((1, NSA_KV_HEADS, N_CMP_PAD, NSA_HEAD_DIM), lambda b: (b, 0, 0, 0)),
                   pl.BlockSpec((1, NSA_KV_HEADS, NSA_HEAD_DIM, N_CMP_PAD), lambda b: (b, 0, 0, 0))],
        out_shape=[jax.ShapeDtypeStruct((B, NSA_KV_HEADS, N_CMP_PAD, NSA_HEAD_DIM), F32),
                   jax.ShapeDtypeStruct((B, NSA_KV_HEADS, NSA_HEAD_DIM, N_CMP_PAD), F32)],
        compiler_params=_cparams(("parallel",)),
        name="nsa_compress",
    )(kvc, kvc, pe_tab, w1_bd, w2k, w2vt)


def _alibi_slopes_np():
    return np.exp2(-(8.0 / NSA_HEADS) * np.arange(1, NSA_HEADS + 1)).astype(np.float32)


def _overlap_t():
    cs = np.arange(N_CMP) * CMP_STRIDE
    ss = np.arange(N_SEL) * SEL_BLOCK
    ov = np.clip(np.minimum(cs[:, None] + CMP_BLOCK, ss[None, :] + SEL_BLOCK)
                 - np.maximum(cs[:, None], ss[None, :]), 0, None).astype(np.float32) / CMP_BLOCK
    ovt = np.zeros((N_SEL, N_CMP_PAD), np.float32)
    ovt[:, :N_CMP] = ov.T
    return jnp.asarray(ovt)


def _q_alibi_table():
    slopes = _alibi_slopes_np().reshape(NSA_KV_HEADS, NSA_GROUP)
    pos = np.arange(SEQ)
    tab = np.zeros((NSA_KV_HEADS, SEQ, LANES), np.float32)
    for g in range(NSA_KV_HEADS):
        for r in range(NSA_GROUP):
            m = slopes[g, r]
            tab[g, :, r * 32 + 0] = -m * SEL_BLOCK * (pos // SEL_BLOCK)
            tab[g, :, r * 32 + 1] = -m * (pos % SEL_BLOCK)
            tab[g, :, r * 32 + 2] = m * SEL_BLOCK
            tab[g, :, r * 32 + 3] = m
    return jnp.asarray(tab)


def _head_select_mats():
    m = np.zeros((NSA_GROUP, NSA_HEAD_DIM, 256), np.float32)
    for r in range(NSA_GROUP):
        m[r, np.arange(64), r * 64 + np.arange(64)] = 1.0
    return jnp.asarray(m, dtype=BF16)


def _cmp_bias_table():
    slopes = jnp.asarray(_alibi_slopes_np()).reshape(NSA_KV_HEADS, NSA_GROUP, 1, 1)
    n = jnp.arange(N_CMP_PAD)[:, None]
    t = jnp.arange(SEQ)[None, :]
    dist = t - (n * CMP_STRIDE + (CMP_BLOCK - 1))
    visible = (dist >= 0) & (n < N_CMP)
    return jnp.where(visible, -slopes * dist.astype(F32), NEG_INF)


def _cmp_attn_kernel(q_ref, kc_ref, vct_ref, ovt_ref, qal_ref, hs_ref, bias_ref,
                     ocmp_ref, qaug_ref, *, tq):
    kc = kc_ref[0, 0].astype(BF16)
    vct = vct_ref[0, 0].astype(BF16)
    k_ext = [_dot(kc, hs_ref[r]).astype(BF16) for r in range(NSA_GROUP)]
    ri = lax.broadcasted_iota(jnp.int32, (256, 256), 0)
    ci = lax.broadcasted_iota(jnp.int32, (256, 256), 1)
    eye = jnp.where(ri == ci, 1.0, 0.0).astype(BF16)
    sub8 = lax.broadcasted_iota(jnp.int32, (8, tq), 0)
    j_idx = lax.broadcasted_iota(jnp.int32, (N_SEL, tq), 0)
    t_lane = lax.broadcasted_iota(jnp.int32, (N_SEL, tq), 1)
    n_grp = N_SEL // 8
    for i in range(SEQ // tq):
        cols = slice(i * tq, (i + 1) * tq)
        nv = min(N_CMP_PAD, (i + 1) * tq // CMP_STRIDE)
        q = q_ref[cols, :]
        lhs = jnp.concatenate([k[:nv] for k in k_ext] + [eye], axis=0)
        res = _dot_nt(lhs, q)
        q_t = res[NSA_GROUP * nv:]
        p_sum = None
        o_parts = []
        for r in range(NSA_GROUP):
            s = res[r * nv:(r + 1) * nv] + bias_ref[0, r, 0:nv, cols]
            m = jnp.max(s, axis=0, keepdims=True)
            e = jnp.exp(s - m)
            if i == 0:
                e = jnp.where(bias_ref[0, r, 0:nv, cols] > 0.5 * NEG_INF, e, 0.0)
            l = jnp.sum(e, axis=0, keepdims=True)
            p = e / jnp.where(l > 0.0, l, 1.0)
            p_sum = p if p_sum is None else p_sum + p
            o_parts.append(_dot(vct[:, :nv], p.astype(BF16)))
        ocmp_ref[cols, :] = jnp.concatenate(o_parts, axis=0).T.astype(ocmp_ref.dtype)

        imp = jnp.dot(ovt_ref[:, 0:nv], p_sum, preferred_element_type=F32,
                      precision=lax.Precision.HIGHEST)
        cur = (i * tq + t_lane) // SEL_BLOCK
        forced = (j_idx == 0) | (j_idx == cur) | (j_idx == cur - 1)
        imp = jnp.where(j_idx > cur, -1.0, imp + jnp.where(forced, FORCE_BONUS, 0.0))
        grp = [imp[8 * a:8 * (a + 1)] for a in range(n_grp)]
        rank = [jnp.zeros((8, tq), F32) for _ in range(n_grp)]
        for k in range(N_SEL):
            row = jnp.broadcast_to(imp[k:k + 1, :], (8, tq))
            for a in range(n_grp):
                ge = jnp.where(row >= grp[a], 1.0, 0.0)
                gt = jnp.where(row > grp[a], 1.0, 0.0)
                if 8 * a > k:
                    cnt = ge
                elif 8 * a + 7 <= k:
                    cnt = gt
                else:
                    cnt = jnp.where(sub8 > k - 8 * a, ge, gt)
                rank[a] = rank[a] + cnt
        sel_bias = jnp.where(jnp.concatenate(rank, axis=0) < float(SEL_TOPN), 0.0, -MASK_BIG)

        qal_t = qal_ref[0, cols, :].T
        for r in range(NSA_GROUP):
            aug_t = jnp.concatenate([q_t[r * 64:(r + 1) * 64], sel_bias,
                                     qal_t[r * 32:(r + 1) * 32, :]], axis=0)
            qaug_ref[0, 0, r, cols, :] = aug_t.T.astype(BF16)


def _cmp_attn(q, kc, vct, ovt, qal, hs, bias, B, tq=256):
    kern = functools.partial(_cmp_attn_kernel, tq=tq)
    return pl.pallas_call(
        kern,
        grid=(NSA_KV_HEADS, B),
        in_specs=[pl.BlockSpec((SEQ, 256), lambda g, b: (b, g)),
                  pl.BlockSpec((1, 1, N_CMP_PAD, NSA_HEAD_DIM), lambda g, b: (b, g, 0, 0)),
                  pl.BlockSpec((1, 1, NSA_HEAD_DIM, N_CMP_PAD), lambda g, b: (b, g, 0, 0)),
                  pl.BlockSpec((N_SEL, N_CMP_PAD), lambda g, b: (0, 0)),
                  pl.BlockSpec((1, SEQ, LANES), lambda g, b: (g, 0, 0)),
                  pl.BlockSpec((NSA_GROUP, NSA_HEAD_DIM, 256), lambda g, b: (0, 0, 0)),
                  pl.BlockSpec((1, NSA_GROUP, N_CMP_PAD, SEQ), lambda g, b: (g, 0, 0, 0))],
        out_specs=[pl.BlockSpec((SEQ, 256), lambda g, b: (b, g)),
                   pl.BlockSpec((1, 1, NSA_GROUP, SEQ, LANES), lambda g, b: (b, g, 0, 0, 0))],
        out_shape=[jax.ShapeDtypeStruct((B * SEQ, 512), BF16),
                   jax.ShapeDtypeStruct((B, NSA_KV_HEADS, NSA_GROUP, SEQ, LANES), BF16)],
        compiler_params=_cparams(("parallel", "parallel")),
        name="nsa_cmp_attn_select",
    )(q, kc, vct, ovt, qal, hs, bias)


def _place_mats():
    m = np.zeros((NSA_GROUP, LANES, 256), np.float32)
    for r in range(NSA_GROUP):
        m[r, np.arange(64), r * 64 + np.arange(64)] = 1.0
    return jnp.asarray(m, dtype=BF16)


def _sel_win_kernel(qaug_ref, ks_ref, vs_ref, kw_ref, vw_ref, place_ref,
                    osel_ref, owin_ref, s_ref, bias_ref, *, tq):
    nq = SEQ // tq
    n_back = WINDOW // tq
    a_idx = lax.broadcasted_iota(jnp.int32, (tq, tq), 0)
    b_idx = lax.broadcasted_iota(jnp.int32, (tq, tq), 1)
    bias_ref[0] = jnp.where(a_idx >= b_idx, 0.0, -MASK_BIG)
    bias_ref[1] = jnp.where(a_idx < b_idx, 0.0, -MASK_BIG)

    def attend(q, k_ref, v_ref, tiles):
        mx = None
        for n, (j, bias) in enumerate(tiles):
            s = _dot_nt(q, k_ref[j * tq:(j + 1) * tq, :])
            if bias is not None:
                s = s + jnp.concatenate([bias_ref[bias]] * NSA_GROUP, axis=0)
            s_ref[:, n * tq:(n + 1) * tq] = s
            t = s[:, 0:LANES]
            for c in range(1, tq // LANES):
                t = jnp.maximum(t, s[:, c * LANES:(c + 1) * LANES])
            mx = t if mx is None else jnp.maximum(mx, t)
        m = jnp.max(mx, axis=1, keepdims=True)
        acc = None
        for n, (j, _) in enumerate(tiles):
            p = jnp.exp(s_ref[:, n * tq:(n + 1) * tq] - m).astype(BF16)
            pv = _dot(p, v_ref[j * tq:(j + 1) * tq, :])
            acc = pv if acc is None else acc + pv
        o = (acc / acc[:, NSA_HEAD_DIM:NSA_HEAD_DIM + 1]).astype(BF16)
        out = jnp.zeros((tq, 256), F32)
        for r in range(NSA_GROUP):
            out = out + _dot(o[r * tq:(r + 1) * tq, :], place_ref[r])
        return out

    for i in range(nq):
        rows = slice(i * tq, (i + 1) * tq)
        q = jnp.concatenate([qaug_ref[0, 0, r, rows, :] for r in range(NSA_GROUP)], axis=0)
        sel_tiles = [(j, None) for j in range(i)] + [(i, 0)]
        osel_ref[rows, :] = attend(q, ks_ref, vs_ref, sel_tiles).astype(osel_ref.dtype)
        win_tiles = [(i - d, 1 if d == n_back else None) for d in range(n_back, 0, -1) if i >= d]
        win_tiles.append((i, 0))
        owin_ref[rows, :] = attend(q, kw_ref, vw_ref, win_tiles).astype(owin_ref.dtype)


def _sel_win_attn(qaug, kv, place, B, tq=256):
    kern = functools.partial(_sel_win_kernel, tq=tq)
    rows = NSA_GROUP * tq

    def kv_spec(col):
        return pl.BlockSpec((SEQ, LANES), lambda b, g: (b, col + g))

    out_spec = pl.BlockSpec((SEQ, 256), lambda b, g: (b, g))
    return pl.pallas_call(
        kern,
        grid=(B, NSA_KV_HEADS),
        in_specs=[pl.BlockSpec((1, 1, NSA_GROUP, SEQ, LANES), lambda b, g: (b, g, 0, 0, 0)),
                  kv_spec(0), kv_spec(2), kv_spec(4), kv_spec(6),
                  pl.BlockSpec((NSA_GROUP, LANES, 256), lambda b, g: (0, 0, 0))],
        out_specs=[out_spec, out_spec],
        out_shape=[jax.ShapeDtypeStruct((B * SEQ, 512), BF16),
                   jax.ShapeDtypeStruct((B * SEQ, 512), BF16)],
        scratch_shapes=[pltpu.VMEM((rows, SEQ), F32), pltpu.VMEM((2, tq, tq), F32)],
        compiler_params=_cparams(("parallel", "parallel")),
        name="nsa_sel_win_attn",
    )(qaug, kv, kv, kv, kv, place)


_GLA_ROWS = 512


def _gla_kernel(qk_ref, v_ref, ga_ref, gz_ref, wg_ref, bg_ref, ng_ref, o_ref,
                b_ref, qd_ref, kd_ref, st_ref):
    C = GLA_CHUNK
    KW = GLA_HEADS * GLA_DK
    wg = wg_ref[...]
    bg = bg_ref[...]
    norm_g = ng_ref[...]

    rin = lax.broadcasted_iota(jnp.int32, (_GLA_ROWS, KW), 0) % C
    lane = lax.broadcasted_iota(jnp.int32, (_GLA_ROWS, KW), 1)
    for blk in range(SEQ // _GLA_ROWS):
        rows = slice(blk * _GLA_ROWS, (blk + 1) * _GLA_ROWS)
        x = _dot(ga_ref[rows, :].astype(BF16), wg) + bg
        b = -(jnp.maximum(-x, 0.0) + jnp.log1p(jnp.exp(-jnp.abs(x)))) / GLA_TAU
        sh = 1
        while sh < C:
            b = b + jnp.where(rin >= sh, pltpu.roll(b, sh, 0), 0.0)
            sh *= 2
        b_ref[rows, :] = b
        q_d = qk_ref[rows, 0:KW] * (GLA_DK ** -0.5) * jnp.exp(b)
        for h in range(GLA_HEADS):
            in_head = (lane >= h * GLA_DK) & (lane < (h + 1) * GLA_DK)
            qd_ref[h, rows, :] = jnp.where(in_head, q_d, 0.0).astype(BF16)
        kd_ref[rows, :] = (qk_ref[rows, KW:2 * KW] * jnp.exp(-b)).astype(BF16)

    r_idx = lax.broadcasted_iota(jnp.int32, (C, C), 0)
    c_idx = lax.broadcasted_iota(jnp.int32, (C, C), 1)
    causal = r_idx >= c_idx
    st_ref[...] = jnp.zeros(st_ref.shape, F32)
    for n in range(SEQ // C):
        rows = slice(n * C, (n + 1) * C)
        b = b_ref[rows, :]
        b_last = b_ref[n * C + C - 1:(n + 1) * C, :]
        k_u = (qk_ref[rows, KW:2 * KW] * jnp.exp(b_last - b)).astype(BF16)
        k_d = kd_ref[rows, :]
        v = v_ref[rows, :].astype(BF16)
        st = st_ref[...]
        st_b = st.astype(BF16)
        for h in range(GLA_HEADS):
            vs = slice(h * GLA_DV, (h + 1) * GLA_DV)
            q_h = qd_ref[h, rows, :]
            a = jnp.where(causal, _dot_nt(q_h, k_d), 0.0)
            o = _dot(a.astype(BF16), v[:, vs]) + _dot_nt(q_h, st_b[vs, :])
            y = o * lax.rsqrt(jnp.mean(o * o, axis=-1, keepdims=True) + NORM_EPS) * norm_g
            y = y * _silu(gz_ref[rows, vs])
            o_ref[rows, vs] = y.astype(o_ref.dtype)
        upd = lax.dot_general(v, k_u, (((0,), (0,)), ((), ())), preferred_element_type=F32)
        st_ref[...] = st * jnp.exp(b_last) + upd


def _gla(gqk, gv, ga, gz, wg, bg, norm_g, B):
    row = lambda b: (b, 0)
    const = lambda b: (0, 0)
    return pl.pallas_call(
        _gla_kernel,
        grid=(B,),
        in_specs=[pl.BlockSpec((SEQ, 512), row), pl.BlockSpec((SEQ, 512), row),
                  pl.BlockSpec((SEQ, LANES), row), pl.BlockSpec((SEQ, 512), row),
                  pl.BlockSpec((LANES, 256), const), pl.BlockSpec((1, 256), const),
                  pl.BlockSpec((1, GLA_DV), const)],
        out_specs=pl.BlockSpec((SEQ, 512), row),
        out_shape=jax.ShapeDtypeStruct((B * SEQ, 512), BF16),
        scratch_shapes=[pltpu.VMEM((SEQ, GLA_HEADS * GLA_DK), F32),
                        pltpu.VMEM((GLA_HEADS, SEQ, GLA_HEADS * GLA_DK), BF16),
                        pltpu.VMEM((SEQ, GLA_HEADS * GLA_DK), BF16),
                        pltpu.VMEM((GLA_HEADS * GLA_DV, GLA_HEADS * GLA_DK), F32)],
        compiler_params=_cparams(("parallel",)),
        name="gla_chunked",
    )(gqk, gv, ga, gz, wg, bg, norm_g)


def _deepnorm_ln(x, y, g, b):
    r = DEEPNORM_ALPHA * x + y
    mu = jnp.mean(r, axis=-1, keepdims=True)
    d = r - mu
    var = jnp.mean(d * d, axis=-1, keepdims=True)
    return d * lax.rsqrt(var + NORM_EPS) * g + b


def _gate_expand_mat():
    e = np.zeros((LANES, 3 * 512), np.float32)
    for br in range(3):
        for h in range(NSA_HEADS):
            e[br * 8 + h, br * 512 + h * 64: br * 512 + (h + 1) * 64] = 1.0
    return jnp.asarray(e, dtype=BF16)


def _even_out_kernel(ocmp_ref, osel_ref, owin_ref, ng_ref, nz_ref, ogla_ref, x_ref,
                     e_ref, w_ref, g_ref, b_ref, o_ref):
    sig = jax.nn.sigmoid(ng_ref[...])
    sig_hi = sig.astype(BF16)
    sig_lo = (sig - sig_hi.astype(F32)).astype(BF16)
    gates = _dot(sig_hi, e_ref[...]) + _dot(sig_lo, e_ref[...])
    o_nsa = (gates[:, 0:512] * ocmp_ref[...].astype(F32)
             + gates[:, 512:1024] * osel_ref[...].astype(F32)
             + gates[:, 1024:1536] * owin_ref[...].astype(F32))
    o_nsa = (o_nsa * _silu(nz_ref[...])).astype(BF16)
    y = _dot(o_nsa, w_ref[0:512, :]) + _dot(ogla_ref[...], w_ref[512:1024, :])
    o_ref[...] = _deepnorm_ln(x_ref[...], y, g_ref[...], b_ref[...])


def _even_out(ocmp, osel, owin, ng, nz, ogla, x2, e_mat, w_bf, ln_g, ln_b, tm=256):
    T = x2.shape[0]
    row = lambda i: (i, 0)
    const = lambda i: (0, 0)
    return pl.pallas_call(
        _even_out_kernel,
        grid=(T // tm,),
        in_specs=[pl.BlockSpec((tm, 512), row), pl.BlockSpec((tm, 512), row),
                  pl.BlockSpec((tm, 512), row), pl.BlockSpec((tm, LANES), row),
                  pl.BlockSpec((tm, 512), row), pl.BlockSpec((tm, 512), row),
                  pl.BlockSpec((tm, D_MODEL), row),
                  pl.BlockSpec((LANES, 1536), const), pl.BlockSpec((1024, D_MODEL), const),
                  pl.BlockSpec((1, D_MODEL), const), pl.BlockSpec((1, D_MODEL), const)],
        out_specs=pl.BlockSpec((tm, D_MODEL), row),
        out_shape=jax.ShapeDtypeStruct((T, D_MODEL), F32),
        compiler_params=_cparams(("parallel",)),
        name="even_out_proj_ln",
    )(ocmp, osel, owin, ng, nz, ogla, x2, e_mat, w_bf, ln_g, ln_b)


def _odd_out_kernel(o_ref_in, z_ref, x_ref, w_ref, g_ref, b_ref, o_ref):
    o = (o_ref_in[...].astype(F32) * _silu(z_ref[...])).astype(BF16)
    y = _dot(o, w_ref[...])
    o_ref[...] = _deepnorm_ln(x_ref[...], y, g_ref[...], b_ref[...])


def _odd_out(o, z, x2, w_bf, ln_g, ln_b, tm=256):
    T = x2.shape[0]
    row = lambda i: (i, 0)
    const = lambda i: (0, 0)
    return pl.pallas_call(
        _odd_out_kernel,
        grid=(T // tm,),
        in_specs=[pl.BlockSpec((tm, D_MODEL), row), pl.BlockSpec((tm, D_MODEL), row),
                  pl.BlockSpec((tm, D_MODEL), row), pl.BlockSpec((D_MODEL, D_MODEL), const),
                  pl.BlockSpec((1, D_MODEL), const), pl.BlockSpec((1, D_MODEL), const)],
        out_specs=pl.BlockSpec((tm, D_MODEL), row),
        out_shape=jax.ShapeDtypeStruct((T, D_MODEL), F32),
        compiler_params=_cparams(("parallel",)),
        name="odd_out_proj_ln",
    )(o, z, x2, w_bf, ln_g, ln_b)


_HEAD_W = LANES
_QK_W = MLA_HEADS * _HEAD_W
_ROPE_HALF = MLA_ROPE // 2


def _odd_weights(w_in, w_uq, w_ukv):
    o = _O_OFF
    z_nope = jnp.zeros((D_MODEL, MLA_NOPE), F32)
    z_tail = jnp.zeros((D_MODEL, _HEAD_W - MLA_NOPE - MLA_ROPE), F32)
    w1 = jnp.concatenate([w_in[:, o[0]:o[2]], z_nope, w_in[:, o[2]:o[3]], z_tail,
                          w_in[:, o[3]:o[4]]], axis=1).astype(BF16)
    uq = w_uq.reshape(MLA_Q_LORA, MLA_HEADS, MLA_NOPE + MLA_ROPE)
    zq = jnp.zeros((MLA_Q_LORA, MLA_HEADS, _HEAD_W - MLA_NOPE - MLA_ROPE), F32)
    wq = jnp.concatenate([uq, zq], axis=-1).reshape(MLA_Q_LORA, _QK_W).astype(BF16)
    ukv = w_ukv.reshape(MLA_KV_LORA, MLA_HEADS, MLA_NOPE + MLA_V)
    wk = ukv[..., :MLA_NOPE].reshape(MLA_KV_LORA, MLA_HEADS * MLA_NOPE).astype(BF16)
    wv = ukv[..., MLA_NOPE:].reshape(MLA_KV_LORA, MLA_HEADS * MLA_V).astype(BF16)
    return w1, wq, wk, wv


def _rope_tables():
    freqs = jnp.exp(-math.log(ROPE_THETA) * jnp.arange(_ROPE_HALF, dtype=F32) * 2.0 / MLA_ROPE)
    ang = jnp.arange(SEQ, dtype=F32)[:, None] * freqs[None, :]
    cos, sin = jnp.cos(ang), jnp.sin(ang)
    z_half = jnp.zeros((SEQ, _ROPE_HALF), F32)
    z_tail = jnp.zeros((SEQ, _HEAD_W - MLA_NOPE - MLA_ROPE), F32)
    z_nope = jnp.zeros((SEQ, MLA_NOPE), F32)

    def tables(scale, nope_gain):
        nope = jnp.full((SEQ, MLA_NOPE), nope_gain, F32)
        c = jnp.concatenate([nope, cos * scale, cos * scale, z_tail], axis=1)
        s1 = jnp.concatenate([z_nope, -sin * scale, z_half, z_tail], axis=1)
        s2 = jnp.concatenate([z_nope, z_half, sin * scale, z_tail], axis=1)
        return c, s1, s2

    q_scale = (MLA_NOPE + MLA_ROPE) ** -0.5 * math.log2(math.e)
    return tables(q_scale, q_scale), tables(1.0, 0.0)


def _rope_block(x, c, s1, s2):
    return x * c + pltpu.roll(x, LANES - _ROPE_HALF, 1) * s1 + pltpu.roll(x, _ROPE_HALF, 1) * s2


def _rms(x, g):
    return x * lax.rsqrt(jnp.mean(x * x, axis=-1, keepdims=True) + NORM_EPS) * g


def _odd_in_kernel(x_ref, w1_ref, wq_ref, wk_ref, wv_ref, qn_ref, kn_ref,
                   qc_ref, qs1_ref, qs2_ref, kc_ref, ks1_ref, ks2_ref,
                   q_ref, k_ref, kr_ref, v_ref, z_ref):
    xb = x_ref[...].astype(BF16)
    c_q = _dot(xb, w1_ref[:, 0:MLA_Q_LORA])
    c_kv = _dot(xb, w1_ref[:, MLA_Q_LORA:640])
    kr = _dot(xb, w1_ref[:, 640:768])
    z_ref[...] = _dot(xb, w1_ref[:, 768:1792])
    cqn = _rms(c_q, qn_ref[...]).astype(BF16)
    ckvn = _rms(c_kv, kn_ref[...]).astype(BF16)
    qa = _dot(cqn, wq_ref[...])
    qc, qs1, qs2 = qc_ref[...], qs1_ref[...], qs2_ref[...]
    for h in range(MLA_HEADS):
        sl = slice(h * _HEAD_W, (h + 1) * _HEAD_W)
        q_ref[:, sl] = _rope_block(qa[:, sl], qc, qs1, qs2).astype(BF16)
    kr_ref[...] = _rope_block(kr, kc_ref[...], ks1_ref[...], ks2_ref[...]).astype(BF16)
    k_ref[...] = _dot(ckvn, wk_ref[...]).astype(BF16)
    v_ref[...] = _dot(ckvn, wv_ref[...]).astype(BF16)


def _odd_in_proj(x2, w1, wq, wk, wv, qn, kn, q_tabs, k_tabs, tm=256):
    T = x2.shape[0]
    s_tiles = SEQ // tm
    row = lambda i: (i, 0)
    const = lambda i: (0, 0)
    pos = lambda i: (i % s_tiles, 0)

    def full(a):
        return pl.BlockSpec(a.shape, const)

    tab = pl.BlockSpec((tm, LANES), pos)
    n_kv = MLA_HEADS * MLA_NOPE
    return pl.pallas_call(
        _odd_in_kernel,
        grid=(T // tm,),
        in_specs=[pl.BlockSpec((tm, D_MODEL), row), full(w1), full(wq), full(wk), full(wv),
                  full(qn), full(kn), tab, tab, tab, tab, tab, tab],
        out_specs=[pl.BlockSpec((tm, _QK_W), row), pl.BlockSpec((tm, n_kv), row),
                   pl.BlockSpec((tm, LANES), row), pl.BlockSpec((tm, n_kv), row),
                   pl.BlockSpec((tm, 1024), row)],
        out_shape=[jax.ShapeDtypeStruct((T, _QK_W), BF16), jax.ShapeDtypeStruct((T, n_kv), BF16),
                   jax.ShapeDtypeStruct((T, LANES), BF16), jax.ShapeDtypeStruct((T, n_kv), BF16),
                   jax.ShapeDtypeStruct((T, 1024), F32)],
        compiler_params=_cparams(("parallel",)),
        name="odd_in_proj",
    )(x2, w1, wq, wk, wv, qn, kn, *q_tabs, *k_tabs)


_VT_ROWS = MLA_V + 16


def _mla_kernel(q_ref, kn_ref, kr_ref, v_ref, o_ref, kx_ref, qt_ref, vt_ref, s_ref, bias_ref, *, tq):
    nq = SEQ // tq
    a_idx = lax.broadcasted_iota(jnp.int32, (tq, tq), 0)
    b_idx = lax.broadcasted_iota(jnp.int32, (tq, tq), 1)
    bias_ref[...] = jnp.where(a_idx <= b_idx, 0.0, NEG_INF)
    lane = lax.broadcasted_iota(jnp.int32, (tq, LANES), 1)
    ones_rows = jnp.where(lax.broadcasted_iota(jnp.int32, (_VT_ROWS - MLA_V, tq), 0) == 0, 1.0, 0.0)

    for i in range(nq):
        rows = slice(i * tq, (i + 1) * tq)
        knp = kn_ref[rows, :].astype(F32)
        krb = kr_ref[rows, :].astype(F32)
        vt = v_ref[rows, :].astype(F32).T
        for h in range(2):
            kn_h = knp if h == 0 else pltpu.roll(knp, MLA_NOPE, 1)
            kx_ref[h, rows, :] = jnp.where(lane < MLA_NOPE, kn_h, krb).astype(BF16)
            qt_ref[h, :, rows] = q_ref[rows, h * _HEAD_W:(h + 1) * _HEAD_W].astype(F32).T.astype(BF16)
            vt_ref[h, :, rows] = jnp.concatenate(
                [vt[h * MLA_V:(h + 1) * MLA_V], ones_rows], axis=0).astype(BF16)

    units = [(i, h) for i in range(nq) for h in range(2)]
    n_buf = s_ref.shape[0]

    def stage1(n):
        i, h = units[n]
        qt = qt_ref[h, :, i * tq:(i + 1) * tq]
        mx = None
        for j in range(i + 1):
            cols = slice(j * tq, (j + 1) * tq)
            s = _dot(kx_ref[h, cols, :], qt)
            if j == i:
                s = s + bias_ref[...]
            s_ref[n % n_buf, cols, :] = s
            t = s[0:8]
            for c in range(1, tq // 8):
                t = jnp.maximum(t, s[c * 8:(c + 1) * 8])
            mx = t if mx is None else jnp.maximum(mx, t)
        return jnp.max(mx, axis=0, keepdims=True)

    def stage2(n, m):
        i, h = units[n]
        acc = None
        for j in range(i + 1):
            cols = slice(j * tq, (j + 1) * tq)
            p = jnp.exp2(s_ref[n % n_buf, cols, :] - m).astype(BF16)
            pv = _dot(vt_ref[h, :, cols], p)
            acc = pv if acc is None else acc + pv
        return acc[0:MLA_V] / acc[MLA_V:MLA_V + 1]

    ms = [stage1(0), stage1(1)]
    outs = []
    for n in range(len(units)):
        if n + 2 < len(units):
            ms.append(stage1(n + 2))
        outs.append(stage2(n, ms[n]))
        if len(outs) == 2:
            i = units[n][0]
            o_ref[i * tq:(i + 1) * tq, :] = jnp.concatenate(outs, axis=0).T.astype(o_ref.dtype)
            outs = []


def _mla_attn(q, kn, kr, v, B, tq=256):
    kern = functools.partial(_mla_kernel, tq=tq)
    pair = pl.BlockSpec((SEQ, LANES), lambda b, h: (b, h))
    return pl.pallas_call(
        kern,
        grid=(B, MLA_HEADS // 2),
        in_specs=[pl.BlockSpec((SEQ, 2 * _HEAD_W), lambda b, h: (b, h)), pair,
                  pl.BlockSpec((SEQ, LANES), lambda b, h: (b, 0)), pair],
        out_specs=pair,
        out_shape=jax.ShapeDtypeStruct((B * SEQ, MLA_HEADS * MLA_V), BF16),
        scratch_shapes=[pltpu.VMEM((2, SEQ, _HEAD_W), BF16), pltpu.VMEM((2, _HEAD_W, SEQ), BF16),
                        pltpu.VMEM((2, _VT_ROWS, SEQ), BF16), pltpu.VMEM((4, SEQ, tq), F32),
                        pltpu.VMEM((tq, tq), F32)],
        compiler_params=_cparams(("parallel", "parallel")),
        name="mla_attn",
    )(q, kn, kr, v)


def _even_layer(x2, B, w_in, cmp_pe, cmp_w1, cmp_w2, gla_w_gate, gla_b_gate, gla_norm,
                w_out, ln_g, ln_b):
    T = x2.shape[0]
    q, kvc, kv, ng, nz, gqk, gv, ga, gz = _even_in_proj(x2, _even_in_weight(w_in), _nsa_kv_table())

    pe_tab, w1_bd = _compress_weights(cmp_pe, cmp_w1)
    kc, vct = _compress(kvc, pe_tab, w1_bd, cmp_w2, B)

    ocmp, qaug = _cmp_attn(q, kc, vct, _overlap_t(), _q_alibi_table(), _head_select_mats(),
                           _cmp_bias_table(), B)
    kv2 = kv.reshape(B * SEQ, 1024)
    osel, owin = _sel_win_attn(qaug, kv2, _place_mats(), B)

    wg = jnp.pad(gla_w_gate, ((0, LANES - GLA_GATE_RANK), (0, 0))).astype(BF16)
    ogla = _gla(gqk, gv, ga, gz, wg, gla_b_gate.reshape(1, -1), gla_norm.reshape(1, -1), B)

    return _even_out(ocmp, osel, owin, ng, nz, ogla, x2, _gate_expand_mat(), w_out.astype(BF16),
                     ln_g.reshape(1, -1), ln_b.reshape(1, -1))


def _odd_layer(x2, B, w_in, q_norm, w_uq, kv_norm, w_ukv, w_out, ln_g, ln_b):
    w1, wq, wk, wv = _odd_weights(w_in, w_uq, w_ukv)
    q_tabs, k_tabs = _rope_tables()
    q, kn, kr, v, z = _odd_in_proj(x2, w1, wq, wk, wv, q_norm.reshape(1, -1),
                                   kv_norm.reshape(1, -1), q_tabs, k_tabs)
    o = _mla_attn(q, kn, kr, v, B)
    return _odd_out(o, z, x2, w_out.astype(BF16), ln_g.reshape(1, -1), ln_b.reshape(1, -1))


def kernel(x, e_w_in, e_cmp_pe, e_cmp_w1, e_cmp_w2, e_gla_w_gate, e_gla_b_gate, e_gla_norm,
           e_w_out, e_ln_g, e_ln_b, o_w_in, o_q_norm, o_w_uq, o_kv_norm, o_w_ukv, o_w_out,
           o_ln_g, o_ln_b):
    B, S, D = x.shape
    x2 = x.reshape(B * S, D)
    for layer in range(DEPTH):
        i = layer // 2
        if layer % 2 == 0:
            x2 = _even_layer(x2, B, e_w_in[i], e_cmp_pe[i], e_cmp_w1[i], e_cmp_w2[i],
                             e_gla_w_gate[i], e_gla_b_gate[i], e_gla_norm[i], e_w_out[i],
                             e_ln_g[i], e_ln_b[i])
        else:
            x2 = _odd_layer(x2, B, o_w_in[i], o_q_norm[i], o_w_uq[i], o_kv_norm[i], o_w_ukv[i],
                            o_w_out[i], o_ln_g[i], o_ln_b[i])
    return x2.reshape(B, S, D)
```

```python
import functools
import math

import jax
import jax.numpy as jnp
import numpy as np
from jax import lax
from jax.experimental import pallas as pl
from jax.experimental.pallas import tpu as pltpu

F32 = jnp.float32
BF16 = jnp.bfloat16

D_MODEL = 1024
SEQ = 2048
DEPTH = 2
NSA_HEADS = 8
NSA_KV_HEADS = 2
NSA_GROUP = 4
NSA_HEAD_DIM = 64
CMP_BLOCK = 32
CMP_STRIDE = 16
CMP_HIDDEN = 128
N_CMP = (SEQ - CMP_BLOCK) // CMP_STRIDE + 1
N_CMP_PAD = 128
SEL_BLOCK = 64
N_SEL = SEQ // SEL_BLOCK
SEL_TOPN = 8
WINDOW = 512
FORCE_BONUS = 1000.0
GLA_HEADS = 4
GLA_DK = 64
GLA_DV = 128
GLA_GATE_RANK = 16
GLA_TAU = 16.0
GLA_CHUNK = 64
MLA_HEADS = 16
MLA_NOPE = 64
MLA_ROPE = 32
MLA_V = 64
MLA_Q_LORA = 384
MLA_KV_LORA = 256
ROPE_THETA = 10000.0
NORM_EPS = 1e-5
NEG_INF = -1e30
DEEPNORM_ALPHA = (2 * DEPTH) ** 0.25

LANES = 128
MASK_BIG = 2.0 ** 100
VMEM_LIMIT = 56 * 1024 * 1024

_E_OFF = np.cumsum([0, 512, 768, 24, 512, 256, 256, 512, 16, 512])
_O_OFF = np.cumsum([0, MLA_Q_LORA, MLA_KV_LORA, MLA_ROPE, MLA_HEADS * MLA_V])


def _cparams(sem):
    return pltpu.CompilerParams(dimension_semantics=sem, vmem_limit_bytes=VMEM_LIMIT)


def _silu(x):
    return x * jax.nn.sigmoid(x)


def _dot(a, b):
    return jnp.dot(a, b, preferred_element_type=F32)


def _dot_nt(a, b):
    return lax.dot_general(a, b, (((1,), (1,)), ((), ())), preferred_element_type=F32)


_A_SEGS = (("q", 512, BF16), ("kvc", 256, F32), ("kv", 1024, BF16), ("ng", 128, F32),
           ("nz", 512, F32), ("gqk", 512, F32), ("gv", 512, F32), ("ga", 128, F32),
           ("gz", 512, F32))
_A_OFF = np.cumsum([0] + [s[1] for s in _A_SEGS])


def _even_in_weight(w):
    o = _E_OFF
    z64 = jnp.zeros((D_MODEL, 64), F32)
    nkv = w[:, o[1]:o[2]]

    def kvcol(br, j, g):
        c = ((br * 2 + j) * NSA_KV_HEADS + g) * 64
        return nkv[:, c:c + 64]

    kv_blocks = []
    for br in (1, 2):
        for j in (0, 1):
            for g in (0, 1):
                kv_blocks += [kvcol(br, j, g), z64]
    gates = w[:, o[2]:o[3]].reshape(D_MODEL, 3, NSA_KV_HEADS, NSA_GROUP).transpose(0, 2, 1, 3)
    gates = jnp.pad(gates.reshape(D_MODEL, NSA_KV_HEADS, 12), ((0, 0), (0, 0), (0, 4)))
    ng = jnp.pad(gates.reshape(D_MODEL, 32), ((0, 0), (0, LANES - 32)))
    ga = jnp.pad(w[:, o[7]:o[8]], ((0, 0), (0, LANES - GLA_GATE_RANK)))
    cols = [w[:, o[0]:o[1]] * (NSA_HEAD_DIM ** -0.5), nkv[:, :256]] + kv_blocks + [
        ng, w[:, o[3]:o[4]], w[:, o[4]:o[6]], w[:, o[6]:o[7]], ga, w[:, o[8]:o[9]]]
    return jnp.concatenate(cols, axis=1).astype(BF16)


def _nsa_kv_table():
    pos = np.arange(SEQ)
    k_al = np.zeros((SEQ, LANES), np.float32)
    k_al[:, 96] = 1.0
    k_al[:, 97] = 1.0
    k_al[:, 98] = pos // SEL_BLOCK
    k_al[:, 99] = pos % SEL_BLOCK
    k_sel = k_al.copy()
    k_sel[pos, 64 + pos // SEL_BLOCK] = 1.0
    v_one = np.zeros((SEQ, LANES), np.float32)
    v_one[:, 64] = 1.0
    tab = np.concatenate([k_sel, k_sel, v_one, v_one, k_al, k_al, v_one, v_one], axis=1)
    return jnp.asarray(tab)


def _even_in_kernel(x_ref, w_ref, wg_ref, tab_ref, *out_refs):
    xb = x_ref[...].astype(BF16)
    for (name, width, dt), off, o_ref in zip(_A_SEGS, _A_OFF[:-1], out_refs):
        if name == "ng":
            o_ref[...] = _dot_nt(wg_ref[...], xb)
            continue
        r = _dot(xb, w_ref[:, off:off + width])
        if name == "kv":
            r = r + tab_ref[...]
        o_ref[...] = r.astype(dt)


def _even_in_proj(x2, w_bf, tab, tm=256):
    T = x2.shape[0]
    n_w = w_bf.shape[1]
    s_tiles = SEQ // tm
    ng_off = _A_OFF[[n for n, _, _ in _A_SEGS].index("ng")]
    wg_t = w_bf[:, ng_off:ng_off + LANES].T
    out_shape = [jax.ShapeDtypeStruct((wd, T) if name == "ng" else (T, wd), dt)
                 for name, wd, dt in _A_SEGS]
    out_specs = [pl.BlockSpec((wd, tm), lambda i: (0, i)) if name == "ng"
                 else pl.BlockSpec((tm, wd), lambda i: (i, 0)) for name, wd, _ in _A_SEGS]
    return pl.pallas_call(
        _even_in_kernel,
        grid=(T // tm,),
        in_specs=[pl.BlockSpec((tm, D_MODEL), lambda i: (i, 0)),
                  pl.BlockSpec((D_MODEL, n_w), lambda i: (0, 0)),
                  pl.BlockSpec((LANES, D_MODEL), lambda i: (0, 0)),
                  pl.BlockSpec((tm, 1024), lambda i: (i % s_tiles, 0))],
        out_specs=out_specs,
        out_shape=out_shape,
        compiler_params=_cparams(("parallel",)),
        name="even_in_proj",
    )(x2, w_bf, wg_t, tab)


def _compress_weights(cmp_pe, cmp_w1):
    pe_tab = jnp.concatenate([cmp_pe[0], cmp_pe[0], cmp_pe[1], cmp_pe[1]], axis=1)
    w = cmp_w1.reshape(2, CMP_BLOCK, NSA_HEAD_DIM, CMP_HIDDEN)
    z = jnp.zeros_like(w)
    w_bd = jnp.concatenate([jnp.concatenate([w, z], axis=3), jnp.concatenate([z, w], axis=3)], axis=2)
    return pe_tab, w_bd.astype(BF16)


def _compress_kernel(xk_ref, xv_ref, pe_ref, w1_ref, w2_ref, w2t_ref, kc_ref, vct_ref):
    half = CMP_BLOCK // 2
    acc = [[None, None], [None, None]]
    for l in range(half):
        for j, x_ref in enumerate((xk_ref, xv_ref)):
            xs = x_ref[pl.ds(l, N_CMP_PAD, stride=CMP_STRIDE), :]
            for part in range(2):
                row = part * half + l
                xb = (xs + pe_ref[row:row + 1, j * LANES:(j + 1) * LANES]).astype(BF16)
                d = _dot(xb, w1_ref[j, row])
                acc[j][part] = d if acc[j][part] is None else acc[j][part] + d
    for j in range(2):
        h = acc[j][0] + pltpu.roll(acc[j][1], N_CMP_PAD - 1, 0)
        hs = _silu(h).astype(BF16)
        for g in range(NSA_KV_HEADS):
            hg = hs[:, g * CMP_HIDDEN:(g + 1) * CMP_HIDDEN]
            if j == 0:
                kc_ref[0, g] = _dot(hg, w2_ref[...])
            else:
                vct_ref[0, g] = _dot_nt(w2t_ref[...], hg)


def _compress(kvc, pe_tab, w1_bd, w2, B):
    w2k = w2[0].astype(BF16)
    w2vt = w2[1].T.astype(BF16)
    return pl.pallas_call(
        _compress_kernel,
        grid=(B,),
        in_specs=[pl.BlockSpec((SEQ, LANES), lambda b: (b, 0)),
                  pl.BlockSpec((SEQ, LANES), lambda b: (b, 1)),
                  pl.BlockSpec((CMP_BLOCK, 256), lambda b: (0, 0)),
                  pl.BlockSpec((2, CMP_BLOCK, LANES, 256), lambda b: (0, 0, 0, 0)),
                  pl.BlockSpec((CMP_HIDDEN, NSA_HEAD_DIM), lambda b: (0, 0)),
                  pl.BlockSpec((NSA_HEAD_DIM, CMP_HIDDEN), lambda b: (0, 0))],
        out_specs=[pl.BlockSpec((1, NSA_KV_HEADS, N_CMP_PAD, NSA_HEAD_DIM), lambda b: (b, 0, 0, 0)),
                   pl.BlockSpec((1, NSA_KV_HEADS, NSA_HEAD_DIM, N_CMP_PAD), lambda b: (b, 0, 0, 0))],
        out_shape=[jax.ShapeDtypeStruct((B, NSA_KV_HEADS, N_CMP_PAD, NSA_HEAD_DIM), F32),
                   jax.ShapeDtypeStruct((B, NSA_KV_HEADS, NSA_HEAD_DIM, N_CMP_PAD), F32)],
        compiler_params=_cparams(("parallel",)),
        name="nsa_compress",
    )(kvc, kvc, pe_tab, w1_bd, w2k, w2vt)


def _alibi_slopes_np():
    return np.exp2(-(8.0 / NSA_HEADS) * np.arange(1, NSA_HEADS + 1)).astype(np.float32)


def _overlap_t():
    cs = np.arange(N_CMP) * CMP_STRIDE
    ss = np.arange(N_SEL) * SEL_BLOCK
    ov = np.clip(np.minimum(cs[:, None] + CMP_BLOCK, ss[None, :] + SEL_BLOCK)
                 - np.maximum(cs[:, None], ss[None, :]), 0, None).astype(np.float32) / CMP_BLOCK
    ovt = np.zeros((N_SEL, N_CMP_PAD), np.float32)
    ovt[:, :N_CMP] = ov.T
    return jnp.asarray(ovt)


def _q_alibi_table():
    slopes = _alibi_slopes_np().reshape(NSA_KV_HEADS, NSA_GROUP)
    pos = np.arange(SEQ)
    tab = np.zeros((NSA_KV_HEADS, SEQ, LANES), np.float32)
    for g in range(NSA_KV_HEADS):
        for r in range(NSA_GROUP):
            m = slopes[g, r]
            tab[g, :, r * 32 + 0] = -m * SEL_BLOCK * (pos // SEL_BLOCK)
            tab[g, :, r * 32 + 1] = -m * (pos % SEL_BLOCK)
            tab[g, :, r * 32 + 2] = m * SEL_BLOCK
            tab[g, :, r * 32 + 3] = m
    return jnp.asarray(tab)


def _head_select_mats():
    m = np.zeros((NSA_GROUP, NSA_HEAD_DIM, 256), np.float32)
    for r in range(NSA_GROUP):
        m[r, np.arange(64), r * 64 + np.arange(64)] = 1.0
    return jnp.asarray(m, dtype=BF16)


_GATE_ROWS = 16


def _cmp_bias_table():
    slopes = jnp.asarray(_alibi_slopes_np()).reshape(NSA_KV_HEADS, NSA_GROUP, 1, 1)
    n = jnp.arange(N_CMP_PAD)[:, None]
    t = jnp.arange(SEQ)[None, :]
    dist = t - (n * CMP_STRIDE + (CMP_BLOCK - 1))
    visible = (dist >= 0) & (n < N_CMP)
    return jnp.where(visible, -slopes * dist.astype(F32), NEG_INF)


def _cmp_attn_kernel(q_ref, kc_ref, vct_ref, ovt_ref, qal_ref, hs_ref, bias_ref, gate_ref,
                     ocmp_ref, qaug_ref, *, tq):
    kc = kc_ref[0, 0].astype(BF16)
    vct = vct_ref[0, 0].astype(BF16)
    k_ext = [_dot(kc, hs_ref[r]).astype(BF16) for r in range(NSA_GROUP)]
    ri = lax.broadcasted_iota(jnp.int32, (256, 256), 0)
    ci = lax.broadcasted_iota(jnp.int32, (256, 256), 1)
    eye = jnp.where(ri == ci, 1.0, 0.0).astype(BF16)
    sub8 = lax.broadcasted_iota(jnp.int32, (8, tq), 0)
    j_idx = lax.broadcasted_iota(jnp.int32, (N_SEL, tq), 0)
    t_lane = lax.broadcasted_iota(jnp.int32, (N_SEL, tq), 1)
    n_grp = N_SEL // 8
    for i in range(SEQ // tq):
        cols = slice(i * tq, (i + 1) * tq)
        nv = min(N_CMP_PAD, (i + 1) * tq // CMP_STRIDE)
        q = q_ref[cols, :]
        lhs = jnp.concatenate([k[:nv] for k in k_ext] + [eye], axis=0)
        res = _dot_nt(lhs, q)
        q_t = res[NSA_GROUP * nv:]
        p_sum = None
        o_parts = []
        for r in range(NSA_GROUP):
            s = res[r * nv:(r + 1) * nv] + bias_ref[0, r, 0:nv, cols]
            m = jnp.max(s, axis=0, keepdims=True)
            e = jnp.exp(s - m)
            if i == 0:
                e = jnp.where(bias_ref[0, r, 0:nv, cols] > 0.5 * NEG_INF, e, 0.0)
            l = jnp.sum(e, axis=0, keepdims=True)
            p = e / jnp.where(l > 0.0, l, 1.0)
            p_sum = p if p_sum is None else p_sum + p
            gate = jax.nn.sigmoid(gate_ref[r:r + 1, cols])
            o_parts.append(_dot(vct[:, :nv], p.astype(BF16)) * gate)
        ocmp_ref[0, 0, :, cols] = jnp.concatenate(o_parts, axis=0).astype(ocmp_ref.dtype)

        imp = jnp.dot(ovt_ref[:, 0:nv], p_sum, preferred_element_type=F32,
                      precision=lax.Precision.HIGHEST)
        cur = (i * tq + t_lane) // SEL_BLOCK
        forced = (j_idx == 0) | (j_idx == cur) | (j_idx == cur - 1)
        imp = jnp.where(j_idx > cur, -1.0, imp + jnp.where(forced, FORCE_BONUS, 0.0))
        grp = [imp[8 * a:8 * (a + 1)] for a in range(n_grp)]
        rank = [jnp.zeros((8, tq), F32) for _ in range(n_grp)]
        for k in range(N_SEL):
            row = jnp.broadcast_to(imp[k:k + 1, :], (8, tq))
            for a in range(n_grp):
                ge = jnp.where(row >= grp[a], 1.0, 0.0)
                gt = jnp.where(row > grp[a], 1.0, 0.0)
                if 8 * a > k:
                    cnt = ge
                elif 8 * a + 7 <= k:
                    cnt = gt
                else:
                    cnt = jnp.where(sub8 > k - 8 * a, ge, gt)
                rank[a] = rank[a] + cnt
        sel_bias = jnp.where(jnp.concatenate(rank, axis=0) < float(SEL_TOPN), 0.0, -MASK_BIG)

        qal_t = qal_ref[0, cols, :].T
        for r in range(NSA_GROUP):
            aug_t = jnp.concatenate([q_t[r * 64:(r + 1) * 64], sel_bias,
                                     qal_t[r * 32:(r + 1) * 32, :]], axis=0)
            qaug_ref[0, 0, r, :, cols] = aug_t.astype(BF16)


def _cmp_attn(q, kc, vct, ovt, qal, hs, bias, gates_t, B, tq=256):
    kern = functools.partial(_cmp_attn_kernel, tq=tq)
    return pl.pallas_call(
        kern,
        grid=(NSA_KV_HEADS, B),
        in_specs=[pl.BlockSpec((SEQ, 256), lambda g, b: (b, g)),
                  pl.BlockSpec((1, 1, N_CMP_PAD, NSA_HEAD_DIM), lambda g, b: (b, g, 0, 0)),
                  pl.BlockSpec((1, 1, NSA_HEAD_DIM, N_CMP_PAD), lambda g, b: (b, g, 0, 0)),
                  pl.BlockSpec((N_SEL, N_CMP_PAD), lambda g, b: (0, 0)),
                  pl.BlockSpec((1, SEQ, LANES), lambda g, b: (g, 0, 0)),
                  pl.BlockSpec((NSA_GROUP, NSA_HEAD_DIM, 256), lambda g, b: (0, 0, 0)),
                  pl.BlockSpec((1, NSA_GROUP, N_CMP_PAD, SEQ), lambda g, b: (g, 0, 0, 0)),
                  pl.BlockSpec((_GATE_ROWS, SEQ), lambda g, b: (g, b))],
        out_specs=[pl.BlockSpec((1, 1, 256, SEQ), lambda g, b: (b, g, 0, 0)),
                   pl.BlockSpec((1, 1, NSA_GROUP, LANES, SEQ), lambda g, b: (b, g, 0, 0, 0))],
        out_shape=[jax.ShapeDtypeStruct((B, NSA_KV_HEADS, 256, SEQ), BF16),
                   jax.ShapeDtypeStruct((B, NSA_KV_HEADS, NSA_GROUP, LANES, SEQ), BF16)],
        compiler_params=_cparams(("parallel", "parallel")),
        name="nsa_cmp_attn_select",
    )(q, kc, vct, ovt, qal, hs, bias, gates_t)


_NSA_VT_ROWS = NSA_HEAD_DIM + 16


def _sel_win_kernel(qaug_ref, ks_ref, vs_ref, kw_ref, vw_ref, gate_ref, ocmp_ref,
                    o_ref, vt_ref, s_ref, bias_ref, *, tq):
    nq = SEQ // tq
    n_back = WINDOW // tq
    a_idx = lax.broadcasted_iota(jnp.int32, (tq, tq), 0)
    b_idx = lax.broadcasted_iota(jnp.int32, (tq, tq), 1)
    bias_ref[0] = jnp.where(a_idx <= b_idx, 0.0, -MASK_BIG)
    bias_ref[1] = jnp.where(a_idx > b_idx, 0.0, -MASK_BIG)
    k_refs = (ks_ref, kw_ref)
    for br, v_ref in enumerate((vs_ref, vw_ref)):
        for i in range(nq):
            rows = slice(i * tq, (i + 1) * tq)
            vt = v_ref[rows, :].astype(F32).T
            vt_ref[br, :, rows] = vt[0:_NSA_VT_ROWS].astype(BF16)

    def tiles_of(i, br):
        if br == 0:
            return [(j, None) for j in range(i)] + [(i, 0)]
        tl = [(i - d, 1 if d == n_back else None) for d in range(n_back, 0, -1) if i >= d]
        return tl + [(i, 0)]

    units = [(i, br) for i in range(nq) for br in range(2)]
    n_buf = s_ref.shape[0]

    def stage1(n):
        i, br = units[n]
        q = jnp.concatenate([qaug_ref[0, 0, r, :, i * tq:(i + 1) * tq] for r in range(NSA_GROUP)],
                            axis=1)
        mx = None
        for t_n, (j, bias) in enumerate(tiles_of(i, br)):
            s = _dot(k_refs[br][j * tq:(j + 1) * tq, :], q)
            if bias is not None:
                s = s + jnp.concatenate([bias_ref[bias]] * NSA_GROUP, axis=1)
            s_ref[n % n_buf, t_n * tq:(t_n + 1) * tq, :] = s
            t = s[0:8]
            for c in range(1, tq // 8):
                t = jnp.maximum(t, s[c * 8:(c + 1) * 8])
            mx = t if mx is None else jnp.maximum(mx, t)
        return jnp.max(mx, axis=0, keepdims=True)

    def stage2(n, m):
        i, br = units[n]
        acc = None
        for t_n, (j, _) in enumerate(tiles_of(i, br)):
            p = jnp.exp(s_ref[n % n_buf, t_n * tq:(t_n + 1) * tq, :] - m).astype(BF16)
            pv = _dot(vt_ref[br, :, j * tq:(j + 1) * tq], p)
            acc = pv if acc is None else acc + pv
        cols = slice(i * tq, (i + 1) * tq)
        parts = []
        for r in range(NSA_GROUP):
            hl = slice(r * tq, (r + 1) * tq)
            g_row = 4 * (br + 1) + r
            scale = jax.nn.sigmoid(gate_ref[g_row:g_row + 1, cols]) / acc[NSA_HEAD_DIM:NSA_HEAD_DIM + 1, hl]
            parts.append(acc[0:NSA_HEAD_DIM, hl] * scale)
        return jnp.concatenate(parts, axis=0)

    ms = [stage1(0), stage1(1)]
    total = None
    for n in range(len(units)):
        if n + 2 < len(units):
            ms.append(stage1(n + 2))
        o_t = stage2(n, ms[n])
        i, br = units[n]
        cols = slice(i * tq, (i + 1) * tq)
        if br == 0:
            total = ocmp_ref[0, 0, :, cols].astype(F32) + o_t
        else:
            o_ref[cols, :] = (total + o_t).T.astype(o_ref.dtype)


def _sel_win_attn(qaug_t, kv, gates_t, ocmp_t, B, tq=256):
    kern = functools.partial(_sel_win_kernel, tq=tq)

    def kv_spec(col):
        return pl.BlockSpec((SEQ, LANES), lambda b, g: (b, col + g))

    return pl.pallas_call(
        kern,
        grid=(B, NSA_KV_HEADS),
        in_specs=[pl.BlockSpec((1, 1, NSA_GROUP, LANES, SEQ), lambda b, g: (b, g, 0, 0, 0)),
                  kv_spec(0), kv_spec(2), kv_spec(4), kv_spec(6),
                  pl.BlockSpec((_GATE_ROWS, SEQ), lambda b, g: (g, b)),
                  pl.BlockSpec((1, 1, 256, SEQ), lambda b, g: (b, g, 0, 0))],
        out_specs=pl.BlockSpec((SEQ, 256), lambda b, g: (b, g)),
        out_shape=jax.ShapeDtypeStruct((B * SEQ, 512), BF16),
        scratch_shapes=[pltpu.VMEM((2, _NSA_VT_ROWS, SEQ), BF16),
                        pltpu.VMEM((3, SEQ, NSA_GROUP * tq), F32),
                        pltpu.VMEM((2, tq, tq), F32)],
        compiler_params=_cparams(("parallel", "parallel")),
        name="nsa_sel_win_attn",
    )(qaug_t, kv, kv, kv, kv, gates_t, ocmp_t)


_GLA_ROWS = 512


def _gla_kernel(qk_ref, v_ref, ga_ref, gz_ref, wg_ref, bg_ref, ng_ref, o_ref,
                b_ref, qd_ref, kd_ref, st_ref):
    C = GLA_CHUNK
    KW = GLA_HEADS * GLA_DK
    wg = wg_ref[...]
    bg = bg_ref[...]
    norm_g = ng_ref[...]

    rin = lax.broadcasted_iota(jnp.int32, (_GLA_ROWS, KW), 0) % C
    lane = lax.broadcasted_iota(jnp.int32, (_GLA_ROWS, KW), 1)
    for blk in range(SEQ // _GLA_ROWS):
        rows = slice(blk * _GLA_ROWS, (blk + 1) * _GLA_ROWS)
        x = _dot(ga_ref[rows, :].astype(BF16), wg) + bg
        b = -(jnp.maximum(-x, 0.0) + jnp.log1p(jnp.exp(-jnp.abs(x)))) / GLA_TAU
        sh = 1
        while sh < C:
            b = b + jnp.where(rin >= sh, pltpu.roll(b, sh, 0), 0.0)
            sh *= 2
        b_ref[rows, :] = b
        q_d = qk_ref[rows, 0:KW] * (GLA_DK ** -0.5) * jnp.exp(b)
        for h in range(GLA_HEADS):
            in_head = (lane >= h * GLA_DK) & (lane < (h + 1) * GLA_DK)
            qd_ref[h, rows, :] = jnp.where(in_head, q_d, 0.0).astype(BF16)
        kd_ref[rows, :] = (qk_ref[rows, KW:2 * KW] * jnp.exp(-b)).astype(BF16)

    r_idx = lax.broadcasted_iota(jnp.int32, (C, C), 0)
    c_idx = lax.broadcasted_iota(jnp.int32, (C, C), 1)
    causal = r_idx >= c_idx
    st_ref[...] = jnp.zeros(st_ref.shape, F32)
    for n in range(SEQ // C):
        rows = slice(n * C, (n + 1) * C)
        b = b_ref[rows, :]
        b_last = b_ref[n * C + C - 1:(n + 1) * C, :]
        k_u = (qk_ref[rows, KW:2 * KW] * jnp.exp(b_last - b)).astype(BF16)
        k_d = kd_ref[rows, :]
        v = v_ref[rows, :].astype(BF16)
        st = st_ref[...]
        st_b = st.astype(BF16)
        for h in range(GLA_HEADS):
            vs = slice(h * GLA_DV, (h + 1) * GLA_DV)
            q_h = qd_ref[h, rows, :]
            a = jnp.where(causal, _dot_nt(q_h, k_d), 0.0)
            o = _dot(a.astype(BF16), v[:, vs]) + _dot_nt(q_h, st_b[vs, :])
            y = o * lax.rsqrt(jnp.mean(o * o, axis=-1, keepdims=True) + NORM_EPS) * norm_g
            y = y * _silu(gz_ref[rows, vs])
            o_ref[rows, vs] = y.astype(o_ref.dtype)
        upd = lax.dot_general(v, k_u, (((0,), (0,)), ((), ())), preferred_element_type=F32)
        st_ref[...] = st * jnp.exp(b_last) + upd


def _gla(gqk, gv, ga, gz, wg, bg, norm_g, B):
    row = lambda b: (b, 0)
    const = lambda b: (0, 0)
    return pl.pallas_call(
        _gla_kernel,
        grid=(B,),
        in_specs=[pl.BlockSpec((SEQ, 512), row), pl.BlockSpec((SEQ, 512), row),
                  pl.BlockSpec((SEQ, LANES), row), pl.BlockSpec((SEQ, 512), row),
                  pl.BlockSpec((LANES, 256), const), pl.BlockSpec((1, 256), const),
                  pl.BlockSpec((1, GLA_DV), const)],
        out_specs=pl.BlockSpec((SEQ, 512), row),
        out_shape=jax.ShapeDtypeStruct((B * SEQ, 512), BF16),
        scratch_shapes=[pltpu.VMEM((SEQ, GLA_HEADS * GLA_DK), F32),
                        pltpu.VMEM((GLA_HEADS, SEQ, GLA_HEADS * GLA_DK), BF16),
                        pltpu.VMEM((SEQ, GLA_HEADS * GLA_DK), BF16),
                        pltpu.VMEM((GLA_HEADS * GLA_DV, GLA_HEADS * GLA_DK), F32)],
        compiler_params=_cparams(("parallel",)),
        name="gla_chunked",
    )(gqk, gv, ga, gz, wg, bg, norm_g)


def _deepnorm_ln(x, y, g, b):
    r = DEEPNORM_ALPHA * x + y
    mu = jnp.mean(r, axis=-1, keepdims=True)
    d = r - mu
    var = jnp.mean(d * d, axis=-1, keepdims=True)
    return d * lax.rsqrt(var + NORM_EPS) * g + b


def _even_out_kernel(onsa_ref, nz_ref, ogla_ref, x_ref, w_ref, g_ref, b_ref, o_ref):
    o_nsa = (onsa_ref[...].astype(F32) * _silu(nz_ref[...])).astype(BF16)
    y = _dot(o_nsa, w_ref[0:512, :]) + _dot(ogla_ref[...], w_ref[512:1024, :])
    o_ref[...] = _deepnorm_ln(x_ref[...], y, g_ref[...], b_ref[...])


def _even_out(onsa, nz, ogla, x2, w_bf, ln_g, ln_b, tm=256):
    T = x2.shape[0]
    row = lambda i: (i, 0)
    const = lambda i: (0, 0)
    return pl.pallas_call(
        _even_out_kernel,
        grid=(T // tm,),
        in_specs=[pl.BlockSpec((tm, 512), row), pl.BlockSpec((tm, 512), row),
                  pl.BlockSpec((tm, 512), row), pl.BlockSpec((tm, D_MODEL), row),
                  pl.BlockSpec((1024, D_MODEL), const),
                  pl.BlockSpec((1, D_MODEL), const), pl.BlockSpec((1, D_MODEL), const)],
        out_specs=pl.BlockSpec((tm, D_MODEL), row),
        out_shape=jax.ShapeDtypeStruct((T, D_MODEL), F32),
        compiler_params=_cparams(("parallel",)),
        name="even_out_proj_ln",
    )(onsa, nz, ogla, x2, w_bf, ln_g, ln_b)


def _odd_out_kernel(o_ref_in, z_ref, x_ref, w_ref, g_ref, b_ref, o_ref):
    o = (o_ref_in[...].astype(F32) * _silu(z_ref[...])).astype(BF16)
    y = _dot(o, w_ref[...])
    o_ref[...] = _deepnorm_ln(x_ref[...], y, g_ref[...], b_ref[...])


def _odd_out(o, z, x2, w_bf, ln_g, ln_b, tm=256):
    T = x2.shape[0]
    row = lambda i: (i, 0)
    const = lambda i: (0, 0)
    return pl.pallas_call(
        _odd_out_kernel,
        grid=(T // tm,),
        in_specs=[pl.BlockSpec((tm, D_MODEL), row), pl.BlockSpec((tm, D_MODEL), row),
                  pl.BlockSpec((tm, D_MODEL), row), pl.BlockSpec((D_MODEL, D_MODEL), const),
                  pl.BlockSpec((1, D_MODEL), const), pl.BlockSpec((1, D_MODEL), const)],
        out_specs=pl.BlockSpec((tm, D_MODEL), row),
        out_shape=jax.ShapeDtypeStruct((T, D_MODEL), F32),
        compiler_params=_cparams(("parallel",)),
        name="odd_out_proj_ln",
    )(o, z, x2, w_bf, ln_g, ln_b)


_HEAD_W = LANES
_QK_W = MLA_HEADS * _HEAD_W
_ROPE_HALF = MLA_ROPE // 2


def _odd_weights(w_in, w_uq, w_ukv):
    o = _O_OFF
    z_nope = jnp.zeros((D_MODEL, MLA_NOPE), F32)
    z_tail = jnp.zeros((D_MODEL, _HEAD_W - MLA_NOPE - MLA_ROPE), F32)
    w1 = jnp.concatenate([w_in[:, o[0]:o[2]], z_nope, w_in[:, o[2]:o[3]], z_tail,
                          w_in[:, o[3]:o[4]]], axis=1).astype(BF16)
    uq = w_uq.reshape(MLA_Q_LORA, MLA_HEADS, MLA_NOPE + MLA_ROPE)
    zq = jnp.zeros((MLA_Q_LORA, MLA_HEADS, _HEAD_W - MLA_NOPE - MLA_ROPE), F32)
    wq = jnp.concatenate([uq, zq], axis=-1).reshape(MLA_Q_LORA, _QK_W).astype(BF16)
    ukv = w_ukv.reshape(MLA_KV_LORA, MLA_HEADS, MLA_NOPE + MLA_V)
    wk = ukv[..., :MLA_NOPE].reshape(MLA_KV_LORA, MLA_HEADS * MLA_NOPE).astype(BF16)
    wv = ukv[..., MLA_NOPE:].reshape(MLA_KV_LORA, MLA_HEADS * MLA_V).astype(BF16)
    return w1, wq, wk, wv


def _rope_tables():
    freqs = jnp.exp(-math.log(ROPE_THETA) * jnp.arange(_ROPE_HALF, dtype=F32) * 2.0 / MLA_ROPE)
    ang = jnp.arange(SEQ, dtype=F32)[:, None] * freqs[None, :]
    cos, sin = jnp.cos(ang), jnp.sin(ang)
    z_half = jnp.zeros((SEQ, _ROPE_HALF), F32)
    z_tail = jnp.zeros((SEQ, _HEAD_W - MLA_NOPE - MLA_ROPE), F32)
    z_nope = jnp.zeros((SEQ, MLA_NOPE), F32)

    def tables(scale, nope_gain):
        nope = jnp.full((SEQ, MLA_NOPE), nope_gain, F32)
        c = jnp.concatenate([nope, cos * scale, cos * scale, z_tail], axis=1)
        s1 = jnp.concatenate([z_nope, -sin * scale, z_half, z_tail], axis=1)
        s2 = jnp.concatenate([z_nope, z_half, sin * scale, z_tail], axis=1)
        return c, s1, s2

    q_scale = (MLA_NOPE + MLA_ROPE) ** -0.5 * math.log2(math.e)
    return tables(q_scale, q_scale), tables(1.0, 0.0)


def _rope_block(x, c, s1, s2):
    return x * c + pltpu.roll(x, LANES - _ROPE_HALF, 1) * s1 + pltpu.roll(x, _ROPE_HALF, 1) * s2


def _rms(x, g):
    return x * lax.rsqrt(jnp.mean(x * x, axis=-1, keepdims=True) + NORM_EPS) * g


def _odd_in_kernel(x_ref, w1_ref, wq_ref, wk_ref, wv_ref, qn_ref, kn_ref,
                   qc_ref, qs1_ref, qs2_ref, kc_ref, ks1_ref, ks2_ref,
                   q_ref, k_ref, kr_ref, v_ref, z_ref):
    xb = x_ref[...].astype(BF16)
    c_q = _dot(xb, w1_ref[:, 0:MLA_Q_LORA])
    c_kv = _dot(xb, w1_ref[:, MLA_Q_LORA:640])
    kr = _dot(xb, w1_ref[:, 640:768])
    z_ref[...] = _dot(xb, w1_ref[:, 768:1792])
    cqn = _rms(c_q, qn_ref[...]).astype(BF16)
    ckvn = _rms(c_kv, kn_ref[...]).astype(BF16)
    qa = _dot(cqn, wq_ref[...])
    qc, qs1, qs2 = qc_ref[...], qs1_ref[...], qs2_ref[...]
    for h in range(MLA_HEADS):
        sl = slice(h * _HEAD_W, (h + 1) * _HEAD_W)
        q_ref[:, sl] = _rope_block(qa[:, sl], qc, qs1, qs2).astype(BF16)
    kr_ref[...] = _rope_block(kr, kc_ref[...], ks1_ref[...], ks2_ref[...]).astype(BF16)
    k_ref[...] = _dot(ckvn, wk_ref[...]).astype(BF16)
    v_ref[...] = _dot(ckvn, wv_ref[...]).astype(BF16)


def _odd_in_proj(x2, w1, wq, wk, wv, qn, kn, q_tabs, k_tabs, tm=256):
    T = x2.shape[0]
    s_tiles = SEQ // tm
    row = lambda i: (i, 0)
    const = lambda i: (0, 0)
    pos = lambda i: (i % s_tiles, 0)

    def full(a):
        return pl.BlockSpec(a.shape, const)

    tab = pl.BlockSpec((tm, LANES), pos)
    n_kv = MLA_HEADS * MLA_NOPE
    return pl.pallas_call(
        _odd_in_kernel,
        grid=(T // tm,),
        in_specs=[pl.BlockSpec((tm, D_MODEL), row), full(w1), full(wq), full(wk), full(wv),
                  full(qn), full(kn), tab, tab, tab, tab, tab, tab],
        out_specs=[pl.BlockSpec((tm, _QK_W), row), pl.BlockSpec((tm, n_kv), row),
                   pl.BlockSpec((tm, LANES), row), pl.BlockSpec((tm, n_kv), row),
                   pl.BlockSpec((tm, 1024), row)],
        out_shape=[jax.ShapeDtypeStruct((T, _QK_W), BF16), jax.ShapeDtypeStruct((T, n_kv), BF16),
                   jax.ShapeDtypeStruct((T, LANES), BF16), jax.ShapeDtypeStruct((T, n_kv), BF16),
                   jax.ShapeDtypeStruct((T, 1024), F32)],
        compiler_params=_cparams(("parallel",)),
        name="odd_in_proj",
    )(x2, w1, wq, wk, wv, qn, kn, *q_tabs, *k_tabs)


_VT_ROWS = MLA_V + 16


def _mla_kernel(q_ref, kn_ref, kr_ref, v_ref, o_ref, kx_ref, qt_ref, vt_ref, s_ref, bias_ref, *, tq):
    nq = SEQ // tq
    a_idx = lax.broadcasted_iota(jnp.int32, (tq, tq), 0)
    b_idx = lax.broadcasted_iota(jnp.int32, (tq, tq), 1)
    bias_ref[...] = jnp.where(a_idx <= b_idx, 0.0, NEG_INF)
    lane = lax.broadcasted_iota(jnp.int32, (tq, LANES), 1)
    ones_rows = jnp.where(lax.broadcasted_iota(jnp.int32, (_VT_ROWS - MLA_V, tq), 0) == 0, 1.0, 0.0)

    for i in range(nq):
        rows = slice(i * tq, (i + 1) * tq)
        knp = kn_ref[rows, :].astype(F32)
        krb = kr_ref[rows, :].astype(F32)
        vt = v_ref[rows, :].astype(F32).T
        for h in range(2):
            kn_h = knp if h == 0 else pltpu.roll(knp, MLA_NOPE, 1)
            kx_ref[h, rows, :] = jnp.where(lane < MLA_NOPE, kn_h, krb).astype(BF16)
            qt_ref[h, :, rows] = q_ref[rows, h * _HEAD_W:(h + 1) * _HEAD_W].astype(F32).T.astype(BF16)
            vt_ref[h, :, rows] = jnp.concatenate(
                [vt[h * MLA_V:(h + 1) * MLA_V], ones_rows], axis=0).astype(BF16)

    units = [(i, h) for i in range(nq) for h in range(2)]
    n_buf = s_ref.shape[0]

    def stage1(n):
        i, h = units[n]
        qt = qt_ref[h, :, i * tq:(i + 1) * tq]
        mx = None
        for j in range(i + 1):
            cols = slice(j * tq, (j + 1) * tq)
            s = _dot(kx_ref[h, cols, :], qt)
            if j == i:
                s = s + bias_ref[...]
            s_ref[n % n_buf, cols, :] = s
            t = s[0:8]
            for c in range(1, tq // 8):
                t = jnp.maximum(t, s[c * 8:(c + 1) * 8])
            mx = t if mx is None else jnp.maximum(mx, t)
        return jnp.max(mx, axis=0, keepdims=True)

    def stage2(n, m):
        i, h = units[n]
        acc = None
        for j in range(i + 1):
            cols = slice(j * tq, (j + 1) * tq)
            p = jnp.exp2(s_ref[n % n_buf, cols, :] - m).astype(BF16)
            pv = _dot(vt_ref[h, :, cols], p)
            acc = pv if acc is None else acc + pv
        return acc[0:MLA_V] / acc[MLA_V:MLA_V + 1]

    ms = [stage1(0), stage1(1)]
    outs = []
    for n in range(len(units)):
        if n + 2 < len(units):
            ms.append(stage1(n + 2))
        outs.append(stage2(n, ms[n]))
        if len(outs) == 2:
            i = units[n][0]
            o_ref[i * tq:(i + 1) * tq, :] = jnp.concatenate(outs, axis=0).T.astype(o_ref.dtype)
            outs = []


def _mla_attn(q, kn, kr, v, B, tq=256):
    kern = functools.partial(_mla_kernel, tq=tq)
    pair = pl.BlockSpec((SEQ, LANES), lambda b, h: (b, h))
    return pl.pallas_call(
        kern,
        grid=(B, MLA_HEADS // 2),
        in_specs=[pl.BlockSpec((SEQ, 2 * _HEAD_W), lambda b, h: (b, h)), pair,
                  pl.BlockSpec((SEQ, LANES), lambda b, h: (b, 0)), pair],
        out_specs=pair,
        out_shape=jax.ShapeDtypeStruct((B * SEQ, MLA_HEADS * MLA_V), BF16),
        scratch_shapes=[pltpu.VMEM((2, SEQ, _HEAD_W), BF16), pltpu.VMEM((2, _HEAD_W, SEQ), BF16),
                        pltpu.VMEM((2, _VT_ROWS, SEQ), BF16), pltpu.VMEM((4, SEQ, tq), F32),
                        pltpu.VMEM((tq, tq), F32)],
        compiler_params=_cparams(("parallel", "parallel")),
        name="mla_attn",
    )(q, kn, kr, v)


def _even_layer(x2, B, w_in, cmp_pe, cmp_w1, cmp_w2, gla_w_gate, gla_b_gate, gla_norm,
                w_out, ln_g, ln_b):
    T = x2.shape[0]
    q, kvc, kv, ng, nz, gqk, gv, ga, gz = _even_in_proj(x2, _even_in_weight(w_in), _nsa_kv_table())

    pe_tab, w1_bd = _compress_weights(cmp_pe, cmp_w1)
    kc, vct = _compress(kvc, pe_tab, w1_bd, cmp_w2, B)

    ocmp_t, qaug_t = _cmp_attn(q, kc, vct, _overlap_t(), _q_alibi_table(), _head_select_mats(),
                               _cmp_bias_table(), ng, B)
    onsa = _sel_win_attn(qaug_t, kv, ng, ocmp_t, B)

    wg = jnp.pad(gla_w_gate, ((0, LANES - GLA_GATE_RANK), (0, 0))).astype(BF16)
    ogla = _gla(gqk, gv, ga, gz, wg, gla_b_gate.reshape(1, -1), gla_norm.reshape(1, -1), B)

    return _even_out(onsa, nz, ogla, x2, w_out.astype(BF16), ln_g.reshape(1, -1), ln_b.reshape(1, -1))


def _odd_layer(x2, B, w_in, q_norm, w_uq, kv_norm, w_ukv, w_out, ln_g, ln_b):
    w1, wq, wk, wv = _odd_weights(w_in, w_uq, w_ukv)
    q_tabs, k_tabs = _rope_tables()
    q, kn, kr, v, z = _odd_in_proj(x2, w1, wq, wk, wv, q_norm.reshape(1, -1),
                                   kv_norm.reshape(1, -1), q_tabs, k_tabs)
    o = _mla_attn(q, kn, kr, v, B)
    return _odd_out(o, z, x2, w_out.astype(BF16), ln_g.reshape(1, -1), ln_b.reshape(1, -1))


def kernel(x, e_w_in, e_cmp_pe, e_cmp_w1, e_cmp_w2, e_gla_w_gate, e_gla_b_gate, e_gla_norm,
           e_w_out, e_ln_g, e_ln_b, o_w_in, o_q_norm, o_w_uq, o_kv_norm, o_w_ukv, o_w_out,
           o_ln_g, o_ln_b):
    B, S, D = x.shape
    x2 = x.reshape(B * S, D)
    for layer in range(DEPTH):
        i = layer // 2
        if layer % 2 == 0:
            x2 = _even_layer(x2, B, e_w_in[i], e_cmp_pe[i], e_cmp_w1[i], e_cmp_w2[i],
                             e_gla_w_gate[i], e_gla_b_gate[i], e_gla_norm[i], e_w_out[i],
                             e_ln_g[i], e_ln_b[i])
        else:
            x2 = _odd_layer(x2, B, o_w_in[i], o_q_norm[i], o_w_uq[i], o_kv_norm[i], o_w_ukv[i],
                            o_w_out[i], o_ln_g[i], o_ln_b[i])
    return x2.reshape(B, S, D)
```

```python
import functools
import math

import jax
import jax.numpy as jnp
import numpy as np
from jax import lax
from jax.experimental import pallas as pl
from jax.experimental.pallas import tpu as pltpu

F32 = jnp.float32
BF16 = jnp.bfloat16

D_MODEL = 1024
SEQ = 2048
DEPTH = 2
NSA_HEADS = 8
NSA_KV_HEADS = 2
NSA_GROUP = 4
NSA_HEAD_DIM = 64
CMP_BLOCK = 32
CMP_STRIDE = 16
CMP_HIDDEN = 128
N_CMP = (SEQ - CMP_BLOCK) // CMP_STRIDE + 1
N_CMP_PAD = 128
SEL_BLOCK = 64
N_SEL = SEQ // SEL_BLOCK
SEL_TOPN = 8
WINDOW = 512
FORCE_BONUS = 1000.0
GLA_HEADS = 4
GLA_DK = 64
GLA_DV = 128
GLA_GATE_RANK = 16
GLA_TAU = 16.0
GLA_CHUNK = 64
MLA_HEADS = 16
MLA_NOPE = 64
MLA_ROPE = 32
MLA_V = 64
MLA_Q_LORA = 384
MLA_KV_LORA = 256
ROPE_THETA = 10000.0
NORM_EPS = 1e-5
NEG_INF = -1e30
DEEPNORM_ALPHA = (2 * DEPTH) ** 0.25

LANES = 128
MASK_BIG = 2.0 ** 100
VMEM_LIMIT = 56 * 1024 * 1024

_E_OFF = np.cumsum([0, 512, 768, 24, 512, 256, 256, 512, 16, 512])
_O_OFF = np.cumsum([0, MLA_Q_LORA, MLA_KV_LORA, MLA_ROPE, MLA_HEADS * MLA_V])


def _cparams(sem):
    return pltpu.CompilerParams(dimension_semantics=sem, vmem_limit_bytes=VMEM_LIMIT)


def _silu(x):
    return x * jax.nn.sigmoid(x)


def _dot(a, b):
    return jnp.dot(a, b, preferred_element_type=F32)


def _dot_nt(a, b):
    return lax.dot_general(a, b, (((1,), (1,)), ((), ())), preferred_element_type=F32)


_A_SEGS = (("q", 512, BF16), ("kvc", 256, F32), ("kv", 1024, BF16), ("ng", 128, F32),
           ("nz", 512, BF16), ("gqk", 512, F32), ("gv", 512, BF16), ("ga", 128, F32),
           ("gz", 512, BF16))
_A_SILU = ("nz", "gz")
_A_OFF = np.cumsum([0] + [s[1] for s in _A_SEGS])


def _even_in_weight(w):
    o = _E_OFF
    z64 = jnp.zeros((D_MODEL, 64), F32)
    nkv = w[:, o[1]:o[2]]

    def kvcol(br, j, g):
        c = ((br * 2 + j) * NSA_KV_HEADS + g) * 64
        return nkv[:, c:c + 64]

    kv_blocks = []
    for br in (1, 2):
        for j in (0, 1):
            for g in (0, 1):
                kv_blocks += [kvcol(br, j, g), z64]
    gates = w[:, o[2]:o[3]].reshape(D_MODEL, 3, NSA_KV_HEADS, NSA_GROUP).transpose(0, 2, 1, 3)
    gates = jnp.pad(gates.reshape(D_MODEL, NSA_KV_HEADS, 12), ((0, 0), (0, 0), (0, 4)))
    ng = jnp.pad(gates.reshape(D_MODEL, 32), ((0, 0), (0, LANES - 32)))
    ga = jnp.pad(w[:, o[7]:o[8]], ((0, 0), (0, LANES - GLA_GATE_RANK)))
    cols = [w[:, o[0]:o[1]] * (NSA_HEAD_DIM ** -0.5), nkv[:, :256]] + kv_blocks + [
        ng, w[:, o[3]:o[4]], w[:, o[4]:o[6]], w[:, o[6]:o[7]], ga, w[:, o[8]:o[9]]]
    return jnp.concatenate(cols, axis=1).astype(BF16)


def _nsa_kv_table():
    pos = np.arange(SEQ)
    k_al = np.zeros((SEQ, LANES), np.float32)
    k_al[:, 96] = 1.0
    k_al[:, 97] = 1.0
    k_al[:, 98] = pos // SEL_BLOCK
    k_al[:, 99] = pos % SEL_BLOCK
    k_sel = k_al.copy()
    k_sel[pos, 64 + pos // SEL_BLOCK] = 1.0
    v_one = np.zeros((SEQ, LANES), np.float32)
    v_one[:, 64] = 1.0
    tab = np.concatenate([k_sel, k_sel, v_one, v_one, k_al, k_al, v_one, v_one], axis=1)
    return jnp.asarray(tab)


def _even_in_kernel(x_ref, w_ref, wg_ref, tab_ref, *out_refs):
    xb = x_ref[...].astype(BF16)
    for (name, width, dt), off, o_ref in zip(_A_SEGS, _A_OFF[:-1], out_refs):
        if name == "ng":
            o_ref[...] = _dot_nt(wg_ref[...], xb)
            continue
        r = _dot(xb, w_ref[:, off:off + width])
        if name == "kv":
            r = r + tab_ref[...]
        if name in _A_SILU:
            r = _silu(r)
        o_ref[...] = r.astype(dt)


def _even_in_proj(x2, w_bf, tab, tm=256):
    T = x2.shape[0]
    n_w = w_bf.shape[1]
    s_tiles = SEQ // tm
    ng_off = _A_OFF[[n for n, _, _ in _A_SEGS].index("ng")]
    wg_t = w_bf[:, ng_off:ng_off + LANES].T
    out_shape = [jax.ShapeDtypeStruct((wd, T) if name == "ng" else (T, wd), dt)
                 for name, wd, dt in _A_SEGS]
    out_specs = [pl.BlockSpec((wd, tm), lambda i: (0, i)) if name == "ng"
                 else pl.BlockSpec((tm, wd), lambda i: (i, 0)) for name, wd, _ in _A_SEGS]
    return pl.pallas_call(
        _even_in_kernel,
        grid=(T // tm,),
        in_specs=[pl.BlockSpec((tm, D_MODEL), lambda i: (i, 0)),
                  pl.BlockSpec((D_MODEL, n_w), lambda i: (0, 0)),
                  pl.BlockSpec((LANES, D_MODEL), lambda i: (0, 0)),
                  pl.BlockSpec((tm, 1024), lambda i: (i % s_tiles, 0))],
        out_specs=out_specs,
        out_shape=out_shape,
        compiler_params=_cparams(("parallel",)),
        name="even_in_proj",
    )(x2, w_bf, wg_t, tab)


def _compress_weights(cmp_pe, cmp_w1):
    pe_tab = jnp.concatenate([cmp_pe[0], cmp_pe[0], cmp_pe[1], cmp_pe[1]], axis=1)
    w = cmp_w1.reshape(2, CMP_BLOCK, NSA_HEAD_DIM, CMP_HIDDEN)
    z = jnp.zeros_like(w)
    w_bd = jnp.concatenate([jnp.concatenate([w, z], axis=3), jnp.concatenate([z, w], axis=3)], axis=2)
    return pe_tab, w_bd.astype(BF16)


def _compress_kernel(xk_ref, xv_ref, pe_ref, w1_ref, w2_ref, w2t_ref, kc_ref, vct_ref):
    half = CMP_BLOCK // 2
    acc = [[None, None], [None, None]]
    for l in range(half):
        for j, x_ref in enumerate((xk_ref, xv_ref)):
            xs = x_ref[pl.ds(l, N_CMP_PAD, stride=CMP_STRIDE), :]
            for part in range(2):
                row = part * half + l
                xb = (xs + pe_ref[row:row + 1, j * LANES:(j + 1) * LANES]).astype(BF16)
                d = _dot(xb, w1_ref[j, row])
                acc[j][part] = d if acc[j][part] is None else acc[j][part] + d
    for j in range(2):
        h = acc[j][0] + pltpu.roll(acc[j][1], N_CMP_PAD - 1, 0)
        hs = _silu(h).astype(BF16)
        for g in range(NSA_KV_HEADS):
            hg = hs[:, g * CMP_HIDDEN:(g + 1) * CMP_HIDDEN]
            if j == 0:
                kc_ref[0, g] = _dot(hg, w2_ref[...])
            else:
                vct_ref[0, g] = _dot_nt(w2t_ref[...], hg)


def _compress(kvc, pe_tab, w1_bd, w2, B):
    w2k = w2[0].astype(BF16)
    w2vt = w2[1].T.astype(BF16)
    return pl.pallas_call(
        _compress_kernel,
        grid=(B,),
        in_specs=[pl.BlockSpec((SEQ, LANES), lambda b: (b, 0)),
                  pl.BlockSpec((SEQ, LANES), lambda b: (b, 1)),
                  pl.BlockSpec((CMP_BLOCK, 256), lambda b: (0, 0)),
                  pl.BlockSpec((2, CMP_BLOCK, LANES, 256), lambda b: (0, 0, 0, 0)),
                  pl.BlockSpec((CMP_HIDDEN, NSA_HEAD_DIM), lambda b: (0, 0)),
                  pl.BlockSpec((NSA_HEAD_DIM, CMP_HIDDEN), lambda b: (0, 0))],
        out_specs=[pl.BlockSpec((1, NSA_KV_HEADS, N_CMP_PAD, NSA_HEAD_DIM), lambda b: (b, 0, 0, 0)),
                   pl.BlockSpec((1, NSA_KV_HEADS, NSA_HEAD_DIM, N_CMP_PAD), lambda b: (b, 0, 0, 0))],
        out_shape=[jax.ShapeDtypeStruct((B, NSA_KV_HEADS, N_CMP_PAD, NSA_HEAD_DIM), F32),
                   jax.ShapeDtypeStruct((B, NSA_KV_HEADS, NSA_HEAD_DIM, N_CMP_PAD), F32)],
        compiler_params=_cparams(("parallel",)),
        name="nsa_compress",
    )(kvc, kvc, pe_tab, w1_bd, w2k, w2vt)


def _alibi_slopes_np():
    return np.exp2(-(8.0 / NSA_HEADS) * np.arange(1, NSA_HEADS + 1)).astype(np.float32)


def _overlap_t():
    cs = np.arange(N_CMP) * CMP_STRIDE
    ss = np.arange(N_SEL) * SEL_BLOCK
    ov = np.clip(np.minimum(cs[:, None] + CMP_BLOCK, ss[None, :] + SEL_BLOCK)
                 - np.maximum(cs[:, None], ss[None, :]), 0, None).astype(np.float32) / CMP_BLOCK
    ovt = np.zeros((N_SEL, N_CMP_PAD), np.float32)
    ovt[:, :N_CMP] = ov.T
    return jnp.asarray(ovt)


def _q_alibi_table():
    slopes = _alibi_slopes_np().reshape(NSA_KV_HEADS, NSA_GROUP)
    pos = np.arange(SEQ)
    tab = np.zeros((NSA_KV_HEADS, SEQ, LANES), np.float32)
    for g in range(NSA_KV_HEADS):
        for r in range(NSA_GROUP):
            m = slopes[g, r]
            tab[g, :, r * 32 + 0] = -m * SEL_BLOCK * (pos // SEL_BLOCK)
            tab[g, :, r * 32 + 1] = -m * (pos % SEL_BLOCK)
            tab[g, :, r * 32 + 2] = m * SEL_BLOCK
            tab[g, :, r * 32 + 3] = m
    return jnp.asarray(tab)


def _head_select_mats():
    m = np.zeros((NSA_GROUP, NSA_HEAD_DIM, 256), np.float32)
    for r in range(NSA_GROUP):
        m[r, np.arange(64), r * 64 + np.arange(64)] = 1.0
    return jnp.asarray(m, dtype=BF16)


_GATE_ROWS = 16


def _cmp_bias_table():
    slopes = jnp.asarray(_alibi_slopes_np()).reshape(NSA_KV_HEADS, NSA_GROUP, 1, 1)
    n = jnp.arange(N_CMP_PAD)[:, None]
    t = jnp.arange(SEQ)[None, :]
    dist = t - (n * CMP_STRIDE + (CMP_BLOCK - 1))
    visible = (dist >= 0) & (n < N_CMP)
    return jnp.where(visible, -slopes * dist.astype(F32), NEG_INF)


def _cmp_attn_kernel(q_ref, kc_ref, vct_ref, ovt_ref, qal_ref, hs_ref, bias_ref, gate_ref,
                     ocmp_ref, qaug_ref, *, tq):
    kc = kc_ref[0, 0].astype(BF16)
    vct = vct_ref[0, 0].astype(BF16)
    k_ext = [_dot(kc, hs_ref[r]).astype(BF16) for r in range(NSA_GROUP)]
    ri = lax.broadcasted_iota(jnp.int32, (256, 256), 0)
    ci = lax.broadcasted_iota(jnp.int32, (256, 256), 1)
    eye = jnp.where(ri == ci, 1.0, 0.0).astype(BF16)
    sub8 = lax.broadcasted_iota(jnp.int32, (8, tq), 0)
    j_idx = lax.broadcasted_iota(jnp.int32, (N_SEL, tq), 0)
    t_lane = lax.broadcasted_iota(jnp.int32, (N_SEL, tq), 1)
    n_grp = N_SEL // 8
    for i in range(SEQ // tq):
        cols = slice(i * tq, (i + 1) * tq)
        nv = min(N_CMP_PAD, (i + 1) * tq // CMP_STRIDE)
        q = q_ref[cols, :]
        lhs = jnp.concatenate([k[:nv] for k in k_ext] + [eye], axis=0)
        res = _dot_nt(lhs, q)
        q_t = res[NSA_GROUP * nv:]
        p_sum = None
        o_parts = []
        for r in range(NSA_GROUP):
            s = res[r * nv:(r + 1) * nv] + bias_ref[0, r, 0:nv, cols]
            m = jnp.max(s, axis=0, keepdims=True)
            e = jnp.exp(s - m)
            if i == 0:
                e = jnp.where(bias_ref[0, r, 0:nv, cols] > 0.5 * NEG_INF, e, 0.0)
            l = jnp.sum(e, axis=0, keepdims=True)
            p = e / jnp.where(l > 0.0, l, 1.0)
            p_sum = p if p_sum is None else p_sum + p
            gate = jax.nn.sigmoid(gate_ref[r:r + 1, cols])
            o_parts.append(_dot(vct[:, :nv], p.astype(BF16)) * gate)
        ocmp_ref[0, 0, :, cols] = jnp.concatenate(o_parts, axis=0).astype(ocmp_ref.dtype)

        imp = jnp.dot(ovt_ref[:, 0:nv], p_sum, preferred_element_type=F32,
                      precision=lax.Precision.HIGHEST)
        cur = (i * tq + t_lane) // SEL_BLOCK
        forced = (j_idx == 0) | (j_idx == cur) | (j_idx == cur - 1)
        imp = jnp.where(j_idx > cur, -1.0, imp + jnp.where(forced, FORCE_BONUS, 0.0))
        grp = [imp[8 * a:8 * (a + 1)] for a in range(n_grp)]
        rank = [jnp.zeros((8, tq), F32) for _ in range(n_grp)]
        for k in range(N_SEL):
            row = jnp.broadcast_to(imp[k:k + 1, :], (8, tq))
            for a in range(n_grp):
                ge = jnp.where(row >= grp[a], 1.0, 0.0)
                gt = jnp.where(row > grp[a], 1.0, 0.0)
                if 8 * a > k:
                    cnt = ge
                elif 8 * a + 7 <= k:
                    cnt = gt
                else:
                    cnt = jnp.where(sub8 > k - 8 * a, ge, gt)
                rank[a] = rank[a] + cnt
        sel_bias = jnp.where(jnp.concatenate(rank, axis=0) < float(SEL_TOPN), 0.0, -MASK_BIG)

        qal_t = qal_ref[0, cols, :].T
        for r in range(NSA_GROUP):
            aug_t = jnp.concatenate([q_t[r * 64:(r + 1) * 64], sel_bias,
                                     qal_t[r * 32:(r + 1) * 32, :]], axis=0)
            qaug_ref[0, 0, r, :, cols] = aug_t.astype(BF16)


def _cmp_attn(q, kc, vct, ovt, qal, hs, bias, gates_t, B, tq=256):
    kern = functools.partial(_cmp_attn_kernel, tq=tq)
    return pl.pallas_call(
        kern,
        grid=(NSA_KV_HEADS, B),
        in_specs=[pl.BlockSpec((SEQ, 256), lambda g, b: (b, g)),
                  pl.BlockSpec((1, 1, N_CMP_PAD, NSA_HEAD_DIM), lambda g, b: (b, g, 0, 0)),
                  pl.BlockSpec((1, 1, NSA_HEAD_DIM, N_CMP_PAD), lambda g, b: (b, g, 0, 0)),
                  pl.BlockSpec((N_SEL, N_CMP_PAD), lambda g, b: (0, 0)),
                  pl.BlockSpec((1, SEQ, LANES), lambda g, b: (g, 0, 0)),
                  pl.BlockSpec((NSA_GROUP, NSA_HEAD_DIM, 256), lambda g, b: (0, 0, 0)),
                  pl.BlockSpec((1, NSA_GROUP, N_CMP_PAD, SEQ), lambda g, b: (g, 0, 0, 0)),
                  pl.BlockSpec((_GATE_ROWS, SEQ), lambda g, b: (g, b))],
        out_specs=[pl.BlockSpec((1, 1, 256, SEQ), lambda g, b: (b, g, 0, 0)),
                   pl.BlockSpec((1, 1, NSA_GROUP, LANES, SEQ), lambda g, b: (b, g, 0, 0, 0))],
        out_shape=[jax.ShapeDtypeStruct((B, NSA_KV_HEADS, 256, SEQ), BF16),
                   jax.ShapeDtypeStruct((B, NSA_KV_HEADS, NSA_GROUP, LANES, SEQ), BF16)],
        compiler_params=_cparams(("parallel", "parallel")),
        name="nsa_cmp_attn_select",
    )(q, kc, vct, ovt, qal, hs, bias, gates_t)


_NSA_VT_ROWS = NSA_HEAD_DIM + 16


def _sel_win_kernel(qaug_ref, ks_ref, vs_ref, kw_ref, vw_ref, gate_ref, ocmp_ref,
                    o_ref, vt_ref, s_ref, bias_ref, *, tq):
    nq = SEQ // tq
    n_back = WINDOW // tq
    a_idx = lax.broadcasted_iota(jnp.int32, (tq, tq), 0)
    b_idx = lax.broadcasted_iota(jnp.int32, (tq, tq), 1)
    bias_ref[0] = jnp.where(a_idx <= b_idx, 0.0, -MASK_BIG)
    bias_ref[1] = jnp.where(a_idx > b_idx, 0.0, -MASK_BIG)
    k_refs = (ks_ref, kw_ref)
    for br, v_ref in enumerate((vs_ref, vw_ref)):
        for i in range(nq):
            rows = slice(i * tq, (i + 1) * tq)
            vt = v_ref[rows, :].astype(F32).T
            vt_ref[br, :, rows] = vt[0:_NSA_VT_ROWS].astype(BF16)

    def tiles_of(i, br):
        if br == 0:
            return [(j, None) for j in range(i)] + [(i, 0)]
        tl = [(i - d, 1 if d == n_back else None) for d in range(n_back, 0, -1) if i >= d]
        return tl + [(i, 0)]

    units = [(i, br) for i in range(nq) for br in range(2)]
    n_buf = s_ref.shape[0]

    def stage1(n):
        i, br = units[n]
        q = jnp.concatenate([qaug_ref[0, 0, r, :, i * tq:(i + 1) * tq] for r in range(NSA_GROUP)],
                            axis=1)
        mx = None
        for t_n, (j, bias) in enumerate(tiles_of(i, br)):
            s = _dot(k_refs[br][j * tq:(j + 1) * tq, :], q)
            if bias is not None:
                s = s + jnp.concatenate([bias_ref[bias]] * NSA_GROUP, axis=1)
            s_ref[n % n_buf, t_n * tq:(t_n + 1) * tq, :] = s
            t = s[0:8]
            for c in range(1, tq // 8):
                t = jnp.maximum(t, s[c * 8:(c + 1) * 8])
            mx = t if mx is None else jnp.maximum(mx, t)
        return jnp.max(mx, axis=0, keepdims=True)

    def stage2(n, m):
        i, br = units[n]
        acc = None
        for t_n, (j, _) in enumerate(tiles_of(i, br)):
            p = jnp.exp(s_ref[n % n_buf, t_n * tq:(t_n + 1) * tq, :] - m).astype(BF16)
            pv = _dot(vt_ref[br, :, j * tq:(j + 1) * tq], p)
            acc = pv if acc is None else acc + pv
        cols = slice(i * tq, (i + 1) * tq)
        parts = []
        for r in range(NSA_GROUP):
            hl = slice(r * tq, (r + 1) * tq)
            g_row = 4 * (br + 1) + r
            scale = jax.nn.sigmoid(gate_ref[g_row:g_row + 1, cols]) / acc[NSA_HEAD_DIM:NSA_HEAD_DIM + 1, hl]
            parts.append(acc[0:NSA_HEAD_DIM, hl] * scale)
        return jnp.concatenate(parts, axis=0)

    ms = [stage1(0), stage1(1)]
    total = None
    for n in range(len(units)):
        if n + 2 < len(units):
            ms.append(stage1(n + 2))
        o_t = stage2(n, ms[n])
        i, br = units[n]
        cols = slice(i * tq, (i + 1) * tq)
        if br == 0:
            total = ocmp_ref[0, 0, :, cols].astype(F32) + o_t
        else:
            o_ref[cols, :] = (total + o_t).T.astype(o_ref.dtype)


def _sel_win_attn(qaug_t, kv, gates_t, ocmp_t, B, tq=256):
    kern = functools.partial(_sel_win_kernel, tq=tq)

    def kv_spec(col):
        return pl.BlockSpec((SEQ, LANES), lambda b, g: (b, col + g))

    return pl.pallas_call(
        kern,
        grid=(B, NSA_KV_HEADS),
        in_specs=[pl.BlockSpec((1, 1, NSA_GROUP, LANES, SEQ), lambda b, g: (b, g, 0, 0, 0)),
                  kv_spec(0), kv_spec(2), kv_spec(4), kv_spec(6),
                  pl.BlockSpec((_GATE_ROWS, SEQ), lambda b, g: (g, b)),
                  pl.BlockSpec((1, 1, 256, SEQ), lambda b, g: (b, g, 0, 0))],
        out_specs=pl.BlockSpec((SEQ, 256), lambda b, g: (b, g)),
        out_shape=jax.ShapeDtypeStruct((B * SEQ, 512), BF16),
        scratch_shapes=[pltpu.VMEM((2, _NSA_VT_ROWS, SEQ), BF16),
                        pltpu.VMEM((3, SEQ, NSA_GROUP * tq), F32),
                        pltpu.VMEM((2, tq, tq), F32)],
        compiler_params=_cparams(("parallel", "parallel")),
        name="nsa_sel_win_attn",
    )(qaug_t, kv, kv, kv, kv, gates_t, ocmp_t)


_GLA_ROWS = 512


def _gla_kernel(qk_ref, v_ref, ga_ref, gz_ref, wg_ref, bg_ref, ng_ref, o_ref,
                b_ref, qd_ref, kd_ref):
    C = GLA_CHUNK
    KW = GLA_HEADS * GLA_DK
    wg = wg_ref[...]
    bg = bg_ref[...]
    norm_g = ng_ref[...]

    rin = lax.broadcasted_iota(jnp.int32, (_GLA_ROWS, KW), 0) % C
    lane = lax.broadcasted_iota(jnp.int32, (_GLA_ROWS, KW), 1)
    for blk in range(SEQ // _GLA_ROWS):
        rows = slice(blk * _GLA_ROWS, (blk + 1) * _GLA_ROWS)
        x = _dot(ga_ref[rows, :].astype(BF16), wg) + bg
        b = -(jnp.maximum(-x, 0.0) + jnp.log1p(jnp.exp(-jnp.abs(x)))) / GLA_TAU
        sh = 1
        while sh < C:
            b = b + jnp.where(rin >= sh, pltpu.roll(b, sh, 0), 0.0)
            sh *= 2
        b_ref[rows, :] = b
        q_d = qk_ref[rows, 0:KW] * (GLA_DK ** -0.5) * jnp.exp(b)
        for h in range(GLA_HEADS):
            in_head = (lane >= h * GLA_DK) & (lane < (h + 1) * GLA_DK)
            qd_ref[h, rows, :] = jnp.where(in_head, q_d, 0.0).astype(BF16)
        kd_ref[rows, :] = (qk_ref[rows, KW:2 * KW] * jnp.exp(-b)).astype(BF16)

    r_idx = lax.broadcasted_iota(jnp.int32, (GLA_HEADS * C, C), 0) % C
    c_idx = lax.broadcasted_iota(jnp.int32, (GLA_HEADS * C, C), 1)
    causal = r_idx >= c_idx

    def prep(n):
        rows = slice(n * C, (n + 1) * C)
        b = b_ref[rows, :]
        b_last = b_ref[n * C + C - 1:(n + 1) * C, :]
        k_u = (qk_ref[rows, KW:2 * KW] * jnp.exp(b_last - b)).astype(BF16)
        v = v_ref[rows, :]
        q_all = jnp.concatenate([qd_ref[h, rows, :] for h in range(GLA_HEADS)], axis=0)
        a = jnp.where(causal, _dot_nt(q_all, kd_ref[rows, :]), 0.0).astype(BF16)
        o_intra = [_dot(a[h * C:(h + 1) * C], v[:, h * GLA_DV:(h + 1) * GLA_DV])
                   for h in range(GLA_HEADS)]
        kv = lax.dot_general(k_u, v, (((0,), (0,)), ((), ())), preferred_element_type=F32)
        upd = jnp.concatenate([kv[h * GLA_DK:(h + 1) * GLA_DK, h * GLA_DV:(h + 1) * GLA_DV]
                               for h in range(GLA_HEADS)], axis=0)
        dec = jnp.exp(jnp.broadcast_to(b_last, (GLA_DV, KW)).T)
        return q_all, o_intra, upd, dec

    st = jnp.zeros((KW, GLA_DV), F32)
    n_chunks = SEQ // C
    nxt = prep(0)
    for n in range(n_chunks):
        q_all, o_intra, upd, dec = nxt
        if n + 1 < n_chunks:
            nxt = prep(n + 1)
        rows = slice(n * C, (n + 1) * C)
        o_inter = _dot(q_all, st.astype(BF16))
        for h in range(GLA_HEADS):
            vs = slice(h * GLA_DV, (h + 1) * GLA_DV)
            o = o_intra[h] + o_inter[h * C:(h + 1) * C]
            y = o * lax.rsqrt(jnp.mean(o * o, axis=-1, keepdims=True) + NORM_EPS) * norm_g
            y = y * gz_ref[rows, vs].astype(F32)
            o_ref[rows, vs] = y.astype(o_ref.dtype)
        st = st * dec + upd


def _gla(gqk, gv, ga, gz, wg, bg, norm_g, B):
    row = lambda b: (b, 0)
    const = lambda b: (0, 0)
    return pl.pallas_call(
        _gla_kernel,
        grid=(B,),
        in_specs=[pl.BlockSpec((SEQ, 512), row), pl.BlockSpec((SEQ, 512), row),
                  pl.BlockSpec((SEQ, LANES), row), pl.BlockSpec((SEQ, 512), row),
                  pl.BlockSpec((LANES, 256), const), pl.BlockSpec((1, 256), const),
                  pl.BlockSpec((1, GLA_DV), const)],
        out_specs=pl.BlockSpec((SEQ, 512), row),
        out_shape=jax.ShapeDtypeStruct((B * SEQ, 512), BF16),
        scratch_shapes=[pltpu.VMEM((SEQ, GLA_HEADS * GLA_DK), F32),
                        pltpu.VMEM((GLA_HEADS, SEQ, GLA_HEADS * GLA_DK), BF16),
                        pltpu.VMEM((SEQ, GLA_HEADS * GLA_DK), BF16)],
        compiler_params=_cparams(("parallel",)),
        name="gla_chunked",
    )(gqk, gv, ga, gz, wg, bg, norm_g)


def _deepnorm_ln(x, y, g, b):
    r = DEEPNORM_ALPHA * x + y
    mu = jnp.mean(r, axis=-1, keepdims=True)
    d = r - mu
    var = jnp.mean(d * d, axis=-1, keepdims=True)
    return d * lax.rsqrt(var + NORM_EPS) * g + b


def _even_out_kernel(onsa_ref, nz_ref, ogla_ref, x_ref, w_ref, g_ref, b_ref, o_ref):
    o_nsa = (onsa_ref[...].astype(F32) * nz_ref[...].astype(F32)).astype(BF16)
    y = _dot(o_nsa, w_ref[0:512, :]) + _dot(ogla_ref[...], w_ref[512:1024, :])
    o_ref[...] = _deepnorm_ln(x_ref[...], y, g_ref[...], b_ref[...])


def _even_out(onsa, nz, ogla, x2, w_bf, ln_g, ln_b, tm=512):
    T = x2.shape[0]
    row = lambda i: (i, 0)
    const = lambda i: (0, 0)
    return pl.pallas_call(
        _even_out_kernel,
        grid=(T // tm,),
        in_specs=[pl.BlockSpec((tm, 512), row), pl.BlockSpec((tm, 512), row),
                  pl.BlockSpec((tm, 512), row), pl.BlockSpec((tm, D_MODEL), row),
                  pl.BlockSpec((1024, D_MODEL), const),
                  pl.BlockSpec((1, D_MODEL), const), pl.BlockSpec((1, D_MODEL), const)],
        out_specs=pl.BlockSpec((tm, D_MODEL), row),
        out_shape=jax.ShapeDtypeStruct((T, D_MODEL), F32),
        compiler_params=_cparams(("parallel",)),
        name="even_out_proj_ln",
    )(onsa, nz, ogla, x2, w_bf, ln_g, ln_b)


def _odd_out_kernel(o_ref_in, z_ref, x_ref, w_ref, g_ref, b_ref, o_ref):
    o = (o_ref_in[...].astype(F32) * z_ref[...].astype(F32)).astype(BF16)
    y = _dot(o, w_ref[...])
    o_ref[...] = _deepnorm_ln(x_ref[...], y, g_ref[...], b_ref[...])


def _odd_out(o, z, x2, w_bf, ln_g, ln_b, tm=512):
    T = x2.shape[0]
    row = lambda i: (i, 0)
    const = lambda i: (0, 0)
    return pl.pallas_call(
        _odd_out_kernel,
        grid=(T // tm,),
        in_specs=[pl.BlockSpec((tm, D_MODEL), row), pl.BlockSpec((tm, D_MODEL), row),
                  pl.BlockSpec((tm, D_MODEL), row), pl.BlockSpec((D_MODEL, D_MODEL), const),
                  pl.BlockSpec((1, D_MODEL), const), pl.BlockSpec((1, D_MODEL), const)],
        out_specs=pl.BlockSpec((tm, D_MODEL), row),
        out_shape=jax.ShapeDtypeStruct((T, D_MODEL), F32),
        compiler_params=_cparams(("parallel",)),
        name="odd_out_proj_ln",
    )(o, z, x2, w_bf, ln_g, ln_b)


_HEAD_W = LANES
_QK_W = MLA_HEADS * _HEAD_W
_ROPE_HALF = MLA_ROPE // 2


def _odd_weights(w_in, w_uq, w_ukv):
    o = _O_OFF
    z_nope = jnp.zeros((D_MODEL, MLA_NOPE), F32)
    z_tail = jnp.zeros((D_MODEL, _HEAD_W - MLA_NOPE - MLA_ROPE), F32)
    w1 = jnp.concatenate([w_in[:, o[0]:o[2]], z_nope, w_in[:, o[2]:o[3]], z_tail,
                          w_in[:, o[3]:o[4]]], axis=1).astype(BF16)
    uq = w_uq.reshape(MLA_Q_LORA, MLA_HEADS, MLA_NOPE + MLA_ROPE)
    zq = jnp.zeros((MLA_Q_LORA, MLA_HEADS, _HEAD_W - MLA_NOPE - MLA_ROPE), F32)
    wq = jnp.concatenate([uq, zq], axis=-1).reshape(MLA_Q_LORA, _QK_W).astype(BF16)
    ukv = w_ukv.reshape(MLA_KV_LORA, MLA_HEADS, MLA_NOPE + MLA_V)
    wk = ukv[..., :MLA_NOPE].reshape(MLA_KV_LORA, MLA_HEADS * MLA_NOPE).astype(BF16)
    wv = ukv[..., MLA_NOPE:].reshape(MLA_KV_LORA, MLA_HEADS * MLA_V).astype(BF16)
    return w1, wq, wk, wv


def _rope_tables():
    freqs = jnp.exp(-math.log(ROPE_THETA) * jnp.arange(_ROPE_HALF, dtype=F32) * 2.0 / MLA_ROPE)
    ang = jnp.arange(SEQ, dtype=F32)[:, None] * freqs[None, :]
    cos, sin = jnp.cos(ang), jnp.sin(ang)
    z_half = jnp.zeros((SEQ, _ROPE_HALF), F32)
    z_tail = jnp.zeros((SEQ, _HEAD_W - MLA_NOPE - MLA_ROPE), F32)
    z_nope = jnp.zeros((SEQ, MLA_NOPE), F32)

    def tables(scale, nope_gain):
        nope = jnp.full((SEQ, MLA_NOPE), nope_gain, F32)
        c = jnp.concatenate([nope, cos * scale, cos * scale, z_tail], axis=1)
        s1 = jnp.concatenate([z_nope, -sin * scale, z_half, z_tail], axis=1)
        s2 = jnp.concatenate([z_nope, z_half, sin * scale, z_tail], axis=1)
        return c, s1, s2

    q_scale = (MLA_NOPE + MLA_ROPE) ** -0.5 * math.log2(math.e)
    return tables(q_scale, q_scale), tables(1.0, 0.0)


def _rope_block(x, c, s1, s2):
    return x * c + pltpu.roll(x, LANES - _ROPE_HALF, 1) * s1 + pltpu.roll(x, _ROPE_HALF, 1) * s2


def _rms(x, g):
    return x * lax.rsqrt(jnp.mean(x * x, axis=-1, keepdims=True) + NORM_EPS) * g


def _odd_in_kernel(x_ref, w1_ref, wq_ref, wk_ref, wv_ref, qn_ref, kn_ref,
                   qc_ref, qs1_ref, qs2_ref, kc_ref, ks1_ref, ks2_ref,
                   q_ref, k_ref, kr_ref, v_ref, z_ref):
    xb = x_ref[...].astype(BF16)
    c_q = _dot(xb, w1_ref[:, 0:MLA_Q_LORA])
    c_kv = _dot(xb, w1_ref[:, MLA_Q_LORA:640])
    kr = _dot(xb, w1_ref[:, 640:768])
    z_ref[...] = _silu(_dot(xb, w1_ref[:, 768:1792])).astype(BF16)
    cqn = _rms(c_q, qn_ref[...]).astype(BF16)
    ckvn = _rms(c_kv, kn_ref[...]).astype(BF16)
    qa = _dot(cqn, wq_ref[...])
    qc, qs1, qs2 = qc_ref[...], qs1_ref[...], qs2_ref[...]
    for h in range(MLA_HEADS):
        sl = slice(h * _HEAD_W, (h + 1) * _HEAD_W)
        q_ref[:, sl] = _rope_block(qa[:, sl], qc, qs1, qs2).astype(BF16)
    kr_ref[...] = _rope_block(kr, kc_ref[...], ks1_ref[...], ks2_ref[...]).astype(BF16)
    k_ref[...] = _dot(ckvn, wk_ref[...]).astype(BF16)
    v_ref[...] = _dot(ckvn, wv_ref[...]).astype(BF16)


def _odd_in_proj(x2, w1, wq, wk, wv, qn, kn, q_tabs, k_tabs, tm=256):
    T = x2.shape[0]
    s_tiles = SEQ // tm
    row = lambda i: (i, 0)
    const = lambda i: (0, 0)
    pos = lambda i: (i % s_tiles, 0)

    def full(a):
        return pl.BlockSpec(a.shape, const)

    tab = pl.BlockSpec((tm, LANES), pos)
    n_kv = MLA_HEADS * MLA_NOPE
    return pl.pallas_call(
        _odd_in_kernel,
        grid=(T // tm,),
        in_specs=[pl.BlockSpec((tm, D_MODEL), row), full(w1), full(wq), full(wk), full(wv),
                  full(qn), full(kn), tab, tab, tab, tab, tab, tab],
        out_specs=[pl.BlockSpec((tm, _QK_W), row), pl.BlockSpec((tm, n_kv), row),
                   pl.BlockSpec((tm, LANES), row), pl.BlockSpec((tm, n_kv), row),
                   pl.BlockSpec((tm, 1024), row)],
        out_shape=[jax.ShapeDtypeStruct((T, _QK_W), BF16), jax.ShapeDtypeStruct((T, n_kv), BF16),
                   jax.ShapeDtypeStruct((T, LANES), BF16), jax.ShapeDtypeStruct((T, n_kv), BF16),
                   jax.ShapeDtypeStruct((T, 1024), BF16)],
        compiler_params=_cparams(("parallel",)),
        name="odd_in_proj",
    )(x2, w1, wq, wk, wv, qn, kn, *q_tabs, *k_tabs)


_VT_ROWS = MLA_V + 16


def _mla_kernel(q_ref, kn_ref, kr_ref, v_ref, o_ref, kx_ref, qt_ref, vt_ref, s_ref, bias_ref, *, tq):
    nq = SEQ // tq
    a_idx = lax.broadcasted_iota(jnp.int32, (tq, tq), 0)
    b_idx = lax.broadcasted_iota(jnp.int32, (tq, tq), 1)
    bias_ref[...] = jnp.where(a_idx <= b_idx, 0.0, NEG_INF)
    lane = lax.broadcasted_iota(jnp.int32, (tq, LANES), 1)
    ones_rows = jnp.where(lax.broadcasted_iota(jnp.int32, (_VT_ROWS - MLA_V, tq), 0) == 0, 1.0, 0.0)

    for i in range(nq):
        rows = slice(i * tq, (i + 1) * tq)
        knp = kn_ref[rows, :].astype(F32)
        krb = kr_ref[rows, :].astype(F32)
        vt = v_ref[rows, :].astype(F32).T
        for h in range(2):
            kn_h = knp if h == 0 else pltpu.roll(knp, MLA_NOPE, 1)
            kx_ref[h, rows, :] = jnp.where(lane < MLA_NOPE, kn_h, krb).astype(BF16)
            qt_ref[h, :, rows] = q_ref[rows, h * _HEAD_W:(h + 1) * _HEAD_W].astype(F32).T.astype(BF16)
            vt_ref[h, :, rows] = jnp.concatenate(
                [vt[h * MLA_V:(h + 1) * MLA_V], ones_rows], axis=0).astype(BF16)

    units = [(i, h) for i in range(nq) for h in range(2)]
    n_buf = s_ref.shape[0]

    def stage1(n):
        i, h = units[n]
        qt = qt_ref[h, :, i * tq:(i + 1) * tq]
        mx = None
        for j in range(i + 1):
            cols = slice(j * tq, (j + 1) * tq)
            s = _dot(kx_ref[h, cols, :], qt)
            if j == i:
                s = s + bias_ref[...]
            s_ref[n % n_buf, cols, :] = s
            t = s[0:8]
            for c in range(1, tq // 8):
                t = jnp.maximum(t, s[c * 8:(c + 1) * 8])
            mx = t if mx is None else jnp.maximum(mx, t)
        return jnp.max(mx, axis=0, keepdims=True)

    def stage2(n, m):
        i, h = units[n]
        acc = None
        for j in range(i + 1):
            cols = slice(j * tq, (j + 1) * tq)
            p = jnp.exp2(s_ref[n % n_buf, cols, :] - m).astype(BF16)
            pv = _dot(vt_ref[h, :, cols], p)
            acc = pv if acc is None else acc + pv
        return acc[0:MLA_V] / acc[MLA_V:MLA_V + 1]

    ms = [stage1(0), stage1(1)]
    outs = []
    for n in range(len(units)):
        if n + 2 < len(units):
            ms.append(stage1(n + 2))
        outs.append(stage2(n, ms[n]))
        if len(outs) == 2:
            i = units[n][0]
            o_ref[i * tq:(i + 1) * tq, :] = jnp.concatenate(outs, axis=0).T.astype(o_ref.dtype)
            outs = []


def _mla_attn(q, kn, kr, v, B, tq=256):
    kern = functools.partial(_mla_kernel, tq=tq)
    pair = pl.BlockSpec((SEQ, LANES), lambda b, h: (b, h))
    return pl.pallas_call(
        kern,
        grid=(B, MLA_HEADS // 2),
        in_specs=[pl.BlockSpec((SEQ, 2 * _HEAD_W), lambda b, h: (b, h)), pair,
                  pl.BlockSpec((SEQ, LANES), lambda b, h: (b, 0)), pair],
        out_specs=pair,
        out_shape=jax.ShapeDtypeStruct((B * SEQ, MLA_HEADS * MLA_V), BF16),
        scratch_shapes=[pltpu.VMEM((2, SEQ, _HEAD_W), BF16), pltpu.VMEM((2, _HEAD_W, SEQ), BF16),
                        pltpu.VMEM((2, _VT_ROWS, SEQ), BF16), pltpu.VMEM((4, SEQ, tq), F32),
                        pltpu.VMEM((tq, tq), F32)],
        compiler_params=_cparams(("parallel", "parallel")),
        name="mla_attn",
    )(q, kn, kr, v)


def _even_layer(x2, B, w_in, cmp_pe, cmp_w1, cmp_w2, gla_w_gate, gla_b_gate, gla_norm,
                w_out, ln_g, ln_b):
    T = x2.shape[0]
    q, kvc, kv, ng, nz, gqk, gv, ga, gz = _even_in_proj(x2, _even_in_weight(w_in), _nsa_kv_table())

    pe_tab, w1_bd = _compress_weights(cmp_pe, cmp_w1)
    kc, vct = _compress(kvc, pe_tab, w1_bd, cmp_w2, B)

    ocmp_t, qaug_t = _cmp_attn(q, kc, vct, _overlap_t(), _q_alibi_table(), _head_select_mats(),
                               _cmp_bias_table(), ng, B)
    onsa = _sel_win_attn(qaug_t, kv, ng, ocmp_t, B)

    wg = jnp.pad(gla_w_gate, ((0, LANES - GLA_GATE_RANK), (0, 0))).astype(BF16)
    ogla = _gla(gqk, gv, ga, gz, wg, gla_b_gate.reshape(1, -1), gla_norm.reshape(1, -1), B)

    return _even_out(onsa, nz, ogla, x2, w_out.astype(BF16), ln_g.reshape(1, -1), ln_b.reshape(1, -1))


def _odd_layer(x2, B, w_in, q_norm, w_uq, kv_norm, w_ukv, w_out, ln_g, ln_b):
    w1, wq, wk, wv = _odd_weights(w_in, w_uq, w_ukv)
    q_tabs, k_tabs = _rope_tables()
    q, kn, kr, v, z = _odd_in_proj(x2, w1, wq, wk, wv, q_norm.reshape(1, -1),
                                   kv_norm.reshape(1, -1), q_tabs, k_tabs)
    o = _mla_attn(q, kn, kr, v, B)
    return _odd_out(o, z, x2, w_out.astype(BF16), ln_g.reshape(1, -1), ln_b.reshape(1, -1))


def kernel(x, e_w_in, e_cmp_pe, e_cmp_w1, e_cmp_w2, e_gla_w_gate, e_gla_b_gate, e_gla_norm,
           e_w_out, e_ln_g, e_ln_b, o_w_in, o_q_norm, o_w_uq, o_kv_norm, o_w_ukv, o_w_out,
           o_ln_g, o_ln_b):
    B, S, D = x.shape
    x2 = x.reshape(B * S, D)
    for layer in range(DEPTH):
        i = layer // 2
        if layer % 2 == 0:
            x2 = _even_layer(x2, B, e_w_in[i], e_cmp_pe[i], e_cmp_w1[i], e_cmp_w2[i],
                             e_gla_w_gate[i], e_gla_b_gate[i], e_gla_norm[i], e_w_out[i],
                             e_ln_g[i], e_ln_b[i])
        else:
            x2 = _odd_layer(x2, B, o_w_in[i], o_q_norm[i], o_w_uq[i], o_kv_norm[i], o_w_ukv[i],
                            o_w_out[i], o_ln_g[i], o_ln_b[i])
    return x2.reshape(B, S, D)
```

```python
import functools
import math

import jax
import jax.numpy as jnp
import numpy as np
from jax import lax
from jax.experimental import pallas as pl
from jax.experimental.pallas import tpu as pltpu

F32 = jnp.float32
BF16 = jnp.bfloat16

D_MODEL = 1024
SEQ = 2048
DEPTH = 2
NSA_HEADS = 8
NSA_KV_HEADS = 2
NSA_GROUP = 4
NSA_HEAD_DIM = 64
CMP_BLOCK = 32
CMP_STRIDE = 16
CMP_HIDDEN = 128
N_CMP = (SEQ - CMP_BLOCK) // CMP_STRIDE + 1
N_CMP_PAD = 128
SEL_BLOCK = 64
N_SEL = SEQ // SEL_BLOCK
SEL_TOPN = 8
WINDOW = 512
FORCE_BONUS = 1000.0
GLA_HEADS = 4
GLA_DK = 64
GLA_DV = 128
GLA_GATE_RANK = 16
GLA_TAU = 16.0
GLA_CHUNK = 64
MLA_HEADS = 16
MLA_NOPE = 64
MLA_ROPE = 32
MLA_V = 64
MLA_Q_LORA = 384
MLA_KV_LORA = 256
ROPE_THETA = 10000.0
NORM_EPS = 1e-5
NEG_INF = -1e30
DEEPNORM_ALPHA = (2 * DEPTH) ** 0.25

LANES = 128
MASK_BIG = 2.0 ** 100
VMEM_LIMIT = 56 * 1024 * 1024

_E_OFF = np.cumsum([0, 512, 768, 24, 512, 256, 256, 512, 16, 512])
_O_OFF = np.cumsum([0, MLA_Q_LORA, MLA_KV_LORA, MLA_ROPE, MLA_HEADS * MLA_V])


def _cparams(sem):
    return pltpu.CompilerParams(dimension_semantics=sem, vmem_limit_bytes=VMEM_LIMIT)


def _silu(x):
    return x * jax.nn.sigmoid(x)


def _dot(a, b):
    return jnp.dot(a, b, preferred_element_type=F32)


def _dot_nt(a, b):
    return lax.dot_general(a, b, (((1,), (1,)), ((), ())), preferred_element_type=F32)


_A_SEGS = (("q", 512, BF16), ("kvc", 256, F32), ("kv", 512, BF16), ("ng", 128, F32),
           ("nz", 512, BF16), ("gqk", 512, F32), ("gv", 512, BF16), ("ga", 128, F32),
           ("gz", 512, BF16))
_A_SILU = ("nz", "gz")
_A_OFF = np.cumsum([0] + [s[1] for s in _A_SEGS])


def _even_in_weight(w):
    o = _E_OFF
    nkv = w[:, o[1]:o[2]]

    def kvcol(br, j, g):
        c = ((br * 2 + j) * NSA_KV_HEADS + g) * 64
        return nkv[:, c:c + 64]

    kv_blocks = [kvcol(br, j, g) for j in (0, 1) for g in (0, 1) for br in (1, 2)]
    gates = w[:, o[2]:o[3]].reshape(D_MODEL, 3, NSA_KV_HEADS, NSA_GROUP).transpose(0, 2, 1, 3)
    gates = jnp.pad(gates.reshape(D_MODEL, NSA_KV_HEADS, 12), ((0, 0), (0, 0), (0, 4)))
    ng = jnp.pad(gates.reshape(D_MODEL, 32), ((0, 0), (0, LANES - 32)))
    ga = jnp.pad(w[:, o[7]:o[8]], ((0, 0), (0, LANES - GLA_GATE_RANK)))
    cols = [w[:, o[0]:o[1]] * (NSA_HEAD_DIM ** -0.5), nkv[:, :256]] + kv_blocks + [
        ng, w[:, o[3]:o[4]], w[:, o[4]:o[6]], w[:, o[6]:o[7]], ga, w[:, o[8]:o[9]]]
    return jnp.concatenate(cols, axis=1).astype(BF16)


def _nsa_k_tables():
    pos = np.arange(SEQ)
    k_al = np.zeros((SEQ, LANES), np.float32)
    k_al[:, 96] = 1.0
    k_al[:, 97] = 1.0
    k_al[:, 98] = pos // SEL_BLOCK
    k_al[:, 99] = pos % SEL_BLOCK
    k_sel = k_al.copy()
    k_sel[pos, 64 + pos // SEL_BLOCK] = 1.0
    return jnp.asarray(np.stack([k_sel, k_al]))


def _even_in_kernel(x_ref, w_ref, wg_ref, *out_refs):
    xb = x_ref[...].astype(BF16)
    for (name, width, dt), off, o_ref in zip(_A_SEGS, _A_OFF[:-1], out_refs):
        if name == "ng":
            o_ref[...] = _dot_nt(wg_ref[...], xb)
            continue
        r = _dot(xb, w_ref[:, off:off + width])
        if name in _A_SILU:
            r = _silu(r)
        o_ref[...] = r.astype(dt)


def _even_in_proj(x2, w_bf, tm=512):
    T = x2.shape[0]
    n_w = w_bf.shape[1]
    ng_off = _A_OFF[[n for n, _, _ in _A_SEGS].index("ng")]
    wg_t = w_bf[:, ng_off:ng_off + LANES].T
    out_shape = [jax.ShapeDtypeStruct((wd, T) if name == "ng" else (T, wd), dt)
                 for name, wd, dt in _A_SEGS]
    out_specs = [pl.BlockSpec((wd, tm), lambda i: (0, i)) if name == "ng"
                 else pl.BlockSpec((tm, wd), lambda i: (i, 0)) for name, wd, _ in _A_SEGS]
    return pl.pallas_call(
        _even_in_kernel,
        grid=(T // tm,),
        in_specs=[pl.BlockSpec((tm, D_MODEL), lambda i: (i, 0)),
                  pl.BlockSpec((D_MODEL, n_w), lambda i: (0, 0)),
                  pl.BlockSpec((LANES, D_MODEL), lambda i: (0, 0))],
        out_specs=out_specs,
        out_shape=out_shape,
        compiler_params=_cparams(("parallel",)),
        name="even_in_proj",
    )(x2, w_bf, wg_t)


def _compress_weights(cmp_pe, cmp_w1):
    pe_tab = jnp.concatenate([cmp_pe[0], cmp_pe[0], cmp_pe[1], cmp_pe[1]], axis=1)
    w = cmp_w1.reshape(2, CMP_BLOCK, NSA_HEAD_DIM, CMP_HIDDEN)
    z = jnp.zeros_like(w)
    w_bd = jnp.concatenate([jnp.concatenate([w, z], axis=3), jnp.concatenate([z, w], axis=3)], axis=2)
    return pe_tab, w_bd.astype(BF16)


def _compress_kernel(xk_ref, xv_ref, pe_ref, w1_ref, w2_ref, w2t_ref, kc_ref, vct_ref):
    half = CMP_BLOCK // 2
    acc = [[None, None], [None, None]]
    for l in range(half):
        for j, x_ref in enumerate((xk_ref, xv_ref)):
            xs = x_ref[pl.ds(l, N_CMP_PAD, stride=CMP_STRIDE), :]
            for part in range(2):
                row = part * half + l
                xb = (xs + pe_ref[row:row + 1, j * LANES:(j + 1) * LANES]).astype(BF16)
                d = _dot(xb, w1_ref[j, row])
                acc[j][part] = d if acc[j][part] is None else acc[j][part] + d
    for j in range(2):
        h = acc[j][0] + pltpu.roll(acc[j][1], N_CMP_PAD - 1, 0)
        hs = _silu(h).astype(BF16)
        for g in range(NSA_KV_HEADS):
            hg = hs[:, g * CMP_HIDDEN:(g + 1) * CMP_HIDDEN]
            if j == 0:
                kc_ref[0, g] = _dot(hg, w2_ref[...])
            else:
                vct_ref[0, g] = _dot_nt(w2t_ref[...], hg)


def _compress(kvc, pe_tab, w1_bd, w2, B):
    w2k = w2[0].astype(BF16)
    w2vt = w2[1].T.astype(BF16)
    return pl.pallas_call(
        _compress_kernel,
        grid=(B,),
        in_specs=[pl.BlockSpec((SEQ, LANES), lambda b: (b, 0)),
                  pl.BlockSpec((SEQ, LANES), lambda b: (b, 1)),
                  pl.BlockSpec((CMP_BLOCK, 256), lambda b: (0, 0)),
                  pl.BlockSpec((2, CMP_BLOCK, LANES, 256), lambda b: (0, 0, 0, 0)),
                  pl.BlockSpec((CMP_HIDDEN, NSA_HEAD_DIM), lambda b: (0, 0)),
                  pl.BlockSpec((NSA_HEAD_DIM, CMP_HIDDEN), lambda b: (0, 0))],
        out_specs=[pl.BlockSpec((1, NSA_KV_HEADS, N_CMP_PAD, NSA_HEAD_DIM), lambda b: (b, 0, 0, 0)),
                   pl.BlockSpec((1, NSA_KV_HEADS, NSA_HEAD_DIM, N_CMP_PAD), lambda b: (b, 0, 0, 0))],
        out_shape=[jax.ShapeDtypeStruct((B, NSA_KV_HEADS, N_CMP_PAD, NSA_HEAD_DIM), F32),
                   jax.ShapeDtypeStruct((B, NSA_KV_HEADS, NSA_HEAD_DIM, N_CMP_PAD), F32)],
        compiler_params=_cparams(("parallel",)),
        name="nsa_compress",
    )(kvc, kvc, pe_tab, w1_bd, w2k, w2vt)


def _alibi_slopes_np():
    return np.exp2(-(8.0 / NSA_HEADS) * np.arange(1, NSA_HEADS + 1)).astype(np.float32)


def _overlap_t():
    cs = np.arange(N_CMP) * CMP_STRIDE
    ss = np.arange(N_SEL) * SEL_BLOCK
    ov = np.clip(np.minimum(cs[:, None] + CMP_BLOCK, ss[None, :] + SEL_BLOCK)
                 - np.maximum(cs[:, None], ss[None, :]), 0, None).astype(np.float32) / CMP_BLOCK
    ovt = np.zeros((N_SEL, N_CMP_PAD), np.float32)
    ovt[:, :N_CMP] = ov.T
    return jnp.asarray(ovt)


def _q_alibi_table():
    slopes = _alibi_slopes_np().reshape(NSA_KV_HEADS, NSA_GROUP)
    pos = np.arange(SEQ)
    tab = np.zeros((NSA_KV_HEADS, SEQ, LANES), np.float32)
    for g in range(NSA_KV_HEADS):
        for r in range(NSA_GROUP):
            m = slopes[g, r]
            tab[g, :, r * 32 + 0] = -m * SEL_BLOCK * (pos // SEL_BLOCK)
            tab[g, :, r * 32 + 1] = -m * (pos % SEL_BLOCK)
            tab[g, :, r * 32 + 2] = m * SEL_BLOCK
            tab[g, :, r * 32 + 3] = m
    return jnp.asarray(tab)


def _head_select_mats():
    m = np.zeros((NSA_GROUP, NSA_HEAD_DIM, 256), np.float32)
    for r in range(NSA_GROUP):
        m[r, np.arange(64), r * 64 + np.arange(64)] = 1.0
    return jnp.asarray(m, dtype=BF16)


_GATE_ROWS = 16


def _cmp_bias_table():
    slopes = jnp.asarray(_alibi_slopes_np()).reshape(NSA_KV_HEADS, NSA_GROUP, 1, 1)
    n = jnp.arange(N_CMP_PAD)[:, None]
    t = jnp.arange(SEQ)[None, :]
    dist = t - (n * CMP_STRIDE + (CMP_BLOCK - 1))
    visible = (dist >= 0) & (n < N_CMP)
    return jnp.where(visible, -slopes * dist.astype(F32), NEG_INF)


def _cmp_attn_kernel(q_ref, kc_ref, vct_ref, ovt_ref, qal_ref, hs_ref, bias_ref, gate_ref,
                     ocmp_ref, qaug_ref, *, tq):
    kc = kc_ref[0, 0].astype(BF16)
    vct = vct_ref[0, 0].astype(BF16)
    k_ext = [_dot(kc, hs_ref[r]).astype(BF16) for r in range(NSA_GROUP)]
    ri = lax.broadcasted_iota(jnp.int32, (256, 256), 0)
    ci = lax.broadcasted_iota(jnp.int32, (256, 256), 1)
    eye = jnp.where(ri == ci, 1.0, 0.0).astype(BF16)
    sub8 = lax.broadcasted_iota(jnp.int32, (8, tq), 0)
    j_idx = lax.broadcasted_iota(jnp.int32, (N_SEL, tq), 0)
    t_lane = lax.broadcasted_iota(jnp.int32, (N_SEL, tq), 1)
    n_grp = N_SEL // 8
    for i in range(SEQ // tq):
        cols = slice(i * tq, (i + 1) * tq)
        nv = min(N_CMP_PAD, (i + 1) * tq // CMP_STRIDE)
        q = q_ref[cols, :]
        lhs = jnp.concatenate([k[:nv] for k in k_ext] + [eye], axis=0)
        res = _dot_nt(lhs, q)
        q_t = res[NSA_GROUP * nv:]
        p_sum = None
        o_parts = []
        for r in range(NSA_GROUP):
            s = res[r * nv:(r + 1) * nv] + bias_ref[0, r, 0:nv, cols]
            m = jnp.max(s, axis=0, keepdims=True)
            e = jnp.exp(s - m)
            if i == 0:
                e = jnp.where(bias_ref[0, r, 0:nv, cols] > 0.5 * NEG_INF, e, 0.0)
            l = jnp.sum(e, axis=0, keepdims=True)
            p = e / jnp.where(l > 0.0, l, 1.0)
            p_sum = p if p_sum is None else p_sum + p
            gate = jax.nn.sigmoid(gate_ref[r:r + 1, cols])
            o_parts.append(_dot(vct[:, :nv], p.astype(BF16)) * gate)
        ocmp_ref[0, 0, :, cols] = jnp.concatenate(o_parts, axis=0).astype(ocmp_ref.dtype)

        imp = jnp.dot(ovt_ref[:, 0:nv], p_sum, preferred_element_type=F32,
                      precision=lax.Precision.HIGHEST)
        cur = (i * tq + t_lane) // SEL_BLOCK
        forced = (j_idx == 0) | (j_idx == cur) | (j_idx == cur - 1)
        imp = jnp.where(j_idx > cur, -1.0, imp + jnp.where(forced, FORCE_BONUS, 0.0))
        grp = [imp[8 * a:8 * (a + 1)] for a in range(n_grp)]
        rank = [jnp.zeros((8, tq), F32) for _ in range(n_grp)]
        for k in range(N_SEL):
            row = jnp.broadcast_to(imp[k:k + 1, :], (8, tq))
            for a in range(n_grp):
                ge = jnp.where(row >= grp[a], 1.0, 0.0)
                gt = jnp.where(row > grp[a], 1.0, 0.0)
                if 8 * a > k:
                    cnt = ge
                elif 8 * a + 7 <= k:
                    cnt = gt
                else:
                    cnt = jnp.where(sub8 > k - 8 * a, ge, gt)
                rank[a] = rank[a] + cnt
        sel_bias = jnp.where(jnp.concatenate(rank, axis=0) < float(SEL_TOPN), 0.0, -MASK_BIG)

        qal_t = qal_ref[0, cols, :].T
        for r in range(NSA_GROUP):
            aug_t = jnp.concatenate([q_t[r * 64:(r + 1) * 64], sel_bias,
                                     qal_t[r * 32:(r + 1) * 32, :]], axis=0)
            qaug_ref[0, 0, r, :, cols] = aug_t.astype(BF16)


def _cmp_attn(q, kc, vct, ovt, qal, hs, bias, gates_t, B, tq=256):
    kern = functools.partial(_cmp_attn_kernel, tq=tq)
    return pl.pallas_call(
        kern,
        grid=(NSA_KV_HEADS, B),
        in_specs=[pl.BlockSpec((SEQ, 256), lambda g, b: (b, g)),
                  pl.BlockSpec((1, 1, N_CMP_PAD, NSA_HEAD_DIM), lambda g, b: (b, g, 0, 0)),
                  pl.BlockSpec((1, 1, NSA_HEAD_DIM, N_CMP_PAD), lambda g, b: (b, g, 0, 0)),
                  pl.BlockSpec((N_SEL, N_CMP_PAD), lambda g, b: (0, 0)),
                  pl.BlockSpec((1, SEQ, LANES), lambda g, b: (g, 0, 0)),
                  pl.BlockSpec((NSA_GROUP, NSA_HEAD_DIM, 256), lambda g, b: (0, 0, 0)),
                  pl.BlockSpec((1, NSA_GROUP, N_CMP_PAD, SEQ), lambda g, b: (g, 0, 0, 0)),
                  pl.BlockSpec((_GATE_ROWS, SEQ), lambda g, b: (g, b))],
        out_specs=[pl.BlockSpec((1, 1, 256, SEQ), lambda g, b: (b, g, 0, 0)),
                   pl.BlockSpec((1, 1, NSA_GROUP, LANES, SEQ), lambda g, b: (b, g, 0, 0, 0))],
        out_shape=[jax.ShapeDtypeStruct((B, NSA_KV_HEADS, 256, SEQ), BF16),
                   jax.ShapeDtypeStruct((B, NSA_KV_HEADS, NSA_GROUP, LANES, SEQ), BF16)],
        compiler_params=_cparams(("parallel", "parallel")),
        name="nsa_cmp_attn_select",
    )(q, kc, vct, ovt, qal, hs, bias, gates_t)


_NSA_VT_ROWS = NSA_HEAD_DIM + 16


def _sel_win_kernel(qaug_ref, k_ref, v_ref, ktab_ref, gate_ref, ocmp_ref,
                    o_ref, kx_ref, vt_ref, s_ref, bias_ref, *, tq):
    nq = SEQ // tq
    n_back = WINDOW // tq
    a_idx = lax.broadcasted_iota(jnp.int32, (tq, tq), 0)
    b_idx = lax.broadcasted_iota(jnp.int32, (tq, tq), 1)
    bias_ref[0] = jnp.where(a_idx <= b_idx, 0.0, -MASK_BIG)
    bias_ref[1] = jnp.where(a_idx > b_idx, 0.0, -MASK_BIG)
    lane = lax.broadcasted_iota(jnp.int32, (tq, LANES), 1)
    ones_rows = jnp.where(
        lax.broadcasted_iota(jnp.int32, (_NSA_VT_ROWS - NSA_HEAD_DIM, tq), 0) == 0, 1.0, 0.0)
    for i in range(nq):
        rows = slice(i * tq, (i + 1) * tq)
        kp = k_ref[rows, :].astype(F32)
        vt = v_ref[rows, :].astype(F32).T
        for br in range(2):
            k_br = kp if br == 0 else pltpu.roll(kp, NSA_HEAD_DIM, 1)
            kx_ref[br, rows, :] = jnp.where(lane < NSA_HEAD_DIM, k_br, ktab_ref[br, rows, :]).astype(BF16)
            vt_ref[br, :, rows] = jnp.concatenate(
                [vt[br * NSA_HEAD_DIM:(br + 1) * NSA_HEAD_DIM], ones_rows], axis=0).astype(BF16)

    def tiles_of(i, br):
        if br == 0:
            return [(j, None) for j in range(i)] + [(i, 0)]
        tl = [(i - d, 1 if d == n_back else None) for d in range(n_back, 0, -1) if i >= d]
        return tl + [(i, 0)]

    units = [(i, br) for i in range(nq) for br in range(2)]
    n_buf = s_ref.shape[0]

    def stage1(n):
        i, br = units[n]
        q = jnp.concatenate([qaug_ref[0, 0, r, :, i * tq:(i + 1) * tq] for r in range(NSA_GROUP)],
                            axis=1)
        mx = None
        for t_n, (j, bias) in enumerate(tiles_of(i, br)):
            s = _dot(kx_ref[br, j * tq:(j + 1) * tq, :], q)
            if bias is not None:
                s = s + jnp.concatenate([bias_ref[bias]] * NSA_GROUP, axis=1)
            s_ref[n % n_buf, t_n * tq:(t_n + 1) * tq, :] = s
            t = s[0:8]
            for c in range(1, tq // 8):
                t = jnp.maximum(t, s[c * 8:(c + 1) * 8])
            mx = t if mx is None else jnp.maximum(mx, t)
        return jnp.max(mx, axis=0, keepdims=True)

    def stage2(n, m):
        i, br = units[n]
        acc = None
        for t_n, (j, _) in enumerate(tiles_of(i, br)):
            p = jnp.exp(s_ref[n % n_buf, t_n * tq:(t_n + 1) * tq, :] - m).astype(BF16)
            pv = _dot(vt_ref[br, :, j * tq:(j + 1) * tq], p)
            acc = pv if acc is None else acc + pv
        cols = slice(i * tq, (i + 1) * tq)
        parts = []
        for r in range(NSA_GROUP):
            hl = slice(r * tq, (r + 1) * tq)
            g_row = 4 * (br + 1) + r
            scale = jax.nn.sigmoid(gate_ref[g_row:g_row + 1, cols]) / acc[NSA_HEAD_DIM:NSA_HEAD_DIM + 1, hl]
            parts.append(acc[0:NSA_HEAD_DIM, hl] * scale)
        return jnp.concatenate(parts, axis=0)

    ms = [stage1(0), stage1(1)]
    total = None
    for n in range(len(units)):
        if n + 2 < len(units):
            ms.append(stage1(n + 2))
        o_t = stage2(n, ms[n])
        i, br = units[n]
        cols = slice(i * tq, (i + 1) * tq)
        if br == 0:
            total = ocmp_ref[0, 0, :, cols].astype(F32) + o_t
        else:
            o_ref[cols, :] = (total + o_t).T.astype(o_ref.dtype)


def _sel_win_attn(qaug_t, kv, k_tabs, gates_t, ocmp_t, B, tq=256):
    kern = functools.partial(_sel_win_kernel, tq=tq)

    def kv_spec(col):
        return pl.BlockSpec((SEQ, LANES), lambda b, g: (b, col + g))

    return pl.pallas_call(
        kern,
        grid=(B, NSA_KV_HEADS),
        in_specs=[pl.BlockSpec((1, 1, NSA_GROUP, LANES, SEQ), lambda b, g: (b, g, 0, 0, 0)),
                  kv_spec(0), kv_spec(2),
                  pl.BlockSpec((2, SEQ, LANES), lambda b, g: (0, 0, 0)),
                  pl.BlockSpec((_GATE_ROWS, SEQ), lambda b, g: (g, b)),
                  pl.BlockSpec((1, 1, 256, SEQ), lambda b, g: (b, g, 0, 0))],
        out_specs=pl.BlockSpec((SEQ, 256), lambda b, g: (b, g)),
        out_shape=jax.ShapeDtypeStruct((B * SEQ, 512), BF16),
        scratch_shapes=[pltpu.VMEM((2, SEQ, LANES), BF16),
                        pltpu.VMEM((2, _NSA_VT_ROWS, SEQ), BF16),
                        pltpu.VMEM((3, SEQ, NSA_GROUP * tq), F32),
                        pltpu.VMEM((2, tq, tq), F32)],
        compiler_params=_cparams(("parallel", "parallel")),
        name="nsa_sel_win_attn",
    )(qaug_t, kv, kv, k_tabs, gates_t, ocmp_t)


_GLA_ROWS = 512


def _gla_kernel(qk_ref, v_ref, ga_ref, gz_ref, wg_ref, bg_ref, ng_ref, o_ref,
                b_ref, qd_ref, kd_ref):
    C = GLA_CHUNK
    KW = GLA_HEADS * GLA_DK
    wg = wg_ref[...]
    bg = bg_ref[...]
    norm_g = ng_ref[...]

    rin = lax.broadcasted_iota(jnp.int32, (_GLA_ROWS, KW), 0) % C
    lane = lax.broadcasted_iota(jnp.int32, (_GLA_ROWS, KW), 1)
    for blk in range(SEQ // _GLA_ROWS):
        rows = slice(blk * _GLA_ROWS, (blk + 1) * _GLA_ROWS)
        x = _dot(ga_ref[rows, :].astype(BF16), wg) + bg
        b = -(jnp.maximum(-x, 0.0) + jnp.log1p(jnp.exp(-jnp.abs(x)))) / GLA_TAU
        sh = 1
        while sh < C:
            b = b + jnp.where(rin >= sh, pltpu.roll(b, sh, 0), 0.0)
            sh *= 2
        b_ref[rows, :] = b
        q_d = qk_ref[rows, 0:KW] * (GLA_DK ** -0.5) * jnp.exp(b)
        for h in range(GLA_HEADS):
            in_head = (lane >= h * GLA_DK) & (lane < (h + 1) * GLA_DK)
            qd_ref[h, rows, :] = jnp.where(in_head, q_d, 0.0).astype(BF16)
        kd_ref[rows, :] = (qk_ref[rows, KW:2 * KW] * jnp.exp(-b)).astype(BF16)

    r_idx = lax.broadcasted_iota(jnp.int32, (GLA_HEADS * C, C), 0) % C
    c_idx = lax.broadcasted_iota(jnp.int32, (GLA_HEADS * C, C), 1)
    causal = r_idx >= c_idx

    def prep(n):
        rows = slice(n * C, (n + 1) * C)
        b = b_ref[rows, :]
        b_last = b_ref[n * C + C - 1:(n + 1) * C, :]
        k_u = (qk_ref[rows, KW:2 * KW] * jnp.exp(b_last - b)).astype(BF16)
        v = v_ref[rows, :]
        q_all = jnp.concatenate([qd_ref[h, rows, :] for h in range(GLA_HEADS)], axis=0)
        a = jnp.where(causal, _dot_nt(q_all, kd_ref[rows, :]), 0.0).astype(BF16)
        o_intra = [_dot(a[h * C:(h + 1) * C], v[:, h * GLA_DV:(h + 1) * GLA_DV])
                   for h in range(GLA_HEADS)]
        kv = lax.dot_general(k_u, v, (((0,), (0,)), ((), ())), preferred_element_type=F32)
        upd = jnp.concatenate([kv[h * GLA_DK:(h + 1) * GLA_DK, h * GLA_DV:(h + 1) * GLA_DV]
                               for h in range(GLA_HEADS)], axis=0)
        dec = jnp.exp(jnp.broadcast_to(b_last, (GLA_DV, KW)).T)
        return q_all, o_intra, upd, dec

    st = jnp.zeros((KW, GLA_DV), F32)
    n_chunks = SEQ // C
    nxt = prep(0)
    for n in range(n_chunks):
        q_all, o_intra, upd, dec = nxt
        if n + 1 < n_chunks:
            nxt = prep(n + 1)
        rows = slice(n * C, (n + 1) * C)
        o_inter = _dot(q_all, st.astype(BF16))
        for h in range(GLA_HEADS):
            vs = slice(h * GLA_DV, (h + 1) * GLA_DV)
            o = o_intra[h] + o_inter[h * C:(h + 1) * C]
            y = o * lax.rsqrt(jnp.mean(o * o, axis=-1, keepdims=True) + NORM_EPS) * norm_g
            y = y * gz_ref[rows, vs].astype(F32)
            o_ref[rows, vs] = y.astype(o_ref.dtype)
        st = st * dec + upd


def _gla(gqk, gv, ga, gz, wg, bg, norm_g, B):
    row = lambda b: (b, 0)
    const = lambda b: (0, 0)
    return pl.pallas_call(
        _gla_kernel,
        grid=(B,),
        in_specs=[pl.BlockSpec((SEQ, 512), row), pl.BlockSpec((SEQ, 512), row),
                  pl.BlockSpec((SEQ, LANES), row), pl.BlockSpec((SEQ, 512), row),
                  pl.BlockSpec((LANES, 256), const), pl.BlockSpec((1, 256), const),
                  pl.BlockSpec((1, GLA_DV), const)],
        out_specs=pl.BlockSpec((SEQ, 512), row),
        out_shape=jax.ShapeDtypeStruct((B * SEQ, 512), BF16),
        scratch_shapes=[pltpu.VMEM((SEQ, GLA_HEADS * GLA_DK), F32),
                        pltpu.VMEM((GLA_HEADS, SEQ, GLA_HEADS * GLA_DK), BF16),
                        pltpu.VMEM((SEQ, GLA_HEADS * GLA_DK), BF16)],
        compiler_params=_cparams(("parallel",)),
        name="gla_chunked",
    )(gqk, gv, ga, gz, wg, bg, norm_g)


def _deepnorm_ln(x, y, g, b):
    r = DEEPNORM_ALPHA * x + y
    mu = jnp.mean(r, axis=-1, keepdims=True)
    d = r - mu
    var = jnp.mean(d * d, axis=-1, keepdims=True)
    return d * lax.rsqrt(var + NORM_EPS) * g + b


def _even_out_kernel(onsa_ref, nz_ref, ogla_ref, x_ref, w_ref, g_ref, b_ref, o_ref):
    o_nsa = (onsa_ref[...].astype(F32) * nz_ref[...].astype(F32)).astype(BF16)
    y = _dot(o_nsa, w_ref[0:512, :]) + _dot(ogla_ref[...], w_ref[512:1024, :])
    o_ref[...] = _deepnorm_ln(x_ref[...], y, g_ref[...], b_ref[...])


def _even_out(onsa, nz, ogla, x2, w_bf, ln_g, ln_b, tm=512):
    T = x2.shape[0]
    row = lambda i: (i, 0)
    const = lambda i: (0, 0)
    return pl.pallas_call(
        _even_out_kernel,
        grid=(T // tm,),
        in_specs=[pl.BlockSpec((tm, 512), row), pl.BlockSpec((tm, 512), row),
                  pl.BlockSpec((tm, 512), row), pl.BlockSpec((tm, D_MODEL), row),
                  pl.BlockSpec((1024, D_MODEL), const),
                  pl.BlockSpec((1, D_MODEL), const), pl.BlockSpec((1, D_MODEL), const)],
        out_specs=pl.BlockSpec((tm, D_MODEL), row),
        out_shape=jax.ShapeDtypeStruct((T, D_MODEL), F32),
        compiler_params=_cparams(("parallel",)),
        name="even_out_proj_ln",
    )(onsa, nz, ogla, x2, w_bf, ln_g, ln_b)


def _odd_out_kernel(o_ref_in, z_ref, x_ref, w_ref, g_ref, b_ref, o_ref):
    o = (o_ref_in[...].astype(F32) * z_ref[...].astype(F32)).astype(BF16)
    y = _dot(o, w_ref[...])
    o_ref[...] = _deepnorm_ln(x_ref[...], y, g_ref[...], b_ref[...])


def _odd_out(o, z, x2, w_bf, ln_g, ln_b, tm=512):
    T = x2.shape[0]
    row = lambda i: (i, 0)
    const = lambda i: (0, 0)
    return pl.pallas_call(
        _odd_out_kernel,
        grid=(T // tm,),
        in_specs=[pl.BlockSpec((tm, D_MODEL), row), pl.BlockSpec((tm, D_MODEL), row),
                  pl.BlockSpec((tm, D_MODEL), row), pl.BlockSpec((D_MODEL, D_MODEL), const),
                  pl.BlockSpec((1, D_MODEL), const), pl.BlockSpec((1, D_MODEL), const)],
        out_specs=pl.BlockSpec((tm, D_MODEL), row),
        out_shape=jax.ShapeDtypeStruct((T, D_MODEL), F32),
        compiler_params=_cparams(("parallel",)),
        name="odd_out_proj_ln",
    )(o, z, x2, w_bf, ln_g, ln_b)


_HEAD_W = LANES
_QK_W = MLA_HEADS * _HEAD_W
_ROPE_HALF = MLA_ROPE // 2


def _odd_weights(w_in, w_uq, w_ukv):
    o = _O_OFF
    z_nope = jnp.zeros((D_MODEL, MLA_NOPE), F32)
    z_tail = jnp.zeros((D_MODEL, _HEAD_W - MLA_NOPE - MLA_ROPE), F32)
    w1 = jnp.concatenate([w_in[:, o[0]:o[2]], z_nope, w_in[:, o[2]:o[3]], z_tail,
                          w_in[:, o[3]:o[4]]], axis=1).astype(BF16)
    uq = w_uq.reshape(MLA_Q_LORA, MLA_HEADS, MLA_NOPE + MLA_ROPE)
    zq = jnp.zeros((MLA_Q_LORA, MLA_HEADS, _HEAD_W - MLA_NOPE - MLA_ROPE), F32)
    wq = jnp.concatenate([uq, zq], axis=-1).reshape(MLA_Q_LORA, _QK_W).astype(BF16)
    ukv = w_ukv.reshape(MLA_KV_LORA, MLA_HEADS, MLA_NOPE + MLA_V)
    wk = ukv[..., :MLA_NOPE].reshape(MLA_KV_LORA, MLA_HEADS * MLA_NOPE).astype(BF16)
    wv = ukv[..., MLA_NOPE:].reshape(MLA_KV_LORA, MLA_HEADS * MLA_V).astype(BF16)
    return w1, wq, wk, wv


def _rope_tables():
    freqs = jnp.exp(-math.log(ROPE_THETA) * jnp.arange(_ROPE_HALF, dtype=F32) * 2.0 / MLA_ROPE)
    ang = jnp.arange(SEQ, dtype=F32)[:, None] * freqs[None, :]
    cos, sin = jnp.cos(ang), jnp.sin(ang)
    z_half = jnp.zeros((SEQ, _ROPE_HALF), F32)
    z_tail = jnp.zeros((SEQ, _HEAD_W - MLA_NOPE - MLA_ROPE), F32)
    z_nope = jnp.zeros((SEQ, MLA_NOPE), F32)

    def tables(scale, nope_gain):
        nope = jnp.full((SEQ, MLA_NOPE), nope_gain, F32)
        c = jnp.concatenate([nope, cos * scale, cos * scale, z_tail], axis=1)
        s1 = jnp.concatenate([z_nope, -sin * scale, z_half, z_tail], axis=1)
        s2 = jnp.concatenate([z_nope, z_half, sin * scale, z_tail], axis=1)
        return c, s1, s2

    q_scale = (MLA_NOPE + MLA_ROPE) ** -0.5 * math.log2(math.e)
    return tables(q_scale, q_scale), tables(1.0, 0.0)


def _rope_block(x, c, s1, s2):
    return x * c + pltpu.roll(x, LANES - _ROPE_HALF, 1) * s1 + pltpu.roll(x, _ROPE_HALF, 1) * s2


def _rms(x, g):
    return x * lax.rsqrt(jnp.mean(x * x, axis=-1, keepdims=True) + NORM_EPS) * g


def _odd_in_kernel(x_ref, w1_ref, wq_ref, wk_ref, wv_ref, qn_ref, kn_ref,
                   qc_ref, qs1_ref, qs2_ref, kc_ref, ks1_ref, ks2_ref,
                   q_ref, k_ref, kr_ref, v_ref, z_ref):
    xb = x_ref[...].astype(BF16)
    c_q = _dot(xb, w1_ref[:, 0:MLA_Q_LORA])
    c_kv = _dot(xb, w1_ref[:, MLA_Q_LORA:640])
    kr = _dot(xb, w1_ref[:, 640:768])
    z_ref[...] = _silu(_dot(xb, w1_ref[:, 768:1792])).astype(BF16)
    cqn = _rms(c_q, qn_ref[...]).astype(BF16)
    ckvn = _rms(c_kv, kn_ref[...]).astype(BF16)
    qa = _dot(cqn, wq_ref[...])
    qc, qs1, qs2 = qc_ref[...], qs1_ref[...], qs2_ref[...]
    for h in range(MLA_HEADS):
        sl = slice(h * _HEAD_W, (h + 1) * _HEAD_W)
        q_ref[:, sl] = _rope_block(qa[:, sl], qc, qs1, qs2).astype(BF16)
    kr_ref[...] = _rope_block(kr, kc_ref[...], ks1_ref[...], ks2_ref[...]).astype(BF16)
    k_ref[...] = _dot(ckvn, wk_ref[...]).astype(BF16)
    v_ref[...] = _dot(ckvn, wv_ref[...]).astype(BF16)


def _odd_in_proj(x2, w1, wq, wk, wv, qn, kn, q_tabs, k_tabs, tm=512):
    T = x2.shape[0]
    s_tiles = SEQ // tm
    row = lambda i: (i, 0)
    const = lambda i: (0, 0)
    pos = lambda i: (i % s_tiles, 0)

    def full(a):
        return pl.BlockSpec(a.shape, const)

    tab = pl.BlockSpec((tm, LANES), pos)
    n_kv = MLA_HEADS * MLA_NOPE
    return pl.pallas_call(
        _odd_in_kernel,
        grid=(T // tm,),
        in_specs=[pl.BlockSpec((tm, D_MODEL), row), full(w1), full(wq), full(wk), full(wv),
                  full(qn), full(kn), tab, tab, tab, tab, tab, tab],
        out_specs=[pl.BlockSpec((tm, _QK_W), row), pl.BlockSpec((tm, n_kv), row),
                   pl.BlockSpec((tm, LANES), row), pl.BlockSpec((tm, n_kv), row),
                   pl.BlockSpec((tm, 1024), row)],
        out_shape=[jax.ShapeDtypeStruct((T, _QK_W), BF16), jax.ShapeDtypeStruct((T, n_kv), BF16),
                   jax.ShapeDtypeStruct((T, LANES), BF16), jax.ShapeDtypeStruct((T, n_kv), BF16),
                   jax.ShapeDtypeStruct((T, 1024), BF16)],
        compiler_params=_cparams(("parallel",)),
        name="odd_in_proj",
    )(x2, w1, wq, wk, wv, qn, kn, *q_tabs, *k_tabs)


_VT_ROWS = MLA_V + 16


def _mla_kernel(q_ref, kn_ref, kr_ref, v_ref, o_ref, kx_ref, qt_ref, vt_ref, s_ref, bias_ref, *, tq):
    nq = SEQ // tq
    a_idx = lax.broadcasted_iota(jnp.int32, (tq, tq), 0)
    b_idx = lax.broadcasted_iota(jnp.int32, (tq, tq), 1)
    bias_ref[...] = jnp.where(a_idx <= b_idx, 0.0, NEG_INF)
    lane = lax.broadcasted_iota(jnp.int32, (tq, LANES), 1)
    ones_rows = jnp.where(lax.broadcasted_iota(jnp.int32, (_VT_ROWS - MLA_V, tq), 0) == 0, 1.0, 0.0)

    for i in range(nq):
        rows = slice(i * tq, (i + 1) * tq)
        knp = kn_ref[rows, :].astype(F32)
        krb = kr_ref[rows, :].astype(F32)
        vt = v_ref[rows, :].astype(F32).T
        for h in range(2):
            kn_h = knp if h == 0 else pltpu.roll(knp, MLA_NOPE, 1)
            kx_ref[h, rows, :] = jnp.where(lane < MLA_NOPE, kn_h, krb).astype(BF16)
            qt_ref[h, :, rows] = q_ref[rows, h * _HEAD_W:(h + 1) * _HEAD_W].astype(F32).T.astype(BF16)
            vt_ref[h, :, rows] = jnp.concatenate(
                [vt[h * MLA_V:(h + 1) * MLA_V], ones_rows], axis=0).astype(BF16)

    units = [(i, h) for i in range(nq) for h in range(2)]
    n_buf = s_ref.shape[0]

    def stage1(n):
        i, h = units[n]
        qt = qt_ref[h, :, i * tq:(i + 1) * tq]
        mx = None
        for j in range(i + 1):
            cols = slice(j * tq, (j + 1) * tq)
            s = _dot(kx_ref[h, cols, :], qt)
            if j == i:
                s = s + bias_ref[...]
            s_ref[n % n_buf, cols, :] = s
            t = s[0:8]
            for c in range(1, tq // 8):
                t = jnp.maximum(t, s[c * 8:(c + 1) * 8])
            mx = t if mx is None else jnp.maximum(mx, t)
        return jnp.max(mx, axis=0, keepdims=True)

    def stage2(n, m):
        i, h = units[n]
        acc = None
        for j in range(i + 1):
            cols = slice(j * tq, (j + 1) * tq)
            p = jnp.exp2(s_ref[n % n_buf, cols, :] - m).astype(BF16)
            pv = _dot(vt_ref[h, :, cols], p)
            acc = pv if acc is None else acc + pv
        return acc[0:MLA_V] / acc[MLA_V:MLA_V + 1]

    ms = [stage1(0), stage1(1)]
    outs = []
    for n in range(len(units)):
        if n + 2 < len(units):
            ms.append(stage1(n + 2))
        outs.append(stage2(n, ms[n]))
        if len(outs) == 2:
            i = units[n][0]
            o_ref[i * tq:(i + 1) * tq, :] = jnp.concatenate(outs, axis=0).T.astype(o_ref.dtype)
            outs = []


def _mla_attn(q, kn, kr, v, B, tq=256):
    kern = functools.partial(_mla_kernel, tq=tq)
    pair = pl.BlockSpec((SEQ, LANES), lambda b, h: (b, h))
    return pl.pallas_call(
        kern,
        grid=(B, MLA_HEADS // 2),
        in_specs=[pl.BlockSpec((SEQ, 2 * _HEAD_W), lambda b, h: (b, h)), pair,
                  pl.BlockSpec((SEQ, LANES), lambda b, h: (b, 0)), pair],
        out_specs=pair,
        out_shape=jax.ShapeDtypeStruct((B * SEQ, MLA_HEADS * MLA_V), BF16),
        scratch_shapes=[pltpu.VMEM((2, SEQ, _HEAD_W), BF16), pltpu.VMEM((2, _HEAD_W, SEQ), BF16),
                        pltpu.VMEM((2, _VT_ROWS, SEQ), BF16), pltpu.VMEM((4, SEQ, tq), F32),
                        pltpu.VMEM((tq, tq), F32)],
        compiler_params=_cparams(("parallel", "parallel")),
        name="mla_attn",
    )(q, kn, kr, v)


def _even_layer(x2, B, w_in, cmp_pe, cmp_w1, cmp_w2, gla_w_gate, gla_b_gate, gla_norm,
                w_out, ln_g, ln_b):
    T = x2.shape[0]
    q, kvc, kv, ng, nz, gqk, gv, ga, gz = _even_in_proj(x2, _even_in_weight(w_in))

    pe_tab, w1_bd = _compress_weights(cmp_pe, cmp_w1)
    kc, vct = _compress(kvc, pe_tab, w1_bd, cmp_w2, B)

    ocmp_t, qaug_t = _cmp_attn(q, kc, vct, _overlap_t(), _q_alibi_table(), _head_select_mats(),
                               _cmp_bias_table(), ng, B)
    onsa = _sel_win_attn(qaug_t, kv, _nsa_k_tables(), ng, ocmp_t, B)

    wg = jnp.pad(gla_w_gate, ((0, LANES - GLA_GATE_RANK), (0, 0))).astype(BF16)
    ogla = _gla(gqk, gv, ga, gz, wg, gla_b_gate.reshape(1, -1), gla_norm.reshape(1, -1), B)

    return _even_out(onsa, nz, ogla, x2, w_out.astype(BF16), ln_g.reshape(1, -1), ln_b.reshape(1, -1))


def _odd_layer(x2, B, w_in, q_norm, w_uq, kv_norm, w_ukv, w_out, ln_g, ln_b):
    w1, wq, wk, wv = _odd_weights(w_in, w_uq, w_ukv)
    q_tabs, k_tabs = _rope_tables()
    q, kn, kr, v, z = _odd_in_proj(x2, w1, wq, wk, wv, q_norm.reshape(1, -1),
                                   kv_norm.reshape(1, -1), q_tabs, k_tabs)
    o = _mla_attn(q, kn, kr, v, B)
    return _odd_out(o, z, x2, w_out.astype(BF16), ln_g.reshape(1, -1), ln_b.reshape(1, -1))


def kernel(x, e_w_in, e_cmp_pe, e_cmp_w1, e_cmp_w2, e_gla_w_gate, e_gla_b_gate, e_gla_norm,
           e_w_out, e_ln_g, e_ln_b, o_w_in, o_q_norm, o_w_uq, o_kv_norm, o_w_ukv, o_w_out,
           o_ln_g, o_ln_b):
    B, S, D = x.shape
    x2 = x.reshape(B * S, D)
    for layer in range(DEPTH):
        i = layer // 2
        if layer % 2 == 0:
            x2 = _even_layer(x2, B, e_w_in[i], e_cmp_pe[i], e_cmp_w1[i], e_cmp_w2[i],
                             e_gla_w_gate[i], e_gla_b_gate[i], e_gla_norm[i], e_w_out[i],
                             e_ln_g[i], e_ln_b[i])
        else:
            x2 = _odd_layer(x2, B, o_w_in[i], o_q_norm[i], o_w_uq[i], o_kv_norm[i], o_w_ukv[i],
                            o_w_out[i], o_ln_g[i], o_ln_b[i])
    return x2.reshape(B, S, D)
```

```python
import functools
import math

import jax
import jax.numpy as jnp
import numpy as np
from jax import lax
from jax.experimental import pallas as pl
from jax.experimental.pallas import tpu as pltpu

F32 = jnp.float32
BF16 = jnp.bfloat16

D_MODEL = 1024
SEQ = 2048
DEPTH = 2
NSA_HEADS = 8
NSA_KV_HEADS = 2
NSA_GROUP = 4
NSA_HEAD_DIM = 64
CMP_BLOCK = 32
CMP_STRIDE = 16
CMP_HIDDEN = 128
N_CMP = (SEQ - CMP_BLOCK) // CMP_STRIDE + 1
N_CMP_PAD = 128
SEL_BLOCK = 64
N_SEL = SEQ // SEL_BLOCK
SEL_TOPN = 8
WINDOW = 512
FORCE_BONUS = 1000.0
GLA_HEADS = 4
GLA_DK = 64
GLA_DV = 128
GLA_GATE_RANK = 16
GLA_TAU = 16.0
GLA_CHUNK = 64
MLA_HEADS = 16
MLA_NOPE = 64
MLA_ROPE = 32
MLA_V = 64
MLA_Q_LORA = 384
MLA_KV_LORA = 256
ROPE_THETA = 10000.0
NORM_EPS = 1e-5
NEG_INF = -1e30
DEEPNORM_ALPHA = (2 * DEPTH) ** 0.25

LANES = 128
MASK_BIG = 2.0 ** 100
VMEM_LIMIT = 56 * 1024 * 1024

_E_OFF = np.cumsum([0, 512, 768, 24, 512, 256, 256, 512, 16, 512])
_O_OFF = np.cumsum([0, MLA_Q_LORA, MLA_KV_LORA, MLA_ROPE, MLA_HEADS * MLA_V])


def _cparams(sem):
    return pltpu.CompilerParams(dimension_semantics=sem, vmem_limit_bytes=VMEM_LIMIT)


def _silu(x):
    return x * jax.nn.sigmoid(x)


def _dot(a, b):
    return jnp.dot(a, b, preferred_element_type=F32)


def _dot_nt(a, b):
    return lax.dot_general(a, b, (((1,), (1,)), ((), ())), preferred_element_type=F32)


_A_SEGS = (("q", 512, BF16), ("kvc", 256, F32), ("kv", 512, BF16), ("ng", 128, F32),
           ("nz", 512, BF16), ("gqk", 512, F32), ("gv", 512, BF16), ("ga", 128, F32),
           ("gz", 512, BF16))
_A_SILU = ("nz", "gz")
_A_OFF = np.cumsum([0] + [s[1] for s in _A_SEGS])


def _even_in_weight(w):
    o = _E_OFF
    nkv = w[:, o[1]:o[2]]

    def kvcol(br, j, g):
        c = ((br * 2 + j) * NSA_KV_HEADS + g) * 64
        return nkv[:, c:c + 64]

    kv_blocks = [kvcol(br, j, g) for j in (0, 1) for g in (0, 1) for br in (1, 2)]
    gates = w[:, o[2]:o[3]].reshape(D_MODEL, 3, NSA_KV_HEADS, NSA_GROUP).transpose(0, 2, 1, 3)
    gates = jnp.pad(gates.reshape(D_MODEL, NSA_KV_HEADS, 12), ((0, 0), (0, 0), (0, 4)))
    ng = jnp.pad(gates.reshape(D_MODEL, 32), ((0, 0), (0, LANES - 32)))
    ga = jnp.pad(w[:, o[7]:o[8]], ((0, 0), (0, LANES - GLA_GATE_RANK)))
    cols = [w[:, o[0]:o[1]] * (NSA_HEAD_DIM ** -0.5), nkv[:, :256]] + kv_blocks + [
        ng, w[:, o[3]:o[4]], w[:, o[4]:o[6]], w[:, o[6]:o[7]], ga, w[:, o[8]:o[9]]]
    return jnp.concatenate(cols, axis=1).astype(BF16)


def _nsa_k_tables():
    pos = np.arange(SEQ)
    k_al = np.zeros((SEQ, LANES), np.float32)
    k_al[:, 96] = 1.0
    k_al[:, 97] = 1.0
    k_al[:, 98] = pos // SEL_BLOCK
    k_al[:, 99] = pos % SEL_BLOCK
    k_sel = k_al.copy()
    k_sel[pos, 64 + pos // SEL_BLOCK] = 1.0
    return jnp.asarray(np.stack([k_sel, k_al]))


def _even_in_kernel(x_ref, w_ref, wg_ref, *out_refs):
    xb = x_ref[...].astype(BF16)
    for (name, width, dt), off, o_ref in zip(_A_SEGS, _A_OFF[:-1], out_refs):
        if name == "ng":
            o_ref[...] = _dot_nt(wg_ref[...], xb)
            continue
        r = _dot(xb, w_ref[:, off:off + width])
        if name in _A_SILU:
            r = _silu(r)
        o_ref[...] = r.astype(dt)


def _even_in_proj(x2, w_bf, tm=512):
    T = x2.shape[0]
    n_w = w_bf.shape[1]
    ng_off = _A_OFF[[n for n, _, _ in _A_SEGS].index("ng")]
    wg_t = w_bf[:, ng_off:ng_off + LANES].T
    out_shape = [jax.ShapeDtypeStruct((wd, T) if name == "ng" else (T, wd), dt)
                 for name, wd, dt in _A_SEGS]
    out_specs = [pl.BlockSpec((wd, tm), lambda i: (0, i)) if name == "ng"
                 else pl.BlockSpec((tm, wd), lambda i: (i, 0)) for name, wd, _ in _A_SEGS]
    return pl.pallas_call(
        _even_in_kernel,
        grid=(T // tm,),
        in_specs=[pl.BlockSpec((tm, D_MODEL), lambda i: (i, 0)),
                  pl.BlockSpec((D_MODEL, n_w), lambda i: (0, 0)),
                  pl.BlockSpec((LANES, D_MODEL), lambda i: (0, 0))],
        out_specs=out_specs,
        out_shape=out_shape,
        compiler_params=_cparams(("parallel",)),
        name="even_in_proj",
    )(x2, w_bf, wg_t)


def _compress_weights(cmp_pe, cmp_w1):
    pe_tab = jnp.concatenate([cmp_pe[0], cmp_pe[0], cmp_pe[1], cmp_pe[1]], axis=1)
    w = cmp_w1.reshape(2, CMP_BLOCK, NSA_HEAD_DIM, CMP_HIDDEN)
    z = jnp.zeros_like(w)
    w_bd = jnp.concatenate([jnp.concatenate([w, z], axis=3), jnp.concatenate([z, w], axis=3)], axis=2)
    return pe_tab, w_bd.astype(BF16)


def _compress_kernel(xk_ref, xv_ref, pe_ref, w1_ref, w2_ref, w2t_ref, kc_ref, vct_ref):
    half = CMP_BLOCK // 2
    acc = [[None, None], [None, None]]
    for l in range(half):
        for j, x_ref in enumerate((xk_ref, xv_ref)):
            xs = x_ref[pl.ds(l, N_CMP_PAD, stride=CMP_STRIDE), :]
            for part in range(2):
                row = part * half + l
                xb = (xs + pe_ref[row:row + 1, j * LANES:(j + 1) * LANES]).astype(BF16)
                d = _dot(xb, w1_ref[j, row])
                acc[j][part] = d if acc[j][part] is None else acc[j][part] + d
    for j in range(2):
        h = acc[j][0] + pltpu.roll(acc[j][1], N_CMP_PAD - 1, 0)
        hs = _silu(h).astype(BF16)
        for g in range(NSA_KV_HEADS):
            hg = hs[:, g * CMP_HIDDEN:(g + 1) * CMP_HIDDEN]
            if j == 0:
                kc_ref[0, g] = _dot(hg, w2_ref[...])
            else:
                vct_ref[0, g] = _dot_nt(w2t_ref[...], hg)


def _compress(kvc, pe_tab, w1_bd, w2, B):
    w2k = w2[0].astype(BF16)
    w2vt = w2[1].T.astype(BF16)
    return pl.pallas_call(
        _compress_kernel,
        grid=(B,),
        in_specs=[pl.BlockSpec((SEQ, LANES), lambda b: (b, 0)),
                  pl.BlockSpec((SEQ, LANES), lambda b: (b, 1)),
                  pl.BlockSpec((CMP_BLOCK, 256), lambda b: (0, 0)),
                  pl.BlockSpec((2, CMP_BLOCK, LANES, 256), lambda b: (0, 0, 0, 0)),
                  pl.BlockSpec((CMP_HIDDEN, NSA_HEAD_DIM), lambda b: (0, 0)),
                  pl.BlockSpec((NSA_HEAD_DIM, CMP_HIDDEN), lambda b: (0, 0))],
        out_specs=[pl.BlockSpec((1, NSA_KV_HEADS, N_CMP_PAD, NSA_HEAD_DIM), lambda b: (b, 0, 0, 0)),
                   pl.BlockSpec((1, NSA_KV_HEADS, NSA_HEAD_DIM, N_CMP_PAD), lambda b: (b, 0, 0, 0))],
        out_shape=[jax.ShapeDtypeStruct((B, NSA_KV_HEADS, N_CMP_PAD, NSA_HEAD_DIM), F32),
                   jax.ShapeDtypeStruct((B, NSA_KV_HEADS, NSA_HEAD_DIM, N_CMP_PAD), F32)],
        compiler_params=_cparams(("parallel",)),
        name="nsa_compress",
    )(kvc, kvc, pe_tab, w1_bd, w2k, w2vt)


def _alibi_slopes_np():
    return np.exp2(-(8.0 / NSA_HEADS) * np.arange(1, NSA_HEADS + 1)).astype(np.float32)


def _overlap_t():
    cs = np.arange(N_CMP) * CMP_STRIDE
    ss = np.arange(N_SEL) * SEL_BLOCK
    ov = np.clip(np.minimum(cs[:, None] + CMP_BLOCK, ss[None, :] + SEL_BLOCK)
                 - np.maximum(cs[:, None], ss[None, :]), 0, None).astype(np.float32) / CMP_BLOCK
    ovt = np.zeros((N_SEL, N_CMP_PAD), np.float32)
    ovt[:, :N_CMP] = ov.T
    return jnp.asarray(ovt)


def _q_alibi_table():
    slopes = _alibi_slopes_np().reshape(NSA_KV_HEADS, NSA_GROUP)
    pos = np.arange(SEQ)
    tab = np.zeros((NSA_KV_HEADS, SEQ, LANES), np.float32)
    for g in range(NSA_KV_HEADS):
        for r in range(NSA_GROUP):
            m = slopes[g, r]
            tab[g, :, r * 32 + 0] = -m * SEL_BLOCK * (pos // SEL_BLOCK)
            tab[g, :, r * 32 + 1] = -m * (pos % SEL_BLOCK)
            tab[g, :, r * 32 + 2] = m * SEL_BLOCK
            tab[g, :, r * 32 + 3] = m
    return jnp.asarray(tab)


def _head_select_mats():
    m = np.zeros((NSA_GROUP, NSA_HEAD_DIM, 256), np.float32)
    for r in range(NSA_GROUP):
        m[r, np.arange(64), r * 64 + np.arange(64)] = 1.0
    return jnp.asarray(m, dtype=BF16)


_GATE_ROWS = 16


def _cmp_bias_table():
    slopes = jnp.asarray(_alibi_slopes_np()).reshape(NSA_KV_HEADS, NSA_GROUP, 1, 1)
    n = jnp.arange(N_CMP_PAD)[:, None]
    t = jnp.arange(SEQ)[None, :]
    dist = t - (n * CMP_STRIDE + (CMP_BLOCK - 1))
    visible = (dist >= 0) & (n < N_CMP)
    return jnp.where(visible, -slopes * dist.astype(F32), NEG_INF)


def _cmp_attn_kernel(q_ref, kc_ref, vct_ref, ovt_ref, qal_ref, hs_ref, bias_ref, gate_ref,
                     ocmp_ref, qaug_ref, *, tq):
    kc = kc_ref[0, 0].astype(BF16)
    vct = vct_ref[0, 0].astype(BF16)
    k_ext = [_dot(kc, hs_ref[r]).astype(BF16) for r in range(NSA_GROUP)]
    ri = lax.broadcasted_iota(jnp.int32, (256, 256), 0)
    ci = lax.broadcasted_iota(jnp.int32, (256, 256), 1)
    eye = jnp.where(ri == ci, 1.0, 0.0).astype(BF16)
    sub8 = lax.broadcasted_iota(jnp.int32, (8, tq), 0)
    j_idx = lax.broadcasted_iota(jnp.int32, (N_SEL, tq), 0)
    t_lane = lax.broadcasted_iota(jnp.int32, (N_SEL, tq), 1)
    n_grp = N_SEL // 8
    for i in range(SEQ // tq):
        cols = slice(i * tq, (i + 1) * tq)
        nv = min(N_CMP_PAD, (i + 1) * tq // CMP_STRIDE)
        q = q_ref[cols, :]
        lhs = jnp.concatenate([k[:nv] for k in k_ext] + [eye], axis=0)
        res = _dot_nt(lhs, q)
        q_t = res[NSA_GROUP * nv:]
        p_sum = None
        o_parts = []
        for r in range(NSA_GROUP):
            s = res[r * nv:(r + 1) * nv] + bias_ref[0, r, 0:nv, cols]
            m = jnp.max(s, axis=0, keepdims=True)
            e = jnp.exp(s - m)
            if i == 0:
                e = jnp.where(bias_ref[0, r, 0:nv, cols] > 0.5 * NEG_INF, e, 0.0)
            l = jnp.sum(e, axis=0, keepdims=True)
            p = e / jnp.where(l > 0.0, l, 1.0)
            p_sum = p if p_sum is None else p_sum + p
            gate = jax.nn.sigmoid(gate_ref[r:r + 1, cols])
            o_parts.append(_dot(vct[:, :nv], p.astype(BF16)) * gate)
        ocmp_ref[0, 0, :, cols] = jnp.concatenate(o_parts, axis=0).astype(ocmp_ref.dtype)

        imp = jnp.dot(ovt_ref[:, 0:nv], p_sum, preferred_element_type=F32,
                      precision=lax.Precision.HIGHEST)
        cur = (i * tq + t_lane) // SEL_BLOCK
        forced = (j_idx == 0) | (j_idx == cur) | (j_idx == cur - 1)
        imp = jnp.where(j_idx > cur, -1.0, imp + jnp.where(forced, FORCE_BONUS, 0.0))
        grp = [imp[8 * a:8 * (a + 1)] for a in range(n_grp)]
        rank = [jnp.zeros((8, tq), F32) for _ in range(n_grp)]
        for k in range(N_SEL):
            row = jnp.broadcast_to(imp[k:k + 1, :], (8, tq))
            for a in range(n_grp):
                ge = jnp.where(row >= grp[a], 1.0, 0.0)
                gt = jnp.where(row > grp[a], 1.0, 0.0)
                if 8 * a > k:
                    cnt = ge
                elif 8 * a + 7 <= k:
                    cnt = gt
                else:
                    cnt = jnp.where(sub8 > k - 8 * a, ge, gt)
                rank[a] = rank[a] + cnt
        sel_bias = jnp.where(jnp.concatenate(rank, axis=0) < float(SEL_TOPN), 0.0, -MASK_BIG)

        qal_t = qal_ref[0, cols, :].T
        for r in range(NSA_GROUP):
            aug_t = jnp.concatenate([q_t[r * 64:(r + 1) * 64], sel_bias,
                                     qal_t[r * 32:(r + 1) * 32, :]], axis=0)
            qaug_ref[0, 0, r, :, cols] = aug_t.astype(BF16)


def _cmp_attn(q, kc, vct, ovt, qal, hs, bias, gates_t, B, tq=256):
    kern = functools.partial(_cmp_attn_kernel, tq=tq)
    return pl.pallas_call(
        kern,
        grid=(NSA_KV_HEADS, B),
        in_specs=[pl.BlockSpec((SEQ, 256), lambda g, b: (b, g)),
                  pl.BlockSpec((1, 1, N_CMP_PAD, NSA_HEAD_DIM), lambda g, b: (b, g, 0, 0)),
                  pl.BlockSpec((1, 1, NSA_HEAD_DIM, N_CMP_PAD), lambda g, b: (b, g, 0, 0)),
                  pl.BlockSpec((N_SEL, N_CMP_PAD), lambda g, b: (0, 0)),
                  pl.BlockSpec((1, SEQ, LANES), lambda g, b: (g, 0, 0)),
                  pl.BlockSpec((NSA_GROUP, NSA_HEAD_DIM, 256), lambda g, b: (0, 0, 0)),
                  pl.BlockSpec((1, NSA_GROUP, N_CMP_PAD, SEQ), lambda g, b: (g, 0, 0, 0)),
                  pl.BlockSpec((_GATE_ROWS, SEQ), lambda g, b: (g, b))],
        out_specs=[pl.BlockSpec((1, 1, 256, SEQ), lambda g, b: (b, g, 0, 0)),
                   pl.BlockSpec((1, 1, NSA_GROUP, LANES, SEQ), lambda g, b: (b, g, 0, 0, 0))],
        out_shape=[jax.ShapeDtypeStruct((B, NSA_KV_HEADS, 256, SEQ), BF16),
                   jax.ShapeDtypeStruct((B, NSA_KV_HEADS, NSA_GROUP, LANES, SEQ), BF16)],
        compiler_params=_cparams(("parallel", "parallel")),
        name="nsa_cmp_attn_select",
    )(q, kc, vct, ovt, qal, hs, bias, gates_t)


_NSA_VT_ROWS = NSA_HEAD_DIM + 16


def _sel_win_kernel(qaug_ref, k_ref, v_ref, ktab_ref, gate_ref, ocmp_ref,
                    o_ref, kx_ref, vt_ref, s_ref, bias_ref, *, tq):
    nq = SEQ // tq
    n_back = WINDOW // tq
    a_idx = lax.broadcasted_iota(jnp.int32, (tq, tq), 0)
    b_idx = lax.broadcasted_iota(jnp.int32, (tq, tq), 1)
    bias_ref[0] = jnp.where(a_idx <= b_idx, 0.0, -MASK_BIG)
    bias_ref[1] = jnp.where(a_idx > b_idx, 0.0, -MASK_BIG)
    lane = lax.broadcasted_iota(jnp.int32, (tq, LANES), 1)
    ones_rows = jnp.where(
        lax.broadcasted_iota(jnp.int32, (_NSA_VT_ROWS - NSA_HEAD_DIM, tq), 0) == 0, 1.0, 0.0)
    for i in range(nq):
        rows = slice(i * tq, (i + 1) * tq)
        kp = k_ref[rows, :].astype(F32)
        vt = v_ref[rows, :].astype(F32).T
        for br in range(2):
            k_br = kp if br == 0 else pltpu.roll(kp, NSA_HEAD_DIM, 1)
            kx_ref[br, rows, :] = jnp.where(lane < NSA_HEAD_DIM, k_br, ktab_ref[br, rows, :]).astype(BF16)
            vt_ref[br, :, rows] = jnp.concatenate(
                [vt[br * NSA_HEAD_DIM:(br + 1) * NSA_HEAD_DIM], ones_rows], axis=0).astype(BF16)

    def tiles_of(i, br):
        if br == 0:
            return [(j, None) for j in range(i)] + [(i, 0)]
        tl = [(i - d, 1 if d == n_back else None) for d in range(n_back, 0, -1) if i >= d]
        return tl + [(i, 0)]

    units = [(i, br) for i in range(nq) for br in range(2)]
    n_buf = s_ref.shape[0]

    def stage1(n):
        i, br = units[n]
        q = jnp.concatenate([qaug_ref[0, 0, r, :, i * tq:(i + 1) * tq] for r in range(NSA_GROUP)],
                            axis=1)
        mx = None
        for t_n, (j, bias) in enumerate(tiles_of(i, br)):
            s = _dot(kx_ref[br, j * tq:(j + 1) * tq, :], q)
            if bias is not None:
                s = s + jnp.concatenate([bias_ref[bias]] * NSA_GROUP, axis=1)
            s_ref[n % n_buf, t_n * tq:(t_n + 1) * tq, :] = s
            t = s[0:8]
            for c in range(1, tq // 8):
                t = jnp.maximum(t, s[c * 8:(c + 1) * 8])
            mx = t if mx is None else jnp.maximum(mx, t)
        return jnp.max(mx, axis=0, keepdims=True)

    def stage2(n, m):
        i, br = units[n]
        acc = None
        for t_n, (j, _) in enumerate(tiles_of(i, br)):
            p = jnp.exp(s_ref[n % n_buf, t_n * tq:(t_n + 1) * tq, :] - m).astype(BF16)
            pv = _dot(vt_ref[br, :, j * tq:(j + 1) * tq], p)
            acc = pv if acc is None else acc + pv
        cols = slice(i * tq, (i + 1) * tq)
        parts = []
        for r in range(NSA_GROUP):
            hl = slice(r * tq, (r + 1) * tq)
            g_row = 4 * (br + 1) + r
            scale = jax.nn.sigmoid(gate_ref[g_row:g_row + 1, cols]) / acc[NSA_HEAD_DIM:NSA_HEAD_DIM + 1, hl]
            parts.append(acc[0:NSA_HEAD_DIM, hl] * scale)
        return jnp.concatenate(parts, axis=0)

    ms = [stage1(0), stage1(1)]
    total = None
    for n in range(len(units)):
        if n + 2 < len(units):
            ms.append(stage1(n + 2))
        o_t = stage2(n, ms[n])
        i, br = units[n]
        cols = slice(i * tq, (i + 1) * tq)
        if br == 0:
            total = ocmp_ref[0, 0, :, cols].astype(F32) + o_t
        else:
            o_ref[cols, :] = (total + o_t).T.astype(o_ref.dtype)


def _sel_win_attn(qaug_t, kv, k_tabs, gates_t, ocmp_t, B, tq=256):
    kern = functools.partial(_sel_win_kernel, tq=tq)

    def kv_spec(col):
        return pl.BlockSpec((SEQ, LANES), lambda b, g: (b, col + g))

    return pl.pallas_call(
        kern,
        grid=(B, NSA_KV_HEADS),
        in_specs=[pl.BlockSpec((1, 1, NSA_GROUP, LANES, SEQ), lambda b, g: (b, g, 0, 0, 0)),
                  kv_spec(0), kv_spec(2),
                  pl.BlockSpec((2, SEQ, LANES), lambda b, g: (0, 0, 0)),
                  pl.BlockSpec((_GATE_ROWS, SEQ), lambda b, g: (g, b)),
                  pl.BlockSpec((1, 1, 256, SEQ), lambda b, g: (b, g, 0, 0))],
        out_specs=pl.BlockSpec((SEQ, 256), lambda b, g: (b, g)),
        out_shape=jax.ShapeDtypeStruct((B * SEQ, 512), BF16),
        scratch_shapes=[pltpu.VMEM((2, SEQ, LANES), BF16),
                        pltpu.VMEM((2, _NSA_VT_ROWS, SEQ), BF16),
                        pltpu.VMEM((3, SEQ, NSA_GROUP * tq), F32),
                        pltpu.VMEM((2, tq, tq), F32)],
        compiler_params=_cparams(("parallel", "parallel")),
        name="nsa_sel_win_attn",
    )(qaug_t, kv, kv, k_tabs, gates_t, ocmp_t)


_GLA_ROWS = 512


def _gla_kernel(qk_ref, v_ref, ga_ref, gz_ref, wg_ref, bg_ref, ng_ref, o_ref,
                b_ref, qd_ref, kd_ref):
    C = GLA_CHUNK
    KW = GLA_HEADS * GLA_DK
    wg = wg_ref[...]
    bg = bg_ref[...]
    norm_g = ng_ref[...]

    rin = lax.broadcasted_iota(jnp.int32, (_GLA_ROWS, KW), 0) % C
    lane = lax.broadcasted_iota(jnp.int32, (_GLA_ROWS, KW), 1)
    for blk in range(SEQ // _GLA_ROWS):
        rows = slice(blk * _GLA_ROWS, (blk + 1) * _GLA_ROWS)
        x = _dot(ga_ref[rows, :].astype(BF16), wg) + bg
        b = -(jnp.maximum(-x, 0.0) + jnp.log1p(jnp.exp(-jnp.abs(x)))) / GLA_TAU
        sh = 1
        while sh < C:
            b = b + jnp.where(rin >= sh, pltpu.roll(b, sh, 0), 0.0)
            sh *= 2
        b_ref[rows, :] = b
        q_d = qk_ref[rows, 0:KW] * (GLA_DK ** -0.5) * jnp.exp(b)
        for h in range(GLA_HEADS):
            in_head = (lane >= h * GLA_DK) & (lane < (h + 1) * GLA_DK)
            qd_ref[h, rows, :] = jnp.where(in_head, q_d, 0.0).astype(BF16)
        kd_ref[rows, :] = (qk_ref[rows, KW:2 * KW] * jnp.exp(-b)).astype(BF16)

    r_idx = lax.broadcasted_iota(jnp.int32, (GLA_HEADS * C, C), 0) % C
    c_idx = lax.broadcasted_iota(jnp.int32, (GLA_HEADS * C, C), 1)
    causal = r_idx >= c_idx

    def prep(n):
        rows = slice(n * C, (n + 1) * C)
        b = b_ref[rows, :]
        b_last = b_ref[n * C + C - 1:(n + 1) * C, :]
        k_u = (qk_ref[rows, KW:2 * KW] * jnp.exp(b_last - b)).astype(BF16)
        v = v_ref[rows, :]
        q_all = jnp.concatenate([qd_ref[h, rows, :] for h in range(GLA_HEADS)], axis=0)
        a = jnp.where(causal, _dot_nt(q_all, kd_ref[rows, :]), 0.0).astype(BF16)
        o_intra = [_dot(a[h * C:(h + 1) * C], v[:, h * GLA_DV:(h + 1) * GLA_DV])
                   for h in range(GLA_HEADS)]
        kv = lax.dot_general(k_u, v, (((0,), (0,)), ((), ())), preferred_element_type=F32)
        upd = jnp.concatenate([kv[h * GLA_DK:(h + 1) * GLA_DK, h * GLA_DV:(h + 1) * GLA_DV]
                               for h in range(GLA_HEADS)], axis=0)
        dec = jnp.exp(jnp.broadcast_to(b_last, (GLA_DV, KW)).T)
        return q_all, o_intra, upd, dec

    st = jnp.zeros((KW, GLA_DV), F32)
    n_chunks = SEQ // C
    nxt = prep(0)
    for n in range(n_chunks):
        q_all, o_intra, upd, dec = nxt
        if n + 1 < n_chunks:
            nxt = prep(n + 1)
        rows = slice(n * C, (n + 1) * C)
        o_inter = _dot(q_all, st.astype(BF16))
        for h in range(GLA_HEADS):
            vs = slice(h * GLA_DV, (h + 1) * GLA_DV)
            o = o_intra[h] + o_inter[h * C:(h + 1) * C]
            y = o * lax.rsqrt(jnp.mean(o * o, axis=-1, keepdims=True) + NORM_EPS) * norm_g
            y = y * gz_ref[rows, vs].astype(F32)
            o_ref[rows, vs] = y.astype(o_ref.dtype)
        st = st * dec + upd


def _gla(gqk, gv, ga, gz, wg, bg, norm_g, B):
    row = lambda b: (b, 0)
    const = lambda b: (0, 0)
    return pl.pallas_call(
        _gla_kernel,
        grid=(B,),
        in_specs=[pl.BlockSpec((SEQ, 512), row), pl.BlockSpec((SEQ, 512), row),
                  pl.BlockSpec((SEQ, LANES), row), pl.BlockSpec((SEQ, 512), row),
                  pl.BlockSpec((LANES, 256), const), pl.BlockSpec((1, 256), const),
                  pl.BlockSpec((1, GLA_DV), const)],
        out_specs=pl.BlockSpec((SEQ, 512), row),
        out_shape=jax.ShapeDtypeStruct((B * SEQ, 512), BF16),
        scratch_shapes=[pltpu.VMEM((SEQ, GLA_HEADS * GLA_DK), F32),
                        pltpu.VMEM((GLA_HEADS, SEQ, GLA_HEADS * GLA_DK), BF16),
                        pltpu.VMEM((SEQ, GLA_HEADS * GLA_DK), BF16)],
        compiler_params=_cparams(("parallel",)),
        name="gla_chunked",
    )(gqk, gv, ga, gz, wg, bg, norm_g)


def _deepnorm_ln(x, y, g, b):
    r = DEEPNORM_ALPHA * x + y
    mu = jnp.mean(r, axis=-1, keepdims=True)
    d = r - mu
    var = jnp.mean(d * d, axis=-1, keepdims=True)
    return d * lax.rsqrt(var + NORM_EPS) * g + b


def _odd_out_kernel(o_ref_in, z_ref, x_ref, w_ref, g_ref, b_ref, o_ref):
    o = (o_ref_in[...].astype(F32) * z_ref[...].astype(F32)).astype(BF16)
    y = _dot(o, w_ref[...])
    o_ref[...] = _deepnorm_ln(x_ref[...], y, g_ref[...], b_ref[...])


def _odd_out(o, z, x2, w_bf, ln_g, ln_b, tm=512):
    T = x2.shape[0]
    row = lambda i: (i, 0)
    const = lambda i: (0, 0)
    return pl.pallas_call(
        _odd_out_kernel,
        grid=(T // tm,),
        in_specs=[pl.BlockSpec((tm, D_MODEL), row), pl.BlockSpec((tm, D_MODEL), row),
                  pl.BlockSpec((tm, D_MODEL), row), pl.BlockSpec((D_MODEL, D_MODEL), const),
                  pl.BlockSpec((1, D_MODEL), const), pl.BlockSpec((1, D_MODEL), const)],
        out_specs=pl.BlockSpec((tm, D_MODEL), row),
        out_shape=jax.ShapeDtypeStruct((T, D_MODEL), F32),
        compiler_params=_cparams(("parallel",)),
        name="odd_out_proj_ln",
    )(o, z, x2, w_bf, ln_g, ln_b)


_HEAD_W = LANES
_QK_W = MLA_HEADS * _HEAD_W
_ROPE_HALF = MLA_ROPE // 2


def _odd_weights(w_in, w_uq, w_ukv):
    o = _O_OFF
    z_nope = jnp.zeros((D_MODEL, MLA_NOPE), F32)
    z_tail = jnp.zeros((D_MODEL, _HEAD_W - MLA_NOPE - MLA_ROPE), F32)
    w1 = jnp.concatenate([w_in[:, o[0]:o[2]], z_nope, w_in[:, o[2]:o[3]], z_tail,
                          w_in[:, o[3]:o[4]]], axis=1).astype(BF16)
    uq = w_uq.reshape(MLA_Q_LORA, MLA_HEADS, MLA_NOPE + MLA_ROPE)
    zq = jnp.zeros((MLA_Q_LORA, MLA_HEADS, _HEAD_W - MLA_NOPE - MLA_ROPE), F32)
    wq = jnp.concatenate([uq, zq], axis=-1).reshape(MLA_Q_LORA, _QK_W).astype(BF16)
    ukv = w_ukv.reshape(MLA_KV_LORA, MLA_HEADS, MLA_NOPE + MLA_V)
    wk = ukv[..., :MLA_NOPE].reshape(MLA_KV_LORA, MLA_HEADS * MLA_NOPE).astype(BF16)
    wv = ukv[..., MLA_NOPE:].reshape(MLA_KV_LORA, MLA_HEADS * MLA_V).astype(BF16)
    return w1, wq, wk, wv


def _rope_tables():
    freqs = jnp.exp(-math.log(ROPE_THETA) * jnp.arange(_ROPE_HALF, dtype=F32) * 2.0 / MLA_ROPE)
    ang = jnp.arange(SEQ, dtype=F32)[:, None] * freqs[None, :]
    cos, sin = jnp.cos(ang), jnp.sin(ang)
    z_half = jnp.zeros((SEQ, _ROPE_HALF), F32)
    z_tail = jnp.zeros((SEQ, _HEAD_W - MLA_NOPE - MLA_ROPE), F32)
    z_nope = jnp.zeros((SEQ, MLA_NOPE), F32)

    def tables(scale, nope_gain):
        nope = jnp.full((SEQ, MLA_NOPE), nope_gain, F32)
        c = jnp.concatenate([nope, cos * scale, cos * scale, z_tail], axis=1)
        s1 = jnp.concatenate([z_nope, -sin * scale, z_half, z_tail], axis=1)
        s2 = jnp.concatenate([z_nope, z_half, sin * scale, z_tail], axis=1)
        return c, s1, s2

    q_scale = (MLA_NOPE + MLA_ROPE) ** -0.5 * math.log2(math.e)
    return tables(q_scale, q_scale), tables(1.0, 0.0)


def _rope_block(x, c, s1, s2):
    return x * c + pltpu.roll(x, LANES - _ROPE_HALF, 1) * s1 + pltpu.roll(x, _ROPE_HALF, 1) * s2


def _rms(x, g):
    return x * lax.rsqrt(jnp.mean(x * x, axis=-1, keepdims=True) + NORM_EPS) * g


def _even_out_odd_in_kernel(onsa_ref, nz_ref, ogla_ref, x_ref, wo_ref, g_ref, b_ref,
                            w1_ref, wq_ref, wk_ref, wv_ref, qn_ref, kn_ref,
                            qc_ref, qs1_ref, qs2_ref, kc_ref, ks1_ref, ks2_ref,
                            x1_ref, q_ref, k_ref, kr_ref, v_ref, z_ref):
    o_nsa = (onsa_ref[...].astype(F32) * nz_ref[...].astype(F32)).astype(BF16)
    y = _dot(o_nsa, wo_ref[0:512, :]) + _dot(ogla_ref[...], wo_ref[512:1024, :])
    x1 = _deepnorm_ln(x_ref[...], y, g_ref[...], b_ref[...])
    x1_ref[...] = x1

    xb = x1.astype(BF16)
    c_q = _dot(xb, w1_ref[:, 0:MLA_Q_LORA])
    c_kv = _dot(xb, w1_ref[:, MLA_Q_LORA:640])
    kr = _dot(xb, w1_ref[:, 640:768])
    cqn = _rms(c_q, qn_ref[...]).astype(BF16)
    ckvn = _rms(c_kv, kn_ref[...]).astype(BF16)
    kr_ref[...] = _rope_block(kr, kc_ref[...], ks1_ref[...], ks2_ref[...]).astype(BF16)
    k_ref[...] = _dot(ckvn, wk_ref[...]).astype(BF16)
    v_ref[...] = _dot(ckvn, wv_ref[...]).astype(BF16)
    qc, qs1, qs2 = qc_ref[...], qs1_ref[...], qs2_ref[...]
    grp = 4
    for g0 in range(0, MLA_HEADS, grp):
        qa = _dot(cqn, wq_ref[:, g0 * _HEAD_W:(g0 + grp) * _HEAD_W])
        for h in range(grp):
            sl = slice(h * _HEAD_W, (h + 1) * _HEAD_W)
            q_ref[:, (g0 + h) * _HEAD_W:(g0 + h + 1) * _HEAD_W] = (
                _rope_block(qa[:, sl], qc, qs1, qs2).astype(BF16))
        zc = slice(g0 // grp * 256, (g0 // grp + 1) * 256)
        z_ref[:, zc] = _silu(_dot(xb, w1_ref[:, 768 + zc.start:768 + zc.stop])).astype(BF16)


def _even_out_odd_in(onsa, nz, ogla, x2, wo, ln_g, ln_b, w1, wq, wk, wv, qn, kn, q_tabs, k_tabs,
                     tm=512):
    T = x2.shape[0]
    s_tiles = SEQ // tm
    row = lambda i: (i, 0)
    const = lambda i: (0, 0)
    pos = lambda i: (i % s_tiles, 0)

    def full(a):
        return pl.BlockSpec(a.shape, const)

    tab = pl.BlockSpec((tm, LANES), pos)
    n_kv = MLA_HEADS * MLA_NOPE
    return pl.pallas_call(
        _even_out_odd_in_kernel,
        grid=(T // tm,),
        in_specs=[pl.BlockSpec((tm, 512), row), pl.BlockSpec((tm, 512), row),
                  pl.BlockSpec((tm, 512), row), pl.BlockSpec((tm, D_MODEL), row),
                  full(wo), full(ln_g), full(ln_b),
                  full(w1), full(wq), full(wk), full(wv), full(qn), full(kn),
                  tab, tab, tab, tab, tab, tab],
        out_specs=[pl.BlockSpec((tm, D_MODEL), row),
                   pl.BlockSpec((tm, _QK_W), row), pl.BlockSpec((tm, n_kv), row),
                   pl.BlockSpec((tm, LANES), row), pl.BlockSpec((tm, n_kv), row),
                   pl.BlockSpec((tm, 1024), row)],
        out_shape=[jax.ShapeDtypeStruct((T, D_MODEL), F32),
                   jax.ShapeDtypeStruct((T, _QK_W), BF16), jax.ShapeDtypeStruct((T, n_kv), BF16),
                   jax.ShapeDtypeStruct((T, LANES), BF16), jax.ShapeDtypeStruct((T, n_kv), BF16),
                   jax.ShapeDtypeStruct((T, 1024), BF16)],
        compiler_params=_cparams(("parallel",)),
        name="even_out_odd_in_proj",
    )(onsa, nz, ogla, x2, wo, ln_g, ln_b, w1, wq, wk, wv, qn, kn, *q_tabs, *k_tabs)


_VT_ROWS = MLA_V + 16


def _mla_kernel(q_ref, kn_ref, kr_ref, v_ref, o_ref, kx_ref, qt_ref, vt_ref, s_ref, bias_ref, *, tq):
    nq = SEQ // tq
    a_idx = lax.broadcasted_iota(jnp.int32, (tq, tq), 0)
    b_idx = lax.broadcasted_iota(jnp.int32, (tq, tq), 1)
    bias_ref[...] = jnp.where(a_idx <= b_idx, 0.0, NEG_INF)
    lane = lax.broadcasted_iota(jnp.int32, (tq, LANES), 1)
    ones_rows = jnp.where(lax.broadcasted_iota(jnp.int32, (_VT_ROWS - MLA_V, tq), 0) == 0, 1.0, 0.0)

    for i in range(nq):
        rows = slice(i * tq, (i + 1) * tq)
        knp = kn_ref[rows, :].astype(F32)
        krb = kr_ref[rows, :].astype(F32)
        vt = v_ref[rows, :].astype(F32).T
        for h in range(2):
            kn_h = knp if h == 0 else pltpu.roll(knp, MLA_NOPE, 1)
            kx_ref[h, rows, :] = jnp.where(lane < MLA_NOPE, kn_h, krb).astype(BF16)
            qt_ref[h, :, rows] = q_ref[rows, h * _HEAD_W:(h + 1) * _HEAD_W].astype(F32).T.astype(BF16)
            vt_ref[h, :, rows] = jnp.concatenate(
                [vt[h * MLA_V:(h + 1) * MLA_V], ones_rows], axis=0).astype(BF16)

    units = [(i, h) for i in range(nq) for h in range(2)]
    n_buf = s_ref.shape[0]

    def stage1(n):
        i, h = units[n]
        qt = qt_ref[h, :, i * tq:(i + 1) * tq]
        mx = None
        for j in range(i + 1):
            cols = slice(j * tq, (j + 1) * tq)
            s = _dot(kx_ref[h, cols, :], qt)
            if j == i:
                s = s + bias_ref[...]
            s_ref[n % n_buf, cols, :] = s
            t = s[0:8]
            for c in range(1, tq // 8):
                t = jnp.maximum(t, s[c * 8:(c + 1) * 8])
            mx = t if mx is None else jnp.maximum(mx, t)
        return jnp.max(mx, axis=0, keepdims=True)

    def stage2(n, m):
        i, h = units[n]
        acc = None
        for j in range(i + 1):
            cols = slice(j * tq, (j + 1) * tq)
            p = jnp.exp2(s_ref[n % n_buf, cols, :] - m).astype(BF16)
            pv = _dot(vt_ref[h, :, cols], p)
            acc = pv if acc is None else acc + pv
        return acc[0:MLA_V] / acc[MLA_V:MLA_V + 1]

    ms = [stage1(0), stage1(1)]
    outs = []
    for n in range(len(units)):
        if n + 2 < len(units):
            ms.append(stage1(n + 2))
        outs.append(stage2(n, ms[n]))
        if len(outs) == 2:
            i = units[n][0]
            o_ref[i * tq:(i + 1) * tq, :] = jnp.concatenate(outs, axis=0).T.astype(o_ref.dtype)
            outs = []


def _mla_attn(q, kn, kr, v, B, tq=256):
    kern = functools.partial(_mla_kernel, tq=tq)
    pair = pl.BlockSpec((SEQ, LANES), lambda b, h: (b, h))
    return pl.pallas_call(
        kern,
        grid=(B, MLA_HEADS // 2),
        in_specs=[pl.BlockSpec((SEQ, 2 * _HEAD_W), lambda b, h: (b, h)), pair,
                  pl.BlockSpec((SEQ, LANES), lambda b, h: (b, 0)), pair],
        out_specs=pair,
        out_shape=jax.ShapeDtypeStruct((B * SEQ, MLA_HEADS * MLA_V), BF16),
        scratch_shapes=[pltpu.VMEM((2, SEQ, _HEAD_W), BF16), pltpu.VMEM((2, _HEAD_W, SEQ), BF16),
                        pltpu.VMEM((2, _VT_ROWS, SEQ), BF16), pltpu.VMEM((4, SEQ, tq), F32),
                        pltpu.VMEM((tq, tq), F32)],
        compiler_params=_cparams(("parallel", "parallel")),
        name="mla_attn",
    )(q, kn, kr, v)


def _even_mixers(x2, B, w_in, cmp_pe, cmp_w1, cmp_w2, gla_w_gate, gla_b_gate, gla_norm):
    q, kvc, kv, ng, nz, gqk, gv, ga, gz = _even_in_proj(x2, _even_in_weight(w_in))

    pe_tab, w1_bd = _compress_weights(cmp_pe, cmp_w1)
    kc, vct = _compress(kvc, pe_tab, w1_bd, cmp_w2, B)

    ocmp_t, qaug_t = _cmp_attn(q, kc, vct, _overlap_t(), _q_alibi_table(), _head_select_mats(),
                               _cmp_bias_table(), ng, B)
    onsa = _sel_win_attn(qaug_t, kv, _nsa_k_tables(), ng, ocmp_t, B)

    wg = jnp.pad(gla_w_gate, ((0, LANES - GLA_GATE_RANK), (0, 0))).astype(BF16)
    ogla = _gla(gqk, gv, ga, gz, wg, gla_b_gate.reshape(1, -1), gla_norm.reshape(1, -1), B)
    return onsa, nz, ogla


def kernel(x, e_w_in, e_cmp_pe, e_cmp_w1, e_cmp_w2, e_gla_w_gate, e_gla_b_gate, e_gla_norm,
           e_w_out, e_ln_g, e_ln_b, o_w_in, o_q_norm, o_w_uq, o_kv_norm, o_w_ukv, o_w_out,
           o_ln_g, o_ln_b):
    assert DEPTH == 2
    B, S, D = x.shape
    x2 = x.reshape(B * S, D)
    onsa, nz, ogla = _even_mixers(x2, B, e_w_in[0], e_cmp_pe[0], e_cmp_w1[0], e_cmp_w2[0],
                                  e_gla_w_gate[0], e_gla_b_gate[0], e_gla_norm[0])
    w1, wq, wk, wv = _odd_weights(o_w_in[0], o_w_uq[0], o_w_ukv[0])
    q_tabs, k_tabs = _rope_tables()
    x1, q, kn, kr, v, z = _even_out_odd_in(
        onsa, nz, ogla, x2, e_w_out[0].astype(BF16), e_ln_g[0].reshape(1, -1), e_ln_b[0].reshape(1, -1),
        w1, wq, wk, wv, o_q_norm[0].reshape(1, -1), o_kv_norm[0].reshape(1, -1), q_tabs, k_tabs)
    o = _mla_attn(q, kn, kr, v, B)
    out = _odd_out(o, z, x1, o_w_out[0].astype(BF16), o_ln_g[0].reshape(1, -1), o_ln_b[0].reshape(1, -1))
    return out.reshape(B, S, D)
```

```python
import functools
import math

import jax
import jax.numpy as jnp
import numpy as np
from jax import lax
from jax.experimental import pallas as pl
from jax.experimental.pallas import tpu as pltpu

F32 = jnp.float32
BF16 = jnp.bfloat16

D_MODEL = 1024
SEQ = 2048
DEPTH = 2
NSA_HEADS = 8
NSA_KV_HEADS = 2
NSA_GROUP = 4
NSA_HEAD_DIM = 64
CMP_BLOCK = 32
CMP_STRIDE = 16
CMP_HIDDEN = 128
N_CMP = (SEQ - CMP_BLOCK) // CMP_STRIDE + 1
N_CMP_PAD = 128
SEL_BLOCK = 64
N_SEL = SEQ // SEL_BLOCK
SEL_TOPN = 8
WINDOW = 512
FORCE_BONUS = 1000.0
GLA_HEADS = 4
GLA_DK = 64
GLA_DV = 128
GLA_GATE_RANK = 16
GLA_TAU = 16.0
GLA_CHUNK = 64
MLA_HEADS = 16
MLA_NOPE = 64
MLA_ROPE = 32
MLA_V = 64
MLA_Q_LORA = 384
MLA_KV_LORA = 256
ROPE_THETA = 10000.0
NORM_EPS = 1e-5
NEG_INF = -1e30
DEEPNORM_ALPHA = (2 * DEPTH) ** 0.25

LANES = 128
MASK_BIG = 2.0 ** 100
VMEM_LIMIT = 56 * 1024 * 1024

_E_OFF = np.cumsum([0, 512, 768, 24, 512, 256, 256, 512, 16, 512])
_O_OFF = np.cumsum([0, MLA_Q_LORA, MLA_KV_LORA, MLA_ROPE, MLA_HEADS * MLA_V])


def _cparams(sem):
    return pltpu.CompilerParams(dimension_semantics=sem, vmem_limit_bytes=VMEM_LIMIT)


def _silu(x):
    return x * jax.nn.sigmoid(x)


def _dot(a, b):
    return jnp.dot(a, b, preferred_element_type=F32)


def _dot_nt(a, b):
    return lax.dot_general(a, b, (((1,), (1,)), ((), ())), preferred_element_type=F32)


_A_SEGS = (("q", 512, BF16), ("kvc", 256, F32), ("kv", 512, BF16), ("ng", 128, F32),
           ("nz", 512, BF16), ("gqk", 512, F32), ("gv", 512, BF16), ("ga", 128, F32),
           ("gz", 512, BF16))
_A_SILU = ("nz", "gz")
_A_OFF = np.cumsum([0] + [s[1] for s in _A_SEGS])


def _even_in_weight(w):
    o = _E_OFF
    nkv = w[:, o[1]:o[2]]

    def kvcol(br, j, g):
        c = ((br * 2 + j) * NSA_KV_HEADS + g) * 64
        return nkv[:, c:c + 64]

    kv_blocks = [kvcol(br, j, g) for j in (0, 1) for g in (0, 1) for br in (1, 2)]
    gates = w[:, o[2]:o[3]].reshape(D_MODEL, 3, NSA_KV_HEADS, NSA_GROUP).transpose(0, 2, 1, 3)
    gates = jnp.pad(gates.reshape(D_MODEL, NSA_KV_HEADS, 12), ((0, 0), (0, 0), (0, 4)))
    ng = jnp.pad(gates.reshape(D_MODEL, 32), ((0, 0), (0, LANES - 32)))
    ga = jnp.pad(w[:, o[7]:o[8]], ((0, 0), (0, LANES - GLA_GATE_RANK)))
    cols = [w[:, o[0]:o[1]] * (NSA_HEAD_DIM ** -0.5), nkv[:, :256]] + kv_blocks + [
        ng, w[:, o[3]:o[4]], w[:, o[4]:o[6]], w[:, o[6]:o[7]], ga, w[:, o[8]:o[9]]]
    return jnp.concatenate(cols, axis=1).astype(BF16)


def _nsa_k_tables():
    pos = np.arange(SEQ)
    k_al = np.zeros((SEQ, LANES), np.float32)
    k_al[:, 96] = 1.0
    k_al[:, 97] = 1.0
    k_al[:, 98] = pos // SEL_BLOCK
    k_al[:, 99] = pos % SEL_BLOCK
    k_sel = k_al.copy()
    k_sel[pos, 64 + pos // SEL_BLOCK] = 1.0
    return jnp.asarray(np.stack([k_sel, k_al]))


def _even_in_kernel(x_ref, w_ref, wg_ref, *out_refs):
    xb = x_ref[...].astype(BF16)
    for (name, width, dt), off, o_ref in zip(_A_SEGS, _A_OFF[:-1], out_refs):
        if name == "ng":
            o_ref[...] = _dot_nt(wg_ref[...], xb)
            continue
        r = _dot(xb, w_ref[:, off:off + width])
        if name in _A_SILU:
            r = _silu(r)
        o_ref[...] = r.astype(dt)


def _even_in_proj(x2, w_bf, tm=512):
    T = x2.shape[0]
    n_w = w_bf.shape[1]
    ng_off = _A_OFF[[n for n, _, _ in _A_SEGS].index("ng")]
    wg_t = w_bf[:, ng_off:ng_off + LANES].T
    out_shape = [jax.ShapeDtypeStruct((wd, T) if name == "ng" else (T, wd), dt)
                 for name, wd, dt in _A_SEGS]
    out_specs = [pl.BlockSpec((wd, tm), lambda i: (0, i)) if name == "ng"
                 else pl.BlockSpec((tm, wd), lambda i: (i, 0)) for name, wd, _ in _A_SEGS]
    return pl.pallas_call(
        _even_in_kernel,
        grid=(T // tm,),
        in_specs=[pl.BlockSpec((tm, D_MODEL), lambda i: (i, 0)),
                  pl.BlockSpec((D_MODEL, n_w), lambda i: (0, 0)),
                  pl.BlockSpec((LANES, D_MODEL), lambda i: (0, 0))],
        out_specs=out_specs,
        out_shape=out_shape,
        compiler_params=_cparams(("parallel",)),
        name="even_in_proj",
    )(x2, w_bf, wg_t)


def _compress_weights(cmp_pe, cmp_w1):
    pe_tab = jnp.concatenate([cmp_pe[0], cmp_pe[0], cmp_pe[1], cmp_pe[1]], axis=1)
    w = cmp_w1.reshape(2, CMP_BLOCK, NSA_HEAD_DIM, CMP_HIDDEN)
    z = jnp.zeros_like(w)
    w_bd = jnp.concatenate([jnp.concatenate([w, z], axis=3), jnp.concatenate([z, w], axis=3)], axis=2)
    return pe_tab, w_bd.astype(BF16)


def _compress_kernel(xk_ref, xv_ref, pe_ref, w1_ref, w2_ref, w2t_ref, kc_ref, vct_ref):
    half = CMP_BLOCK // 2
    acc = [[None, None], [None, None]]
    for l in range(half):
        for j, x_ref in enumerate((xk_ref, xv_ref)):
            xs = x_ref[pl.ds(l, N_CMP_PAD, stride=CMP_STRIDE), :]
            for part in range(2):
                row = part * half + l
                xb = (xs + pe_ref[row:row + 1, j * LANES:(j + 1) * LANES]).astype(BF16)
                d = _dot(xb, w1_ref[j, row])
                acc[j][part] = d if acc[j][part] is None else acc[j][part] + d
    for j in range(2):
        h = acc[j][0] + pltpu.roll(acc[j][1], N_CMP_PAD - 1, 0)
        hs = _silu(h).astype(BF16)
        for g in range(NSA_KV_HEADS):
            hg = hs[:, g * CMP_HIDDEN:(g + 1) * CMP_HIDDEN]
            if j == 0:
                kc_ref[0, g] = _dot(hg, w2_ref[...])
            else:
                vct_ref[0, g] = _dot_nt(w2t_ref[...], hg)


def _compress(kvc, pe_tab, w1_bd, w2, B):
    w2k = w2[0].astype(BF16)
    w2vt = w2[1].T.astype(BF16)
    return pl.pallas_call(
        _compress_kernel,
        grid=(B,),
        in_specs=[pl.BlockSpec((SEQ, LANES), lambda b: (b, 0)),
                  pl.BlockSpec((SEQ, LANES), lambda b: (b, 1)),
                  pl.BlockSpec((CMP_BLOCK, 256), lambda b: (0, 0)),
                  pl.BlockSpec((2, CMP_BLOCK, LANES, 256), lambda b: (0, 0, 0, 0)),
                  pl.BlockSpec((CMP_HIDDEN, NSA_HEAD_DIM), lambda b: (0, 0)),
                  pl.BlockSpec((NSA_HEAD_DIM, CMP_HIDDEN), lambda b: (0, 0))],
        out_specs=[pl.BlockSpec((1, NSA_KV_HEADS, N_CMP_PAD, NSA_HEAD_DIM), lambda b: (b, 0, 0, 0)),
                   pl.BlockSpec((1, NSA_KV_HEADS, NSA_HEAD_DIM, N_CMP_PAD), lambda b: (b, 0, 0, 0))],
        out_shape=[jax.ShapeDtypeStruct((B, NSA_KV_HEADS, N_CMP_PAD, NSA_HEAD_DIM), F32),
                   jax.ShapeDtypeStruct((B, NSA_KV_HEADS, NSA_HEAD_DIM, N_CMP_PAD), F32)],
        compiler_params=_cparams(("parallel",)),
        name="nsa_compress",
    )(kvc, kvc, pe_tab, w1_bd, w2k, w2vt)


_TILE_GROUP = 2


def _trace_round_robin(gens):
    gens = list(gens)
    while gens:
        for g in list(gens):
            try:
                next(g)
            except StopIteration:
                gens.remove(g)


def _alibi_slopes_np():
    return np.exp2(-(8.0 / NSA_HEADS) * np.arange(1, NSA_HEADS + 1)).astype(np.float32)


def _overlap_t():
    cs = np.arange(N_CMP) * CMP_STRIDE
    ss = np.arange(N_SEL) * SEL_BLOCK
    ov = np.clip(np.minimum(cs[:, None] + CMP_BLOCK, ss[None, :] + SEL_BLOCK)
                 - np.maximum(cs[:, None], ss[None, :]), 0, None).astype(np.float32) / CMP_BLOCK
    ovt = np.zeros((N_SEL, N_CMP_PAD), np.float32)
    ovt[:, :N_CMP] = ov.T
    return jnp.asarray(ovt)


def _q_alibi_table():
    slopes = _alibi_slopes_np().reshape(NSA_KV_HEADS, NSA_GROUP)
    pos = np.arange(SEQ)
    tab = np.zeros((NSA_KV_HEADS, SEQ, LANES), np.float32)
    for g in range(NSA_KV_HEADS):
        for r in range(NSA_GROUP):
            m = slopes[g, r]
            tab[g, :, r * 32 + 0] = -m * SEL_BLOCK * (pos // SEL_BLOCK)
            tab[g, :, r * 32 + 1] = -m * (pos % SEL_BLOCK)
            tab[g, :, r * 32 + 2] = m * SEL_BLOCK
            tab[g, :, r * 32 + 3] = m
    return jnp.asarray(tab)


def _head_select_mats():
    m = np.zeros((NSA_GROUP, NSA_HEAD_DIM, 256), np.float32)
    for r in range(NSA_GROUP):
        m[r, np.arange(64), r * 64 + np.arange(64)] = 1.0
    return jnp.asarray(m, dtype=BF16)


_GATE_ROWS = 16


def _cmp_bias_table():
    slopes = jnp.asarray(_alibi_slopes_np()).reshape(NSA_KV_HEADS, NSA_GROUP, 1, 1)
    n = jnp.arange(N_CMP_PAD)[:, None]
    t = jnp.arange(SEQ)[None, :]
    dist = t - (n * CMP_STRIDE + (CMP_BLOCK - 1))
    visible = (dist >= 0) & (n < N_CMP)
    return jnp.where(visible, -slopes * dist.astype(F32), NEG_INF)


def _cmp_attn_kernel(q_ref, kc_ref, vct_ref, ovt_ref, qal_ref, hs_ref, bias_ref, gate_ref,
                     ocmp_ref, qaug_ref, *, tq):
    kc = kc_ref[0, 0].astype(BF16)
    vct = vct_ref[0, 0].astype(BF16)
    k_ext = [_dot(kc, hs_ref[r]).astype(BF16) for r in range(NSA_GROUP)]
    ri = lax.broadcasted_iota(jnp.int32, (256, 256), 0)
    ci = lax.broadcasted_iota(jnp.int32, (256, 256), 1)
    eye = jnp.where(ri == ci, 1.0, 0.0).astype(BF16)
    sub8 = lax.broadcasted_iota(jnp.int32, (8, tq), 0)
    j_idx = lax.broadcasted_iota(jnp.int32, (N_SEL, tq), 0)
    t_lane = lax.broadcasted_iota(jnp.int32, (N_SEL, tq), 1)
    n_grp = N_SEL // 8
    for i in range(SEQ // tq):
        cols = slice(i * tq, (i + 1) * tq)
        nv = min(N_CMP_PAD, (i + 1) * tq // CMP_STRIDE)
        q = q_ref[cols, :]
        lhs = jnp.concatenate([k[:nv] for k in k_ext] + [eye], axis=0)
        res = _dot_nt(lhs, q)
        q_t = res[NSA_GROUP * nv:]
        p_sum = None
        o_parts = []
        for r in range(NSA_GROUP):
            s = res[r * nv:(r + 1) * nv] + bias_ref[0, r, 0:nv, cols]
            m = jnp.max(s, axis=0, keepdims=True)
            e = jnp.exp(s - m)
            if i == 0:
                e = jnp.where(bias_ref[0, r, 0:nv, cols] > 0.5 * NEG_INF, e, 0.0)
            l = jnp.sum(e, axis=0, keepdims=True)
            p = e / jnp.where(l > 0.0, l, 1.0)
            p_sum = p if p_sum is None else p_sum + p
            gate = jax.nn.sigmoid(gate_ref[r:r + 1, cols])
            o_parts.append(_dot(vct[:, :nv], p.astype(BF16)) * gate)
        ocmp_ref[0, 0, :, cols] = jnp.concatenate(o_parts, axis=0).astype(ocmp_ref.dtype)

        imp = jnp.dot(ovt_ref[:, 0:nv], p_sum, preferred_element_type=F32,
                      precision=lax.Precision.HIGHEST)
        cur = (i * tq + t_lane) // SEL_BLOCK
        forced = (j_idx == 0) | (j_idx == cur) | (j_idx == cur - 1)
        imp = jnp.where(j_idx > cur, -1.0, imp + jnp.where(forced, FORCE_BONUS, 0.0))
        grp = [imp[8 * a:8 * (a + 1)] for a in range(n_grp)]
        rank = [jnp.zeros((8, tq), F32) for _ in range(n_grp)]
        for k in range(N_SEL):
            row = jnp.broadcast_to(imp[k:k + 1, :], (8, tq))
            for a in range(n_grp):
                ge = jnp.where(row >= grp[a], 1.0, 0.0)
                gt = jnp.where(row > grp[a], 1.0, 0.0)
                if 8 * a > k:
                    cnt = ge
                elif 8 * a + 7 <= k:
                    cnt = gt
                else:
                    cnt = jnp.where(sub8 > k - 8 * a, ge, gt)
                rank[a] = rank[a] + cnt
        sel_bias = jnp.where(jnp.concatenate(rank, axis=0) < float(SEL_TOPN), 0.0, -MASK_BIG)

        qal_t = qal_ref[0, cols, :].T
        for r in range(NSA_GROUP):
            aug_t = jnp.concatenate([q_t[r * 64:(r + 1) * 64], sel_bias,
                                     qal_t[r * 32:(r + 1) * 32, :]], axis=0)
            qaug_ref[0, 0, r, :, cols] = aug_t.astype(BF16)


def _cmp_attn(q, kc, vct, ovt, qal, hs, bias, gates_t, B, tq=256):
    kern = functools.partial(_cmp_attn_kernel, tq=tq)
    return pl.pallas_call(
        kern,
        grid=(NSA_KV_HEADS, B),
        in_specs=[pl.BlockSpec((SEQ, 256), lambda g, b: (b, g)),
                  pl.BlockSpec((1, 1, N_CMP_PAD, NSA_HEAD_DIM), lambda g, b: (b, g, 0, 0)),
                  pl.BlockSpec((1, 1, NSA_HEAD_DIM, N_CMP_PAD), lambda g, b: (b, g, 0, 0)),
                  pl.BlockSpec((N_SEL, N_CMP_PAD), lambda g, b: (0, 0)),
                  pl.BlockSpec((1, SEQ, LANES), lambda g, b: (g, 0, 0)),
                  pl.BlockSpec((NSA_GROUP, NSA_HEAD_DIM, 256), lambda g, b: (0, 0, 0)),
                  pl.BlockSpec((1, NSA_GROUP, N_CMP_PAD, SEQ), lambda g, b: (g, 0, 0, 0)),
                  pl.BlockSpec((_GATE_ROWS, SEQ), lambda g, b: (g, b))],
        out_specs=[pl.BlockSpec((1, 1, 256, SEQ), lambda g, b: (b, g, 0, 0)),
                   pl.BlockSpec((1, 1, NSA_GROUP, LANES, SEQ), lambda g, b: (b, g, 0, 0, 0))],
        out_shape=[jax.ShapeDtypeStruct((B, NSA_KV_HEADS, 256, SEQ), BF16),
                   jax.ShapeDtypeStruct((B, NSA_KV_HEADS, NSA_GROUP, LANES, SEQ), BF16)],
        compiler_params=_cparams(("parallel", "parallel")),
        name="nsa_cmp_attn_select",
    )(q, kc, vct, ovt, qal, hs, bias, gates_t)


_NSA_VT_ROWS = NSA_HEAD_DIM + 16


def _sel_win_kernel(qaug_ref, k_ref, v_ref, ktab_ref, gate_ref, ocmp_ref,
                    o_ref, kx_ref, vt_ref, s_ref, bias_ref, *, tq):
    nq = SEQ // tq
    n_back = WINDOW // tq
    a_idx = lax.broadcasted_iota(jnp.int32, (tq, tq), 0)
    b_idx = lax.broadcasted_iota(jnp.int32, (tq, tq), 1)
    bias_ref[0] = jnp.where(a_idx <= b_idx, 0.0, -MASK_BIG)
    bias_ref[1] = jnp.where(a_idx > b_idx, 0.0, -MASK_BIG)
    lane = lax.broadcasted_iota(jnp.int32, (tq, LANES), 1)
    ones_rows = jnp.where(
        lax.broadcasted_iota(jnp.int32, (_NSA_VT_ROWS - NSA_HEAD_DIM, tq), 0) == 0, 1.0, 0.0)
    for i in range(nq):
        rows = slice(i * tq, (i + 1) * tq)
        kp = k_ref[rows, :].astype(F32)
        vt = v_ref[rows, :].astype(F32).T
        for br in range(2):
            k_br = kp if br == 0 else pltpu.roll(kp, NSA_HEAD_DIM, 1)
            kx_ref[br, rows, :] = jnp.where(lane < NSA_HEAD_DIM, k_br, ktab_ref[br, rows, :]).astype(BF16)
            vt_ref[br, :, rows] = jnp.concatenate(
                [vt[br * NSA_HEAD_DIM:(br + 1) * NSA_HEAD_DIM], ones_rows], axis=0).astype(BF16)

    def tiles_of(i, br):
        if br == 0:
            return [(j, None) for j in range(i)] + [(i, 0)]
        tl = [(i - d, 1 if d == n_back else None) for d in range(n_back, 0, -1) if i >= d]
        return tl + [(i, 0)]

    units = [(i, br) for i in range(nq) for br in range(2)]
    n_buf = s_ref.shape[0]

    def stage1(n):
        i, br = units[n]
        q = jnp.concatenate([qaug_ref[0, 0, r, :, i * tq:(i + 1) * tq] for r in range(NSA_GROUP)],
                            axis=1)
        mx = None
        for t_n, (j, bias) in enumerate(tiles_of(i, br)):
            s = _dot(kx_ref[br, j * tq:(j + 1) * tq, :], q)
            if bias is not None:
                s = s + jnp.concatenate([bias_ref[bias]] * NSA_GROUP, axis=1)
            s_ref[n % n_buf, t_n * tq:(t_n + 1) * tq, :] = s
            t = s[0:8]
            for c in range(1, tq // 8):
                t = jnp.maximum(t, s[c * 8:(c + 1) * 8])
            mx = t if mx is None else jnp.maximum(mx, t)
        return jnp.max(mx, axis=0, keepdims=True)

    def stage2(n, m):
        i, br = units[n]
        acc = None
        for t_n, (j, _) in enumerate(tiles_of(i, br)):
            p = jnp.exp(s_ref[n % n_buf, t_n * tq:(t_n + 1) * tq, :] - m).astype(BF16)
            pv = _dot(vt_ref[br, :, j * tq:(j + 1) * tq], p)
            acc = pv if acc is None else acc + pv
        cols = slice(i * tq, (i + 1) * tq)
        parts = []
        for r in range(NSA_GROUP):
            hl = slice(r * tq, (r + 1) * tq)
            g_row = 4 * (br + 1) + r
            scale = jax.nn.sigmoid(gate_ref[g_row:g_row + 1, cols]) / acc[NSA_HEAD_DIM:NSA_HEAD_DIM + 1, hl]
            parts.append(acc[0:NSA_HEAD_DIM, hl] * scale)
        return jnp.concatenate(parts, axis=0)

    ms = [stage1(0), stage1(1)]
    total = None
    for n in range(len(units)):
        if n + 2 < len(units):
            ms.append(stage1(n + 2))
        o_t = stage2(n, ms[n])
        i, br = units[n]
        cols = slice(i * tq, (i + 1) * tq)
        if br == 0:
            total = ocmp_ref[0, 0, :, cols].astype(F32) + o_t
        else:
            o_ref[cols, :] = (total + o_t).T.astype(o_ref.dtype)


def _sel_win_attn(qaug_t, kv, k_tabs, gates_t, ocmp_t, B, tq=256):
    kern = functools.partial(_sel_win_kernel, tq=tq)

    def kv_spec(col):
        return pl.BlockSpec((SEQ, LANES), lambda b, g: (b, col + g))

    return pl.pallas_call(
        kern,
        grid=(B, NSA_KV_HEADS),
        in_specs=[pl.BlockSpec((1, 1, NSA_GROUP, LANES, SEQ), lambda b, g: (b, g, 0, 0, 0)),
                  kv_spec(0), kv_spec(2),
                  pl.BlockSpec((2, SEQ, LANES), lambda b, g: (0, 0, 0)),
                  pl.BlockSpec((_GATE_ROWS, SEQ), lambda b, g: (g, b)),
                  pl.BlockSpec((1, 1, 256, SEQ), lambda b, g: (b, g, 0, 0))],
        out_specs=pl.BlockSpec((SEQ, 256), lambda b, g: (b, g)),
        out_shape=jax.ShapeDtypeStruct((B * SEQ, 512), BF16),
        scratch_shapes=[pltpu.VMEM((2, SEQ, LANES), BF16),
                        pltpu.VMEM((2, _NSA_VT_ROWS, SEQ), BF16),
                        pltpu.VMEM((3, SEQ, NSA_GROUP * tq), F32),
                        pltpu.VMEM((2, tq, tq), F32)],
        compiler_params=_cparams(("parallel", "parallel")),
        name="nsa_sel_win_attn",
    )(qaug_t, kv, kv, k_tabs, gates_t, ocmp_t)


_GLA_ROWS = 512


def _gla_kernel(qk_ref, v_ref, ga_ref, gz_ref, wg_ref, bg_ref, ng_ref, o_ref,
                b_ref, qd_ref, kd_ref):
    C = GLA_CHUNK
    KW = GLA_HEADS * GLA_DK
    wg = wg_ref[...]
    bg = bg_ref[...]
    norm_g = ng_ref[...]

    rin = lax.broadcasted_iota(jnp.int32, (_GLA_ROWS, KW), 0) % C
    lane = lax.broadcasted_iota(jnp.int32, (_GLA_ROWS, KW), 1)
    for blk in range(SEQ // _GLA_ROWS):
        rows = slice(blk * _GLA_ROWS, (blk + 1) * _GLA_ROWS)
        x = _dot(ga_ref[rows, :].astype(BF16), wg) + bg
        b = -(jnp.maximum(-x, 0.0) + jnp.log1p(jnp.exp(-jnp.abs(x)))) / GLA_TAU
        sh = 1
        while sh < C:
            b = b + jnp.where(rin >= sh, pltpu.roll(b, sh, 0), 0.0)
            sh *= 2
        b_ref[rows, :] = b
        q_d = qk_ref[rows, 0:KW] * (GLA_DK ** -0.5) * jnp.exp(b)
        for h in range(GLA_HEADS):
            in_head = (lane >= h * GLA_DK) & (lane < (h + 1) * GLA_DK)
            qd_ref[h, rows, :] = jnp.where(in_head, q_d, 0.0).astype(BF16)
        kd_ref[rows, :] = (qk_ref[rows, KW:2 * KW] * jnp.exp(-b)).astype(BF16)

    r_idx = lax.broadcasted_iota(jnp.int32, (GLA_HEADS * C, C), 0) % C
    c_idx = lax.broadcasted_iota(jnp.int32, (GLA_HEADS * C, C), 1)
    causal = r_idx >= c_idx

    def prep(n):
        rows = slice(n * C, (n + 1) * C)
        b = b_ref[rows, :]
        b_last = b_ref[n * C + C - 1:(n + 1) * C, :]
        k_u = (qk_ref[rows, KW:2 * KW] * jnp.exp(b_last - b)).astype(BF16)
        v = v_ref[rows, :]
        q_all = jnp.concatenate([qd_ref[h, rows, :] for h in range(GLA_HEADS)], axis=0)
        a = jnp.where(causal, _dot_nt(q_all, kd_ref[rows, :]), 0.0).astype(BF16)
        o_intra = [_dot(a[h * C:(h + 1) * C], v[:, h * GLA_DV:(h + 1) * GLA_DV])
                   for h in range(GLA_HEADS)]
        kv = lax.dot_general(k_u, v, (((0,), (0,)), ((), ())), preferred_element_type=F32)
        upd = jnp.concatenate([kv[h * GLA_DK:(h + 1) * GLA_DK, h * GLA_DV:(h + 1) * GLA_DV]
                               for h in range(GLA_HEADS)], axis=0)
        dec = jnp.exp(jnp.broadcast_to(b_last, (GLA_DV, KW)).T)
        return q_all, o_intra, upd, dec

    st = jnp.zeros((KW, GLA_DV), F32)
    n_chunks = SEQ // C
    nxt = prep(0)
    for n in range(n_chunks):
        q_all, o_intra, upd, dec = nxt
        if n + 1 < n_chunks:
            nxt = prep(n + 1)
        rows = slice(n * C, (n + 1) * C)
        o_inter = _dot(q_all, st.astype(BF16))
        for h in range(GLA_HEADS):
            vs = slice(h * GLA_DV, (h + 1) * GLA_DV)
            o = o_intra[h] + o_inter[h * C:(h + 1) * C]
            y = o * lax.rsqrt(jnp.mean(o * o, axis=-1, keepdims=True) + NORM_EPS) * norm_g
            y = y * gz_ref[rows, vs].astype(F32)
            o_ref[rows, vs] = y.astype(o_ref.dtype)
        st = st * dec + upd


def _gla(gqk, gv, ga, gz, wg, bg, norm_g, B):
    row = lambda b: (b, 0)
    const = lambda b: (0, 0)
    return pl.pallas_call(
        _gla_kernel,
        grid=(B,),
        in_specs=[pl.BlockSpec((SEQ, 512), row), pl.BlockSpec((SEQ, 512), row),
                  pl.BlockSpec((SEQ, LANES), row), pl.BlockSpec((SEQ, 512), row),
                  pl.BlockSpec((LANES, 256), const), pl.BlockSpec((1, 256), const),
                  pl.BlockSpec((1, GLA_DV), const)],
        out_specs=pl.BlockSpec((SEQ, 512), row),
        out_shape=jax.ShapeDtypeStruct((B * SEQ, 512), BF16),
        scratch_shapes=[pltpu.VMEM((SEQ, GLA_HEADS * GLA_DK), F32),
                        pltpu.VMEM((GLA_HEADS, SEQ, GLA_HEADS * GLA_DK), BF16),
                        pltpu.VMEM((SEQ, GLA_HEADS * GLA_DK), BF16)],
        compiler_params=_cparams(("parallel",)),
        name="gla_chunked",
    )(gqk, gv, ga, gz, wg, bg, norm_g)


def _deepnorm_ln(x, y, g, b):
    r = DEEPNORM_ALPHA * x + y
    mu = jnp.mean(r, axis=-1, keepdims=True)
    d = r - mu
    var = jnp.mean(d * d, axis=-1, keepdims=True)
    return d * lax.rsqrt(var + NORM_EPS) * g + b


def _odd_out_kernel(o_ref_in, z_ref, x_ref, w_ref, g_ref, b_ref, o_ref):
    o = (o_ref_in[...].astype(F32) * z_ref[...].astype(F32)).astype(BF16)
    y = _dot(o, w_ref[...])
    o_ref[...] = _deepnorm_ln(x_ref[...], y, g_ref[...], b_ref[...])


def _odd_out(o, z, x2, w_bf, ln_g, ln_b, tm=512):
    T = x2.shape[0]
    row = lambda i: (i, 0)
    const = lambda i: (0, 0)
    return pl.pallas_call(
        _odd_out_kernel,
        grid=(T // tm,),
        in_specs=[pl.BlockSpec((tm, D_MODEL), row), pl.BlockSpec((tm, D_MODEL), row),
                  pl.BlockSpec((tm, D_MODEL), row), pl.BlockSpec((D_MODEL, D_MODEL), const),
                  pl.BlockSpec((1, D_MODEL), const), pl.BlockSpec((1, D_MODEL), const)],
        out_specs=pl.BlockSpec((tm, D_MODEL), row),
        out_shape=jax.ShapeDtypeStruct((T, D_MODEL), F32),
        compiler_params=_cparams(("parallel",)),
        name="odd_out_proj_ln",
    )(o, z, x2, w_bf, ln_g, ln_b)


_HEAD_W = LANES
_QK_W = MLA_HEADS * _HEAD_W
_ROPE_HALF = MLA_ROPE // 2


def _odd_weights(w_in, w_uq, w_ukv):
    o = _O_OFF
    z_nope = jnp.zeros((D_MODEL, MLA_NOPE), F32)
    z_tail = jnp.zeros((D_MODEL, _HEAD_W - MLA_NOPE - MLA_ROPE), F32)
    w1 = jnp.concatenate([w_in[:, o[0]:o[2]], z_nope, w_in[:, o[2]:o[3]], z_tail,
                          w_in[:, o[3]:o[4]]], axis=1).astype(BF16)
    uq = w_uq.reshape(MLA_Q_LORA, MLA_HEADS, MLA_NOPE + MLA_ROPE)
    zq = jnp.zeros((MLA_Q_LORA, MLA_HEADS, _HEAD_W - MLA_NOPE - MLA_ROPE), F32)
    wq = jnp.concatenate([uq, zq], axis=-1).reshape(MLA_Q_LORA, _QK_W).astype(BF16)
    ukv = w_ukv.reshape(MLA_KV_LORA, MLA_HEADS, MLA_NOPE + MLA_V)
    wk = ukv[..., :MLA_NOPE].reshape(MLA_KV_LORA, MLA_HEADS * MLA_NOPE).astype(BF16)
    wv = ukv[..., MLA_NOPE:].reshape(MLA_KV_LORA, MLA_HEADS * MLA_V).astype(BF16)
    return w1, wq, wk, wv


def _rope_tables():
    freqs = jnp.exp(-math.log(ROPE_THETA) * jnp.arange(_ROPE_HALF, dtype=F32) * 2.0 / MLA_ROPE)
    ang = jnp.arange(SEQ, dtype=F32)[:, None] * freqs[None, :]
    cos, sin = jnp.cos(ang), jnp.sin(ang)
    z_half = jnp.zeros((SEQ, _ROPE_HALF), F32)
    z_tail = jnp.zeros((SEQ, _HEAD_W - MLA_NOPE - MLA_ROPE), F32)
    z_nope = jnp.zeros((SEQ, MLA_NOPE), F32)

    def tables(scale, nope_gain):
        nope = jnp.full((SEQ, MLA_NOPE), nope_gain, F32)
        c = jnp.concatenate([nope, cos * scale, cos * scale, z_tail], axis=1)
        s1 = jnp.concatenate([z_nope, -sin * scale, z_half, z_tail], axis=1)
        s2 = jnp.concatenate([z_nope, z_half, sin * scale, z_tail], axis=1)
        return c, s1, s2

    q_scale = (MLA_NOPE + MLA_ROPE) ** -0.5 * math.log2(math.e)
    return tables(q_scale, q_scale), tables(1.0, 0.0)


def _rope_block(x, c, s1, s2):
    return x * c + pltpu.roll(x, LANES - _ROPE_HALF, 1) * s1 + pltpu.roll(x, _ROPE_HALF, 1) * s2


def _rms(x, g):
    return x * lax.rsqrt(jnp.mean(x * x, axis=-1, keepdims=True) + NORM_EPS) * g


def _even_out_odd_in_kernel(onsa_ref, nz_ref, ogla_ref, x_ref, wo_ref, g_ref, b_ref,
                            w1_ref, wq_ref, wk_ref, wv_ref, qn_ref, kn_ref,
                            qc_ref, qs1_ref, qs2_ref, kc_ref, ks1_ref, ks2_ref,
                            x1_ref, q_ref, k_ref, kr_ref, v_ref, z_ref):
    tm = x_ref.shape[0]
    halves = [slice(0, tm // 2), slice(tm // 2, tm)]
    qc_all, qs1_all, qs2_all = qc_ref[...], qs1_ref[...], qs2_ref[...]

    def out_proj(r):
        o_nsa = (onsa_ref[r, :].astype(F32) * nz_ref[r, :].astype(F32)).astype(BF16)
        return _dot(o_nsa, wo_ref[0:512, :]) + _dot(ogla_ref[r, :], wo_ref[512:1024, :])

    def layer_norm(r, y):
        x1 = _deepnorm_ln(x_ref[r, :], y, g_ref[...], b_ref[...])
        x1_ref[r, :] = x1
        return x1.astype(BF16)

    def latents(r, xb):
        c_q = _dot(xb, w1_ref[:, 0:MLA_Q_LORA])
        c_kv = _dot(xb, w1_ref[:, MLA_Q_LORA:640])
        kr = _dot(xb, w1_ref[:, 640:768])
        z_ref[r, 0:512] = _silu(_dot(xb, w1_ref[:, 768:1280])).astype(BF16)
        return c_q, c_kv, kr

    def up_proj(r, xb, c_q, c_kv, kr):
        cqn = _rms(c_q, qn_ref[...]).astype(BF16)
        ckvn = _rms(c_kv, kn_ref[...]).astype(BF16)
        kr_ref[r, :] = _rope_block(kr, kc_ref[r, :], ks1_ref[r, :], ks2_ref[r, :]).astype(BF16)
        k_ref[r, :] = _dot(ckvn, wk_ref[...]).astype(BF16)
        v_ref[r, :] = _dot(ckvn, wv_ref[...]).astype(BF16)
        qc, qs1, qs2 = qc_all[r], qs1_all[r], qs2_all[r]
        grp = 4
        for g0 in range(0, MLA_HEADS, grp):
            qa = _dot(cqn, wq_ref[:, g0 * _HEAD_W:(g0 + grp) * _HEAD_W])
            for h in range(grp):
                sl = slice(h * _HEAD_W, (h + 1) * _HEAD_W)
                q_ref[r, (g0 + h) * _HEAD_W:(g0 + h + 1) * _HEAD_W] = (
                    _rope_block(qa[:, sl], qc, qs1, qs2).astype(BF16))
            if g0 >= 2 * grp:
                zc = slice((g0 // grp) * 256, (g0 // grp + 1) * 256)
                z_ref[r, zc] = _silu(_dot(xb, w1_ref[:, 768 + zc.start:768 + zc.stop])).astype(BF16)

    ra, rb = halves
    ya = out_proj(ra)
    yb = out_proj(rb)
    xa = layer_norm(ra, ya)
    la = latents(ra, xa)
    xbb = layer_norm(rb, yb)
    lb = latents(rb, xbb)
    up_proj(ra, xa, *la)
    up_proj(rb, xbb, *lb)


def _even_out_odd_in(onsa, nz, ogla, x2, wo, ln_g, ln_b, w1, wq, wk, wv, qn, kn, q_tabs, k_tabs,
                     tm=512):
    T = x2.shape[0]
    s_tiles = SEQ // tm
    row = lambda i: (i, 0)
    const = lambda i: (0, 0)
    pos = lambda i: (i % s_tiles, 0)

    def full(a):
        return pl.BlockSpec(a.shape, const)

    tab = pl.BlockSpec((tm, LANES), pos)
    n_kv = MLA_HEADS * MLA_NOPE
    return pl.pallas_call(
        _even_out_odd_in_kernel,
        grid=(T // tm,),
        in_specs=[pl.BlockSpec((tm, 512), row), pl.BlockSpec((tm, 512), row),
                  pl.BlockSpec((tm, 512), row), pl.BlockSpec((tm, D_MODEL), row),
                  full(wo), full(ln_g), full(ln_b),
                  full(w1), full(wq), full(wk), full(wv), full(qn), full(kn),
                  tab, tab, tab, tab, tab, tab],
        out_specs=[pl.BlockSpec((tm, D_MODEL), row),
                   pl.BlockSpec((tm, _QK_W), row), pl.BlockSpec((tm, n_kv), row),
                   pl.BlockSpec((tm, LANES), row), pl.BlockSpec((tm, n_kv), row),
                   pl.BlockSpec((tm, 1024), row)],
        out_shape=[jax.ShapeDtypeStruct((T, D_MODEL), F32),
                   jax.ShapeDtypeStruct((T, _QK_W), BF16), jax.ShapeDtypeStruct((T, n_kv), BF16),
                   jax.ShapeDtypeStruct((T, LANES), BF16), jax.ShapeDtypeStruct((T, n_kv), BF16),
                   jax.ShapeDtypeStruct((T, 1024), BF16)],
        compiler_params=_cparams(("parallel",)),
        name="even_out_odd_in_proj",
    )(onsa, nz, ogla, x2, wo, ln_g, ln_b, w1, wq, wk, wv, qn, kn, *q_tabs, *k_tabs)


_VT_ROWS = MLA_V + 16


def _mla_kernel(q_ref, kn_ref, kr_ref, v_ref, o_ref, kx_ref, qt_ref, vt_ref, s_ref, bias_ref, *, tq):
    nq = SEQ // tq
    a_idx = lax.broadcasted_iota(jnp.int32, (tq, tq), 0)
    b_idx = lax.broadcasted_iota(jnp.int32, (tq, tq), 1)
    bias_ref[...] = jnp.where(a_idx <= b_idx, 0.0, NEG_INF)
    lane = lax.broadcasted_iota(jnp.int32, (tq, LANES), 1)
    ones_rows = jnp.where(lax.broadcasted_iota(jnp.int32, (_VT_ROWS - MLA_V, tq), 0) == 0, 1.0, 0.0)

    for i in range(nq):
        rows = slice(i * tq, (i + 1) * tq)
        knp = kn_ref[rows, :].astype(F32)
        krb = kr_ref[rows, :].astype(F32)
        vt = v_ref[rows, :].astype(F32).T
        for h in range(2):
            kn_h = knp if h == 0 else pltpu.roll(knp, MLA_NOPE, 1)
            kx_ref[h, rows, :] = jnp.where(lane < MLA_NOPE, kn_h, krb).astype(BF16)
            qt_ref[h, :, rows] = q_ref[rows, h * _HEAD_W:(h + 1) * _HEAD_W].astype(F32).T.astype(BF16)
            vt_ref[h, :, rows] = jnp.concatenate(
                [vt[h * MLA_V:(h + 1) * MLA_V], ones_rows], axis=0).astype(BF16)

    units = [(i, h) for i in range(nq) for h in range(2)]
    n_buf = s_ref.shape[0]

    def stage1(n, ms):
        i, h = units[n]
        qt = qt_ref[h, :, i * tq:(i + 1) * tq]
        mx = None
        for j in range(i + 1):
            cols = slice(j * tq, (j + 1) * tq)
            s = _dot(kx_ref[h, cols, :], qt)
            if j == i:
                s = s + bias_ref[...]
            s_ref[n % n_buf, cols, :] = s
            t = s[0:8]
            for c in range(1, tq // 8):
                t = jnp.maximum(t, s[c * 8:(c + 1) * 8])
            mx = t if mx is None else jnp.maximum(mx, t)
            if j % _TILE_GROUP == _TILE_GROUP - 1:
                yield
        ms[n] = jnp.max(mx, axis=0, keepdims=True)

    def stage2(n, m, outs):
        i, h = units[n]
        acc = None
        for j in range(i + 1):
            cols = slice(j * tq, (j + 1) * tq)
            p = jnp.exp2(s_ref[n % n_buf, cols, :] - m).astype(BF16)
            pv = _dot(vt_ref[h, :, cols], p)
            acc = pv if acc is None else acc + pv
            if j % _TILE_GROUP == _TILE_GROUP - 1:
                yield
        outs.append(acc[0:MLA_V] / acc[MLA_V:MLA_V + 1])

    ms = {}
    _trace_round_robin([stage1(0, ms)])
    _trace_round_robin([stage1(1, ms)])
    outs = []
    for n in range(len(units)):
        gens = [stage2(n, ms[n], outs)]
        if n + 2 < len(units):
            gens.insert(0, stage1(n + 2, ms))
        _trace_round_robin(gens)
        if len(outs) == 2:
            i = units[n][0]
            o_ref[i * tq:(i + 1) * tq, :] = jnp.concatenate(outs, axis=0).T.astype(o_ref.dtype)
            outs = []


def _mla_attn(q, kn, kr, v, B, tq=256):
    kern = functools.partial(_mla_kernel, tq=tq)
    pair = pl.BlockSpec((SEQ, LANES), lambda b, h: (b, h))
    return pl.pallas_call(
        kern,
        grid=(B, MLA_HEADS // 2),
        in_specs=[pl.BlockSpec((SEQ, 2 * _HEAD_W), lambda b, h: (b, h)), pair,
                  pl.BlockSpec((SEQ, LANES), lambda b, h: (b, 0)), pair],
        out_specs=pair,
        out_shape=jax.ShapeDtypeStruct((B * SEQ, MLA_HEADS * MLA_V), BF16),
        scratch_shapes=[pltpu.VMEM((2, SEQ, _HEAD_W), BF16), pltpu.VMEM((2, _HEAD_W, SEQ), BF16),
                        pltpu.VMEM((2, _VT_ROWS, SEQ), BF16), pltpu.VMEM((4, SEQ, tq), F32),
                        pltpu.VMEM((tq, tq), F32)],
        compiler_params=_cparams(("parallel", "parallel")),
        name="mla_attn",
    )(q, kn, kr, v)


def _even_mixers(x2, B, w_in, cmp_pe, cmp_w1, cmp_w2, gla_w_gate, gla_b_gate, gla_norm):
    q, kvc, kv, ng, nz, gqk, gv, ga, gz = _even_in_proj(x2, _even_in_weight(w_in))

    pe_tab, w1_bd = _compress_weights(cmp_pe, cmp_w1)
    kc, vct = _compress(kvc, pe_tab, w1_bd, cmp_w2, B)

    ocmp_t, qaug_t = _cmp_attn(q, kc, vct, _overlap_t(), _q_alibi_table(), _head_select_mats(),
                               _cmp_bias_table(), ng, B)
    onsa = _sel_win_attn(qaug_t, kv, _nsa_k_tables(), ng, ocmp_t, B)

    wg = jnp.pad(gla_w_gate, ((0, LANES - GLA_GATE_RANK), (0, 0))).astype(BF16)
    ogla = _gla(gqk, gv, ga, gz, wg, gla_b_gate.reshape(1, -1), gla_norm.reshape(1, -1), B)
    return onsa, nz, ogla


def kernel(x, e_w_in, e_cmp_pe, e_cmp_w1, e_cmp_w2, e_gla_w_gate, e_gla_b_gate, e_gla_norm,
           e_w_out, e_ln_g, e_ln_b, o_w_in, o_q_norm, o_w_uq, o_kv_norm, o_w_ukv, o_w_out,
           o_ln_g, o_ln_b):
    assert DEPTH == 2
    B, S, D = x.shape
    x2 = x.reshape(B * S, D)
    onsa, nz, ogla = _even_mixers(x2, B, e_w_in[0], e_cmp_pe[0], e_cmp_w1[0], e_cmp_w2[0],
                                  e_gla_w_gate[0], e_gla_b_gate[0], e_gla_norm[0])
    w1, wq, wk, wv = _odd_weights(o_w_in[0], o_w_uq[0], o_w_ukv[0])
    q_tabs, k_tabs = _rope_tables()
    x1, q, kn, kr, v, z = _even_out_odd_in(
        onsa, nz, ogla, x2, e_w_out[0].astype(BF16), e_ln_g[0].reshape(1, -1), e_ln_b[0].reshape(1, -1),
        w1, wq, wk, wv, o_q_norm[0].reshape(1, -1), o_kv_norm[0].reshape(1, -1), q_tabs, k_tabs)
    o = _mla_attn(q, kn, kr, v, B)
    out = _odd_out(o, z, x1, o_w_out[0].astype(BF16), o_ln_g[0].reshape(1, -1), o_ln_b[0].reshape(1, -1))
    return out.reshape(B, S, D)
```

```python
import functools
import math

import jax
import jax.numpy as jnp
import numpy as np
from jax import lax
from jax.experimental import pallas as pl
from jax.experimental.pallas import tpu as pltpu

F32 = jnp.float32
BF16 = jnp.bfloat16

D_MODEL = 1024
SEQ = 2048
DEPTH = 2
NSA_HEADS = 8
NSA_KV_HEADS = 2
NSA_GROUP = 4
NSA_HEAD_DIM = 64
CMP_BLOCK = 32
CMP_STRIDE = 16
CMP_HIDDEN = 128
N_CMP = (SEQ - CMP_BLOCK) // CMP_STRIDE + 1
N_CMP_PAD = 128
SEL_BLOCK = 64
N_SEL = SEQ // SEL_BLOCK
SEL_TOPN = 8
WINDOW = 512
FORCE_BONUS = 1000.0
GLA_HEADS = 4
GLA_DK = 64
GLA_DV = 128
GLA_GATE_RANK = 16
GLA_TAU = 16.0
GLA_CHUNK = 64
MLA_HEADS = 16
MLA_NOPE = 64
MLA_ROPE = 32
MLA_V = 64
MLA_Q_LORA = 384
MLA_KV_LORA = 256
ROPE_THETA = 10000.0
NORM_EPS = 1e-5
NEG_INF = -1e30
DEEPNORM_ALPHA = (2 * DEPTH) ** 0.25

LANES = 128
MASK_BIG = 2.0 ** 100
VMEM_LIMIT = 56 * 1024 * 1024

_E_OFF = np.cumsum([0, 512, 768, 24, 512, 256, 256, 512, 16, 512])
_O_OFF = np.cumsum([0, MLA_Q_LORA, MLA_KV_LORA, MLA_ROPE, MLA_HEADS * MLA_V])


def _cparams(sem):
    return pltpu.CompilerParams(dimension_semantics=sem, vmem_limit_bytes=VMEM_LIMIT)


def _silu(x):
    return x * jax.nn.sigmoid(x)


def _dot(a, b):
    return jnp.dot(a, b, preferred_element_type=F32)


def _dot_nt(a, b):
    return lax.dot_general(a, b, (((1,), (1,)), ((), ())), preferred_element_type=F32)


_A_SEGS = (("q", 512, BF16), ("kvc", 256, F32), ("kv", 512, BF16), ("ng", 128, F32),
           ("nz", 512, BF16), ("gqk", 512, F32), ("gv", 512, BF16), ("ga", 128, F32),
           ("gz", 512, BF16))
_A_SILU = ("nz", "gz")
_A_OFF = np.cumsum([0] + [s[1] for s in _A_SEGS])


def _even_in_weight(w):
    o = _E_OFF
    w = w.astype(BF16)
    nkv = w[:, o[1]:o[2]]

    def kvcol(br, j, g):
        c = ((br * 2 + j) * NSA_KV_HEADS + g) * 64
        return nkv[:, c:c + 64]

    kv_blocks = [kvcol(br, j, g) for j in (0, 1) for g in (0, 1) for br in (1, 2)]
    gates = w[:, o[2]:o[3]].reshape(D_MODEL, 3, NSA_KV_HEADS, NSA_GROUP).transpose(0, 2, 1, 3)
    gates = jnp.pad(gates.reshape(D_MODEL, NSA_KV_HEADS, 12), ((0, 0), (0, 0), (0, 4)))
    ng = jnp.pad(gates.reshape(D_MODEL, 32), ((0, 0), (0, LANES - 32)))
    ga = jnp.pad(w[:, o[7]:o[8]], ((0, 0), (0, LANES - GLA_GATE_RANK)))
    cols = [w[:, o[0]:o[1]] * (NSA_HEAD_DIM ** -0.5), nkv[:, :256]] + kv_blocks + [
        ng, w[:, o[3]:o[4]], w[:, o[4]:o[6]], w[:, o[6]:o[7]], ga, w[:, o[8]:o[9]]]
    return jnp.concatenate(cols, axis=1)


def _nsa_k_tables():
    pos = np.arange(SEQ)
    k_al = np.zeros((SEQ, LANES), np.float32)
    k_al[:, 96] = 1.0
    k_al[:, 97] = 1.0
    k_al[:, 98] = pos // SEL_BLOCK
    k_al[:, 99] = pos % SEL_BLOCK
    k_sel = k_al.copy()
    k_sel[pos, 64 + pos // SEL_BLOCK] = 1.0
    return jnp.asarray(np.stack([k_sel, k_al]))


def _even_in_kernel(x_ref, w_ref, wg_ref, *out_refs):
    xb = x_ref[...].astype(BF16)
    for (name, width, dt), off, o_ref in zip(_A_SEGS, _A_OFF[:-1], out_refs):
        if name == "ng":
            o_ref[...] = _dot_nt(wg_ref[...], xb)
            continue
        r = _dot(xb, w_ref[:, off:off + width])
        if name in _A_SILU:
            r = _silu(r)
        o_ref[...] = r.astype(dt)


def _even_in_proj(x2, w_bf, tm=512):
    T = x2.shape[0]
    n_w = w_bf.shape[1]
    ng_off = _A_OFF[[n for n, _, _ in _A_SEGS].index("ng")]
    wg_t = w_bf[:, ng_off:ng_off + LANES].T
    out_shape = [jax.ShapeDtypeStruct((wd, T) if name == "ng" else (T, wd), dt)
                 for name, wd, dt in _A_SEGS]
    out_specs = [pl.BlockSpec((wd, tm), lambda i: (0, i)) if name == "ng"
                 else pl.BlockSpec((tm, wd), lambda i: (i, 0)) for name, wd, _ in _A_SEGS]
    return pl.pallas_call(
        _even_in_kernel,
        grid=(T // tm,),
        in_specs=[pl.BlockSpec((tm, D_MODEL), lambda i: (i, 0)),
                  pl.BlockSpec((D_MODEL, n_w), lambda i: (0, 0)),
                  pl.BlockSpec((LANES, D_MODEL), lambda i: (0, 0))],
        out_specs=out_specs,
        out_shape=out_shape,
        compiler_params=_cparams(("parallel",)),
        name="even_in_proj",
    )(x2, w_bf, wg_t)


def _compress_weights(cmp_pe, cmp_w1):
    pe_tab = jnp.concatenate([cmp_pe[0], cmp_pe[0], cmp_pe[1], cmp_pe[1]], axis=1)
    w = cmp_w1.astype(BF16).reshape(2, CMP_BLOCK, NSA_HEAD_DIM, CMP_HIDDEN)
    z = jnp.zeros_like(w)
    w_bd = jnp.concatenate([jnp.concatenate([w, z], axis=3), jnp.concatenate([z, w], axis=3)], axis=2)
    return pe_tab, w_bd


def _compress_kernel(xk_ref, xv_ref, pe_ref, w1_ref, w2_ref, w2t_ref, kc_ref, vct_ref):
    half = CMP_BLOCK // 2
    acc = [[None, None], [None, None]]
    for l in range(half):
        for j, x_ref in enumerate((xk_ref, xv_ref)):
            xs = x_ref[pl.ds(l, N_CMP_PAD, stride=CMP_STRIDE), :]
            for part in range(2):
                row = part * half + l
                xb = (xs + pe_ref[row:row + 1, j * LANES:(j + 1) * LANES]).astype(BF16)
                d = _dot(xb, w1_ref[j, row])
                acc[j][part] = d if acc[j][part] is None else acc[j][part] + d
    for j in range(2):
        h = acc[j][0] + pltpu.roll(acc[j][1], N_CMP_PAD - 1, 0)
        hs = _silu(h).astype(BF16)
        for g in range(NSA_KV_HEADS):
            hg = hs[:, g * CMP_HIDDEN:(g + 1) * CMP_HIDDEN]
            if j == 0:
                kc_ref[0, g] = _dot(hg, w2_ref[...])
            else:
                vct_ref[0, g] = _dot_nt(w2t_ref[...], hg)


def _compress(kvc, pe_tab, w1_bd, w2, B):
    w2k = w2[0].astype(BF16)
    w2vt = w2[1].T.astype(BF16)
    return pl.pallas_call(
        _compress_kernel,
        grid=(B,),
        in_specs=[pl.BlockSpec((SEQ, LANES), lambda b: (b, 0)),
                  pl.BlockSpec((SEQ, LANES), lambda b: (b, 1)),
                  pl.BlockSpec((CMP_BLOCK, 256), lambda b: (0, 0)),
                  pl.BlockSpec((2, CMP_BLOCK, LANES, 256), lambda b: (0, 0, 0, 0)),
                  pl.BlockSpec((CMP_HIDDEN, NSA_HEAD_DIM), lambda b: (0, 0)),
                  pl.BlockSpec((NSA_HEAD_DIM, CMP_HIDDEN), lambda b: (0, 0))],
        out_specs=[pl.BlockSpec((1, NSA_KV_HEADS, N_CMP_PAD, NSA_HEAD_DIM), lambda b: (b, 0, 0, 0)),
                   pl.BlockSpec((1, NSA_KV_HEADS, NSA_HEAD_DIM, N_CMP_PAD), lambda b: (b, 0, 0, 0))],
        out_shape=[jax.ShapeDtypeStruct((B, NSA_KV_HEADS, N_CMP_PAD, NSA_HEAD_DIM), F32),
                   jax.ShapeDtypeStruct((B, NSA_KV_HEADS, NSA_HEAD_DIM, N_CMP_PAD), F32)],
        compiler_params=_cparams(("parallel",)),
        name="nsa_compress",
    )(kvc, kvc, pe_tab, w1_bd, w2k, w2vt)


_TILE_GROUP = 2


def _trace_round_robin(gens):
    gens = list(gens)
    while gens:
        for g in list(gens):
            try:
                next(g)
            except StopIteration:
                gens.remove(g)


def _alibi_slopes_np():
    return np.exp2(-(8.0 / NSA_HEADS) * np.arange(1, NSA_HEADS + 1)).astype(np.float32)


def _overlap_t():
    cs = np.arange(N_CMP) * CMP_STRIDE
    ss = np.arange(N_SEL) * SEL_BLOCK
    ov = np.clip(np.minimum(cs[:, None] + CMP_BLOCK, ss[None, :] + SEL_BLOCK)
                 - np.maximum(cs[:, None], ss[None, :]), 0, None).astype(np.float32) / CMP_BLOCK
    ovt = np.zeros((N_SEL, N_CMP_PAD), np.float32)
    ovt[:, :N_CMP] = ov.T
    return jnp.asarray(ovt)


def _q_alibi_table():
    slopes = _alibi_slopes_np().reshape(NSA_KV_HEADS, NSA_GROUP)
    pos = np.arange(SEQ)
    tab = np.zeros((NSA_KV_HEADS, SEQ, LANES), np.float32)
    for g in range(NSA_KV_HEADS):
        for r in range(NSA_GROUP):
            m = slopes[g, r]
            tab[g, :, r * 32 + 0] = -m * SEL_BLOCK * (pos // SEL_BLOCK)
            tab[g, :, r * 32 + 1] = -m * (pos % SEL_BLOCK)
            tab[g, :, r * 32 + 2] = m * SEL_BLOCK
            tab[g, :, r * 32 + 3] = m
    return jnp.asarray(tab)


def _head_select_mats():
    m = np.zeros((NSA_GROUP, NSA_HEAD_DIM, 256), np.float32)
    for r in range(NSA_GROUP):
        m[r, np.arange(64), r * 64 + np.arange(64)] = 1.0
    return jnp.asarray(m, dtype=BF16)


_GATE_ROWS = 16


def _cmp_bias_table():
    slopes = jnp.asarray(_alibi_slopes_np()).reshape(NSA_KV_HEADS, NSA_GROUP, 1, 1)
    n = jnp.arange(N_CMP_PAD)[:, None]
    t = jnp.arange(SEQ)[None, :]
    dist = t - (n * CMP_STRIDE + (CMP_BLOCK - 1))
    visible = (dist >= 0) & (n < N_CMP)
    return jnp.where(visible, -slopes * dist.astype(F32), NEG_INF)


def _cmp_attn_kernel(q_ref, kc_ref, vct_ref, ovt_ref, qal_ref, hs_ref, bias_ref, gate_ref,
                     ocmp_ref, qaug_ref, *, tq):
    kc = kc_ref[0, 0].astype(BF16)
    vct = vct_ref[0, 0].astype(BF16)
    k_ext = [_dot(kc, hs_ref[r]).astype(BF16) for r in range(NSA_GROUP)]
    ri = lax.broadcasted_iota(jnp.int32, (256, 256), 0)
    ci = lax.broadcasted_iota(jnp.int32, (256, 256), 1)
    eye = jnp.where(ri == ci, 1.0, 0.0).astype(BF16)
    sub8 = lax.broadcasted_iota(jnp.int32, (8, tq), 0)
    j_idx = lax.broadcasted_iota(jnp.int32, (N_SEL, tq), 0)
    t_lane = lax.broadcasted_iota(jnp.int32, (N_SEL, tq), 1)
    n_grp = N_SEL // 8
    for i in range(SEQ // tq):
        cols = slice(i * tq, (i + 1) * tq)
        nv = min(N_CMP_PAD, (i + 1) * tq // CMP_STRIDE)
        q = q_ref[cols, :]
        lhs = jnp.concatenate([k[:nv] for k in k_ext] + [eye], axis=0)
        res = _dot_nt(lhs, q)
        q_t = res[NSA_GROUP * nv:]
        p_sum = None
        o_parts = []
        for r in range(NSA_GROUP):
            s = res[r * nv:(r + 1) * nv] + bias_ref[0, r, 0:nv, cols]
            m = jnp.max(s, axis=0, keepdims=True)
            e = jnp.exp(s - m)
            if i == 0:
                e = jnp.where(bias_ref[0, r, 0:nv, cols] > 0.5 * NEG_INF, e, 0.0)
            l = jnp.sum(e, axis=0, keepdims=True)
            p = e / jnp.where(l > 0.0, l, 1.0)
            p_sum = p if p_sum is None else p_sum + p
            gate = jax.nn.sigmoid(gate_ref[r:r + 1, cols])
            o_parts.append(_dot(vct[:, :nv], p.astype(BF16)) * gate)
        ocmp_ref[0, 0, :, cols] = jnp.concatenate(o_parts, axis=0).astype(ocmp_ref.dtype)

        imp = jnp.dot(ovt_ref[:, 0:nv], p_sum, preferred_element_type=F32,
                      precision=lax.Precision.HIGHEST)
        cur = (i * tq + t_lane) // SEL_BLOCK
        forced = (j_idx == 0) | (j_idx == cur) | (j_idx == cur - 1)
        imp = jnp.where(j_idx > cur, -1.0, imp + jnp.where(forced, FORCE_BONUS, 0.0))
        grp = [imp[8 * a:8 * (a + 1)] for a in range(n_grp)]
        rank = [jnp.zeros((8, tq), F32) for _ in range(n_grp)]
        for k in range(N_SEL):
            row = jnp.broadcast_to(imp[k:k + 1, :], (8, tq))
            for a in range(n_grp):
                ge = jnp.where(row >= grp[a], 1.0, 0.0)
                gt = jnp.where(row > grp[a], 1.0, 0.0)
                if 8 * a > k:
                    cnt = ge
                elif 8 * a + 7 <= k:
                    cnt = gt
                else:
                    cnt = jnp.where(sub8 > k - 8 * a, ge, gt)
                rank[a] = rank[a] + cnt
        sel_bias = jnp.where(jnp.concatenate(rank, axis=0) < float(SEL_TOPN), 0.0, -MASK_BIG)

        qal_t = qal_ref[0, cols, :].T
        for r in range(NSA_GROUP):
            aug_t = jnp.concatenate([q_t[r * 64:(r + 1) * 64], sel_bias,
                                     qal_t[r * 32:(r + 1) * 32, :]], axis=0)
            qaug_ref[0, 0, r, :, cols] = aug_t.astype(BF16)


def _cmp_attn(q, kc, vct, ovt, qal, hs, bias, gates_t, B, tq=256):
    kern = functools.partial(_cmp_attn_kernel, tq=tq)
    return pl.pallas_call(
        kern,
        grid=(NSA_KV_HEADS, B),
        in_specs=[pl.BlockSpec((SEQ, 256), lambda g, b: (b, g)),
                  pl.BlockSpec((1, 1, N_CMP_PAD, NSA_HEAD_DIM), lambda g, b: (b, g, 0, 0)),
                  pl.BlockSpec((1, 1, NSA_HEAD_DIM, N_CMP_PAD), lambda g, b: (b, g, 0, 0)),
                  pl.BlockSpec((N_SEL, N_CMP_PAD), lambda g, b: (0, 0)),
                  pl.BlockSpec((1, SEQ, LANES), lambda g, b: (g, 0, 0)),
                  pl.BlockSpec((NSA_GROUP, NSA_HEAD_DIM, 256), lambda g, b: (0, 0, 0)),
                  pl.BlockSpec((1, NSA_GROUP, N_CMP_PAD, SEQ), lambda g, b: (g, 0, 0, 0)),
                  pl.BlockSpec((_GATE_ROWS, SEQ), lambda g, b: (g, b))],
        out_specs=[pl.BlockSpec((1, 1, 256, SEQ), lambda g, b: (b, g, 0, 0)),
                   pl.BlockSpec((1, 1, NSA_GROUP, LANES, SEQ), lambda g, b: (b, g, 0, 0, 0))],
        out_shape=[jax.ShapeDtypeStruct((B, NSA_KV_HEADS, 256, SEQ), BF16),
                   jax.ShapeDtypeStruct((B, NSA_KV_HEADS, NSA_GROUP, LANES, SEQ), BF16)],
        compiler_params=_cparams(("parallel", "parallel")),
        name="nsa_cmp_attn_select",
    )(q, kc, vct, ovt, qal, hs, bias, gates_t)


_NSA_VT_ROWS = NSA_HEAD_DIM + 16


def _sel_win_kernel(qaug_ref, k_ref, v_ref, ktab_ref, gate_ref, ocmp_ref,
                    o_ref, kx_ref, vt_ref, s_ref, bias_ref, *, tq):
    nq = SEQ // tq
    n_back = WINDOW // tq
    a_idx = lax.broadcasted_iota(jnp.int32, (tq, tq), 0)
    b_idx = lax.broadcasted_iota(jnp.int32, (tq, tq), 1)
    bias_ref[0] = jnp.where(a_idx <= b_idx, 0.0, -MASK_BIG)
    bias_ref[1] = jnp.where(a_idx > b_idx, 0.0, -MASK_BIG)
    lane = lax.broadcasted_iota(jnp.int32, (tq, LANES), 1)
    ones_rows = jnp.where(
        lax.broadcasted_iota(jnp.int32, (_NSA_VT_ROWS - NSA_HEAD_DIM, tq), 0) == 0, 1.0, 0.0)
    for i in range(nq):
        rows = slice(i * tq, (i + 1) * tq)
        kp = k_ref[rows, :].astype(F32)
        vt = v_ref[rows, :].astype(F32).T
        for br in range(2):
            k_br = kp if br == 0 else pltpu.roll(kp, NSA_HEAD_DIM, 1)
            kx_ref[br, rows, :] = jnp.where(lane < NSA_HEAD_DIM, k_br, ktab_ref[br, rows, :]).astype(BF16)
            vt_ref[br, :, rows] = jnp.concatenate(
                [vt[br * NSA_HEAD_DIM:(br + 1) * NSA_HEAD_DIM], ones_rows], axis=0).astype(BF16)

    def tiles_of(i, br):
        if br == 0:
            return [(j, None) for j in range(i)] + [(i, 0)]
        tl = [(i - d, 1 if d == n_back else None) for d in range(n_back, 0, -1) if i >= d]
        return tl + [(i, 0)]

    units = [(i, br) for i in range(nq) for br in range(2)]
    n_buf = s_ref.shape[0]

    def stage1(n):
        i, br = units[n]
        q = jnp.concatenate([qaug_ref[0, 0, r, :, i * tq:(i + 1) * tq] for r in range(NSA_GROUP)],
                            axis=1)
        mx = None
        for t_n, (j, bias) in enumerate(tiles_of(i, br)):
            s = _dot(kx_ref[br, j * tq:(j + 1) * tq, :], q)
            if bias is not None:
                s = s + jnp.concatenate([bias_ref[bias]] * NSA_GROUP, axis=1)
            s_ref[n % n_buf, t_n * tq:(t_n + 1) * tq, :] = s
            t = s[0:8]
            for c in range(1, tq // 8):
                t = jnp.maximum(t, s[c * 8:(c + 1) * 8])
            mx = t if mx is None else jnp.maximum(mx, t)
        return jnp.max(mx, axis=0, keepdims=True)

    def stage2(n, m):
        i, br = units[n]
        acc = None
        for t_n, (j, _) in enumerate(tiles_of(i, br)):
            p = jnp.exp(s_ref[n % n_buf, t_n * tq:(t_n + 1) * tq, :] - m).astype(BF16)
            pv = _dot(vt_ref[br, :, j * tq:(j + 1) * tq], p)
            acc = pv if acc is None else acc + pv
        cols = slice(i * tq, (i + 1) * tq)
        parts = []
        for r in range(NSA_GROUP):
            hl = slice(r * tq, (r + 1) * tq)
            g_row = 4 * (br + 1) + r
            scale = jax.nn.sigmoid(gate_ref[g_row:g_row + 1, cols]) / acc[NSA_HEAD_DIM:NSA_HEAD_DIM + 1, hl]
            parts.append(acc[0:NSA_HEAD_DIM, hl] * scale)
        return jnp.concatenate(parts, axis=0)

    ms = [stage1(0), stage1(1)]
    total = None
    for n in range(len(units)):
        if n + 2 < len(units):
            ms.append(stage1(n + 2))
        o_t = stage2(n, ms[n])
        i, br = units[n]
        cols = slice(i * tq, (i + 1) * tq)
        if br == 0:
            total = ocmp_ref[0, 0, :, cols].astype(F32) + o_t
        else:
            o_ref[cols, :] = (total + o_t).T.astype(o_ref.dtype)


def _sel_win_attn(qaug_t, kv, k_tabs, gates_t, ocmp_t, B, tq=256):
    kern = functools.partial(_sel_win_kernel, tq=tq)

    def kv_spec(col):
        return pl.BlockSpec((SEQ, LANES), lambda b, g: (b, col + g))

    return pl.pallas_call(
        kern,
        grid=(B, NSA_KV_HEADS),
        in_specs=[pl.BlockSpec((1, 1, NSA_GROUP, LANES, SEQ), lambda b, g: (b, g, 0, 0, 0)),
                  kv_spec(0), kv_spec(2),
                  pl.BlockSpec((2, SEQ, LANES), lambda b, g: (0, 0, 0)),
                  pl.BlockSpec((_GATE_ROWS, SEQ), lambda b, g: (g, b)),
                  pl.BlockSpec((1, 1, 256, SEQ), lambda b, g: (b, g, 0, 0))],
        out_specs=pl.BlockSpec((SEQ, 256), lambda b, g: (b, g)),
        out_shape=jax.ShapeDtypeStruct((B * SEQ, 512), BF16),
        scratch_shapes=[pltpu.VMEM((2, SEQ, LANES), BF16),
                        pltpu.VMEM((2, _NSA_VT_ROWS, SEQ), BF16),
                        pltpu.VMEM((3, SEQ, NSA_GROUP * tq), F32),
                        pltpu.VMEM((2, tq, tq), F32)],
        compiler_params=_cparams(("parallel", "parallel")),
        name="nsa_sel_win_attn",
    )(qaug_t, kv, kv, k_tabs, gates_t, ocmp_t)


_GLA_ROWS = 512


def _gla_kernel(qk_ref, v_ref, ga_ref, gz_ref, wg_ref, bg_ref, ng_ref, o_ref,
                b_ref, qd_ref, kd_ref):
    C = GLA_CHUNK
    KW = GLA_HEADS * GLA_DK
    wg = wg_ref[...]
    bg = bg_ref[...]
    norm_g = ng_ref[...]

    rin = lax.broadcasted_iota(jnp.int32, (_GLA_ROWS, KW), 0) % C
    lane = lax.broadcasted_iota(jnp.int32, (_GLA_ROWS, KW), 1)
    for blk in range(SEQ // _GLA_ROWS):
        rows = slice(blk * _GLA_ROWS, (blk + 1) * _GLA_ROWS)
        x = _dot(ga_ref[rows, :].astype(BF16), wg) + bg
        b = -(jnp.maximum(-x, 0.0) + jnp.log1p(jnp.exp(-jnp.abs(x)))) / GLA_TAU
        sh = 1
        while sh < C:
            b = b + jnp.where(rin >= sh, pltpu.roll(b, sh, 0), 0.0)
            sh *= 2
        b_ref[rows, :] = b
        q_d = qk_ref[rows, 0:KW] * (GLA_DK ** -0.5) * jnp.exp(b)
        for h in range(GLA_HEADS):
            in_head = (lane >= h * GLA_DK) & (lane < (h + 1) * GLA_DK)
            qd_ref[h, rows, :] = jnp.where(in_head, q_d, 0.0).astype(BF16)
        kd_ref[rows, :] = (qk_ref[rows, KW:2 * KW] * jnp.exp(-b)).astype(BF16)

    r_idx = lax.broadcasted_iota(jnp.int32, (GLA_HEADS * C, C), 0) % C
    c_idx = lax.broadcasted_iota(jnp.int32, (GLA_HEADS * C, C), 1)
    causal = r_idx >= c_idx

    def prep(n):
        rows = slice(n * C, (n + 1) * C)
        b = b_ref[rows, :]
        b_last = b_ref[n * C + C - 1:(n + 1) * C, :]
        k_u = (qk_ref[rows, KW:2 * KW] * jnp.exp(b_last - b)).astype(BF16)
        v = v_ref[rows, :]
        q_all = jnp.concatenate([qd_ref[h, rows, :] for h in range(GLA_HEADS)], axis=0)
        a = jnp.where(causal, _dot_nt(q_all, kd_ref[rows, :]), 0.0).astype(BF16)
        o_intra = [_dot(a[h * C:(h + 1) * C], v[:, h * GLA_DV:(h + 1) * GLA_DV])
                   for h in range(GLA_HEADS)]
        kv = lax.dot_general(k_u, v, (((0,), (0,)), ((), ())), preferred_element_type=F32)
        upd = jnp.concatenate([kv[h * GLA_DK:(h + 1) * GLA_DK, h * GLA_DV:(h + 1) * GLA_DV]
                               for h in range(GLA_HEADS)], axis=0)
        dec = jnp.exp(jnp.broadcast_to(b_last, (GLA_DV, KW)).T)
        return q_all, o_intra, upd, dec

    st = jnp.zeros((KW, GLA_DV), F32)
    n_chunks = SEQ // C
    nxt = prep(0)
    for n in range(n_chunks):
        q_all, o_intra, upd, dec = nxt
        if n + 1 < n_chunks:
            nxt = prep(n + 1)
        rows = slice(n * C, (n + 1) * C)
        o_inter = _dot(q_all, st.astype(BF16))
        for h in range(GLA_HEADS):
            vs = slice(h * GLA_DV, (h + 1) * GLA_DV)
            o = o_intra[h] + o_inter[h * C:(h + 1) * C]
            y = o * lax.rsqrt(jnp.mean(o * o, axis=-1, keepdims=True) + NORM_EPS) * norm_g
            y = y * gz_ref[rows, vs].astype(F32)
            o_ref[rows, vs] = y.astype(o_ref.dtype)
        st = st * dec + upd


def _gla(gqk, gv, ga, gz, wg, bg, norm_g, B):
    row = lambda b: (b, 0)
    const = lambda b: (0, 0)
    return pl.pallas_call(
        _gla_kernel,
        grid=(B,),
        in_specs=[pl.BlockSpec((SEQ, 512), row), pl.BlockSpec((SEQ, 512), row),
                  pl.BlockSpec((SEQ, LANES), row), pl.BlockSpec((SEQ, 512), row),
                  pl.BlockSpec((LANES, 256), const), pl.BlockSpec((1, 256), const),
                  pl.BlockSpec((1, GLA_DV), const)],
        out_specs=pl.BlockSpec((SEQ, 512), row),
        out_shape=jax.ShapeDtypeStruct((B * SEQ, 512), BF16),
        scratch_shapes=[pltpu.VMEM((SEQ, GLA_HEADS * GLA_DK), F32),
                        pltpu.VMEM((GLA_HEADS, SEQ, GLA_HEADS * GLA_DK), BF16),
                        pltpu.VMEM((SEQ, GLA_HEADS * GLA_DK), BF16)],
        compiler_params=_cparams(("parallel",)),
        name="gla_chunked",
    )(gqk, gv, ga, gz, wg, bg, norm_g)


def _deepnorm_ln(x, y, g, b):
    r = DEEPNORM_ALPHA * x + y
    mu = jnp.mean(r, axis=-1, keepdims=True)
    d = r - mu
    var = jnp.mean(d * d, axis=-1, keepdims=True)
    return d * lax.rsqrt(var + NORM_EPS) * g + b


def _odd_out_kernel(o_ref_in, z_ref, x_ref, w_ref, g_ref, b_ref, o_ref):
    o = (o_ref_in[...].astype(F32) * z_ref[...].astype(F32)).astype(BF16)
    y = _dot(o, w_ref[...])
    o_ref[...] = _deepnorm_ln(x_ref[...], y, g_ref[...], b_ref[...])


def _odd_out(o, z, x2, w_bf, ln_g, ln_b, tm=512):
    T = x2.shape[0]
    row = lambda i: (i, 0)
    const = lambda i: (0, 0)
    return pl.pallas_call(
        _odd_out_kernel,
        grid=(T // tm,),
        in_specs=[pl.BlockSpec((tm, D_MODEL), row), pl.BlockSpec((tm, D_MODEL), row),
                  pl.BlockSpec((tm, D_MODEL), row), pl.BlockSpec((D_MODEL, D_MODEL), const),
                  pl.BlockSpec((1, D_MODEL), const), pl.BlockSpec((1, D_MODEL), const)],
        out_specs=pl.BlockSpec((tm, D_MODEL), row),
        out_shape=jax.ShapeDtypeStruct((T, D_MODEL), F32),
        compiler_params=_cparams(("parallel",)),
        name="odd_out_proj_ln",
    )(o, z, x2, w_bf, ln_g, ln_b)


_HEAD_W = LANES
_QK_W = MLA_HEADS * _HEAD_W
_ROPE_HALF = MLA_ROPE // 2


def _odd_weights(w_in, w_uq, w_ukv):
    o = _O_OFF
    w_in, w_uq, w_ukv = (w.astype(BF16) for w in (w_in, w_uq, w_ukv))
    z_nope = jnp.zeros((D_MODEL, MLA_NOPE), BF16)
    z_tail = jnp.zeros((D_MODEL, _HEAD_W - MLA_NOPE - MLA_ROPE), BF16)
    w1 = jnp.concatenate([w_in[:, o[0]:o[2]], z_nope, w_in[:, o[2]:o[3]], z_tail,
                          w_in[:, o[3]:o[4]]], axis=1)
    uq = w_uq.reshape(MLA_Q_LORA, MLA_HEADS, MLA_NOPE + MLA_ROPE)
    zq = jnp.zeros((MLA_Q_LORA, MLA_HEADS, _HEAD_W - MLA_NOPE - MLA_ROPE), BF16)
    wq = jnp.concatenate([uq, zq], axis=-1).reshape(MLA_Q_LORA, _QK_W)
    ukv = w_ukv.reshape(MLA_KV_LORA, MLA_HEADS, MLA_NOPE + MLA_V)
    wk = ukv[..., :MLA_NOPE].reshape(MLA_KV_LORA, MLA_HEADS * MLA_NOPE)
    wv = ukv[..., MLA_NOPE:].reshape(MLA_KV_LORA, MLA_HEADS * MLA_V)
    return w1, wq, wk, wv


def _rope_tables():
    freqs = jnp.exp(-math.log(ROPE_THETA) * jnp.arange(_ROPE_HALF, dtype=F32) * 2.0 / MLA_ROPE)
    ang = jnp.arange(SEQ, dtype=F32)[:, None] * freqs[None, :]
    cos, sin = jnp.cos(ang), jnp.sin(ang)
    z_half = jnp.zeros((SEQ, _ROPE_HALF), F32)
    z_tail = jnp.zeros((SEQ, _HEAD_W - MLA_NOPE - MLA_ROPE), F32)
    z_nope = jnp.zeros((SEQ, MLA_NOPE), F32)

    def tables(scale, nope_gain):
        nope = jnp.full((SEQ, MLA_NOPE), nope_gain, F32)
        c = jnp.concatenate([nope, cos * scale, cos * scale, z_tail], axis=1)
        s1 = jnp.concatenate([z_nope, -sin * scale, z_half, z_tail], axis=1)
        s2 = jnp.concatenate([z_nope, z_half, sin * scale, z_tail], axis=1)
        return c, s1, s2

    q_scale = (MLA_NOPE + MLA_ROPE) ** -0.5 * math.log2(math.e)
    return tables(q_scale, q_scale), tables(1.0, 0.0)


def _rope_block(x, c, s1, s2):
    return x * c + pltpu.roll(x, LANES - _ROPE_HALF, 1) * s1 + pltpu.roll(x, _ROPE_HALF, 1) * s2


def _rms(x, g):
    return x * lax.rsqrt(jnp.mean(x * x, axis=-1, keepdims=True) + NORM_EPS) * g


def _even_out_odd_in_kernel(onsa_ref, nz_ref, ogla_ref, x_ref, wo_ref, g_ref, b_ref,
                            w1_ref, wq_ref, wk_ref, wv_ref, qn_ref, kn_ref,
                            qc_ref, qs1_ref, qs2_ref, kc_ref, ks1_ref, ks2_ref,
                            x1_ref, q_ref, k_ref, kr_ref, v_ref, z_ref):
    tm = x_ref.shape[0]
    halves = [slice(0, tm // 2), slice(tm // 2, tm)]
    qc_all, qs1_all, qs2_all = qc_ref[...], qs1_ref[...], qs2_ref[...]

    def out_proj(r):
        o_nsa = (onsa_ref[r, :].astype(F32) * nz_ref[r, :].astype(F32)).astype(BF16)
        return _dot(o_nsa, wo_ref[0:512, :]) + _dot(ogla_ref[r, :], wo_ref[512:1024, :])

    def layer_norm(r, y):
        x1 = _deepnorm_ln(x_ref[r, :], y, g_ref[...], b_ref[...])
        x1_ref[r, :] = x1
        return x1.astype(BF16)

    def latents(r, xb):
        c_q = _dot(xb, w1_ref[:, 0:MLA_Q_LORA])
        c_kv = _dot(xb, w1_ref[:, MLA_Q_LORA:640])
        kr = _dot(xb, w1_ref[:, 640:768])
        z_ref[r, 0:512] = _silu(_dot(xb, w1_ref[:, 768:1280])).astype(BF16)
        return c_q, c_kv, kr

    def up_proj(r, xb, c_q, c_kv, kr):
        cqn = _rms(c_q, qn_ref[...]).astype(BF16)
        ckvn = _rms(c_kv, kn_ref[...]).astype(BF16)
        kr_ref[r, :] = _rope_block(kr, kc_ref[r, :], ks1_ref[r, :], ks2_ref[r, :]).astype(BF16)
        k_ref[r, :] = _dot(ckvn, wk_ref[...]).astype(BF16)
        v_ref[r, :] = _dot(ckvn, wv_ref[...]).astype(BF16)
        qc, qs1, qs2 = qc_all[r], qs1_all[r], qs2_all[r]
        grp = 4
        for g0 in range(0, MLA_HEADS, grp):
            qa = _dot(cqn, wq_ref[:, g0 * _HEAD_W:(g0 + grp) * _HEAD_W])
            for h in range(grp):
                sl = slice(h * _HEAD_W, (h + 1) * _HEAD_W)
                q_ref[r, (g0 + h) * _HEAD_W:(g0 + h + 1) * _HEAD_W] = (
                    _rope_block(qa[:, sl], qc, qs1, qs2).astype(BF16))
            if g0 >= 2 * grp:
                zc = slice((g0 // grp) * 256, (g0 // grp + 1) * 256)
                z_ref[r, zc] = _silu(_dot(xb, w1_ref[:, 768 + zc.start:768 + zc.stop])).astype(BF16)

    ra, rb = halves
    ya = out_proj(ra)
    yb = out_proj(rb)
    xa = layer_norm(ra, ya)
    la = latents(ra, xa)
    xbb = layer_norm(rb, yb)
    lb = latents(rb, xbb)
    up_proj(ra, xa, *la)
    up_proj(rb, xbb, *lb)


def _even_out_odd_in(onsa, nz, ogla, x2, wo, ln_g, ln_b, w1, wq, wk, wv, qn, kn, q_tabs, k_tabs,
                     tm=512):
    T = x2.shape[0]
    s_tiles = SEQ // tm
    row = lambda i: (i, 0)
    const = lambda i: (0, 0)
    pos = lambda i: (i % s_tiles, 0)

    def full(a):
        return pl.BlockSpec(a.shape, const)

    tab = pl.BlockSpec((tm, LANES), pos)
    n_kv = MLA_HEADS * MLA_NOPE
    return pl.pallas_call(
        _even_out_odd_in_kernel,
        grid=(T // tm,),
        in_specs=[pl.BlockSpec((tm, 512), row), pl.BlockSpec((tm, 512), row),
                  pl.BlockSpec((tm, 512), row), pl.BlockSpec((tm, D_MODEL), row),
                  full(wo), full(ln_g), full(ln_b),
                  full(w1), full(wq), full(wk), full(wv), full(qn), full(kn),
                  tab, tab, tab, tab, tab, tab],
        out_specs=[pl.BlockSpec((tm, D_MODEL), row),
                   pl.BlockSpec((tm, _QK_W), row), pl.BlockSpec((tm, n_kv), row),
                   pl.BlockSpec((tm, LANES), row), pl.BlockSpec((tm, n_kv), row),
                   pl.BlockSpec((tm, 1024), row)],
        out_shape=[jax.ShapeDtypeStruct((T, D_MODEL), F32),
                   jax.ShapeDtypeStruct((T, _QK_W), BF16), jax.ShapeDtypeStruct((T, n_kv), BF16),
                   jax.ShapeDtypeStruct((T, LANES), BF16), jax.ShapeDtypeStruct((T, n_kv), BF16),
                   jax.ShapeDtypeStruct((T, 1024), BF16)],
        compiler_params=_cparams(("parallel",)),
        name="even_out_odd_in_proj",
    )(onsa, nz, ogla, x2, wo, ln_g, ln_b, w1, wq, wk, wv, qn, kn, *q_tabs, *k_tabs)


_VT_ROWS = MLA_V + 16


def _mla_kernel(q_ref, kn_ref, kr_ref, v_ref, o_ref, kx_ref, qt_ref, vt_ref, s_ref, bias_ref, *, tq):
    nq = SEQ // tq
    a_idx = lax.broadcasted_iota(jnp.int32, (tq, tq), 0)
    b_idx = lax.broadcasted_iota(jnp.int32, (tq, tq), 1)
    bias_ref[...] = jnp.where(a_idx <= b_idx, 0.0, NEG_INF)
    lane = lax.broadcasted_iota(jnp.int32, (tq, LANES), 1)
    ones_rows = jnp.where(lax.broadcasted_iota(jnp.int32, (_VT_ROWS - MLA_V, tq), 0) == 0, 1.0, 0.0)

    for i in range(nq):
        rows = slice(i * tq, (i + 1) * tq)
        knp = kn_ref[rows, :].astype(F32)
        krb = kr_ref[rows, :].astype(F32)
        vt = v_ref[rows, :].astype(F32).T
        for h in range(2):
            kn_h = knp if h == 0 else pltpu.roll(knp, MLA_NOPE, 1)
            kx_ref[h, rows, :] = jnp.where(lane < MLA_NOPE, kn_h, krb).astype(BF16)
            qt_ref[h, :, rows] = q_ref[rows, h * _HEAD_W:(h + 1) * _HEAD_W].astype(F32).T.astype(BF16)
            vt_ref[h, :, rows] = jnp.concatenate(
                [vt[h * MLA_V:(h + 1) * MLA_V], ones_rows], axis=0).astype(BF16)

    units = [(i, h) for i in range(nq) for h in range(2)]
    n_buf = s_ref.shape[0]

    def stage1(n, ms):
        i, h = units[n]
        qt = qt_ref[h, :, i * tq:(i + 1) * tq]
        mx = None
        for j in range(i + 1):
            cols = slice(j * tq, (j + 1) * tq)
            s = _dot(kx_ref[h, cols, :], qt)
            if j == i:
                s = s + bias_ref[...]
            s_ref[n % n_buf, cols, :] = s
            t = s[0:8]
            for c in range(1, tq // 8):
                t = jnp.maximum(t, s[c * 8:(c + 1) * 8])
            mx = t if mx is None else jnp.maximum(mx, t)
            if j % _TILE_GROUP == _TILE_GROUP - 1:
                yield
        ms[n] = jnp.max(mx, axis=0, keepdims=True)

    def stage2(n, m, outs):
        i, h = units[n]
        acc = None
        for j in range(i + 1):
            cols = slice(j * tq, (j + 1) * tq)
            p = jnp.exp2(s_ref[n % n_buf, cols, :] - m).astype(BF16)
            pv = _dot(vt_ref[h, :, cols], p)
            acc = pv if acc is None else acc + pv
            if j % _TILE_GROUP == _TILE_GROUP - 1:
                yield
        outs.append(acc[0:MLA_V] / acc[MLA_V:MLA_V + 1])

    ms = {}
    _trace_round_robin([stage1(0, ms)])
    _trace_round_robin([stage1(1, ms)])
    outs = []
    for n in range(len(units)):
        gens = [stage2(n, ms[n], outs)]
        if n + 2 < len(units):
            gens.insert(0, stage1(n + 2, ms))
        _trace_round_robin(gens)
        if len(outs) == 2:
            i = units[n][0]
            o_ref[i * tq:(i + 1) * tq, :] = jnp.concatenate(outs, axis=0).T.astype(o_ref.dtype)
            outs = []


def _mla_attn(q, kn, kr, v, B, tq=256):
    kern = functools.partial(_mla_kernel, tq=tq)
    pair = pl.BlockSpec((SEQ, LANES), lambda b, h: (b, h))
    return pl.pallas_call(
        kern,
        grid=(B, MLA_HEADS // 2),
        in_specs=[pl.BlockSpec((SEQ, 2 * _HEAD_W), lambda b, h: (b, h)), pair,
                  pl.BlockSpec((SEQ, LANES), lambda b, h: (b, 0)), pair],
        out_specs=pair,
        out_shape=jax.ShapeDtypeStruct((B * SEQ, MLA_HEADS * MLA_V), BF16),
        scratch_shapes=[pltpu.VMEM((2, SEQ, _HEAD_W), BF16), pltpu.VMEM((2, _HEAD_W, SEQ), BF16),
                        pltpu.VMEM((2, _VT_ROWS, SEQ), BF16), pltpu.VMEM((4, SEQ, tq), F32),
                        pltpu.VMEM((tq, tq), F32)],
        compiler_params=_cparams(("parallel", "parallel")),
        name="mla_attn",
    )(q, kn, kr, v)


def _even_mixers(x2, B, w_in, cmp_pe, cmp_w1, cmp_w2, gla_w_gate, gla_b_gate, gla_norm):
    q, kvc, kv, ng, nz, gqk, gv, ga, gz = _even_in_proj(x2, _even_in_weight(w_in))

    pe_tab, w1_bd = _compress_weights(cmp_pe, cmp_w1)
    kc, vct = _compress(kvc, pe_tab, w1_bd, cmp_w2, B)

    ocmp_t, qaug_t = _cmp_attn(q, kc, vct, _overlap_t(), _q_alibi_table(), _head_select_mats(),
                               _cmp_bias_table(), ng, B)
    onsa = _sel_win_attn(qaug_t, kv, _nsa_k_tables(), ng, ocmp_t, B)

    wg = jnp.pad(gla_w_gate, ((0, LANES - GLA_GATE_RANK), (0, 0))).astype(BF16)
    ogla = _gla(gqk, gv, ga, gz, wg, gla_b_gate.reshape(1, -1), gla_norm.reshape(1, -1), B)
    return onsa, nz, ogla


def kernel(x, e_w_in, e_cmp_pe, e_cmp_w1, e_cmp_w2, e_gla_w_gate, e_gla_b_gate, e_gla_norm,
           e_w_out, e_ln_g, e_ln_b, o_w_in, o_q_norm, o_w_uq, o_kv_norm, o_w_ukv, o_w_out,
           o_ln_g, o_ln_b):
    assert DEPTH == 2
    B, S, D = x.shape
    x2 = x.reshape(B * S, D)
    onsa, nz, ogla = _even_mixers(x2, B, e_w_in[0], e_cmp_pe[0], e_cmp_w1[0], e_cmp_w2[0],
                                  e_gla_w_gate[0], e_gla_b_gate[0], e_gla_norm[0])
    w1, wq, wk, wv = _odd_weights(o_w_in[0], o_w_uq[0], o_w_ukv[0])
    q_tabs, k_tabs = _rope_tables()
    x1, q, kn, kr, v, z = _even_out_odd_in(
        onsa, nz, ogla, x2, e_w_out[0].astype(BF16), e_ln_g[0].reshape(1, -1), e_ln_b[0].reshape(1, -1),
        w1, wq, wk, wv, o_q_norm[0].reshape(1, -1), o_kv_norm[0].reshape(1, -1), q_tabs, k_tabs)
    o = _mla_attn(q, kn, kr, v, B)
    out = _odd_out(o, z, x1, o_w_out[0].astype(BF16), o_ln_g[0].reshape(1, -1), o_ln_b[0].reshape(1, -1))
    return out.reshape(B, S, D)
```

```python
import functools
import math

import jax
import jax.numpy as jnp
import numpy as np
from jax import lax
from jax.experimental import pallas as pl
from jax.experimental.pallas import tpu as pltpu

F32 = jnp.float32
BF16 = jnp.bfloat16

D_MODEL = 1024
SEQ = 2048
DEPTH = 2
NSA_HEADS = 8
NSA_KV_HEADS = 2
NSA_GROUP = 4
NSA_HEAD_DIM = 64
CMP_BLOCK = 32
CMP_STRIDE = 16
CMP_HIDDEN = 128
N_CMP = (SEQ - CMP_BLOCK) // CMP_STRIDE + 1
N_CMP_PAD = 128
SEL_BLOCK = 64
N_SEL = SEQ // SEL_BLOCK
SEL_TOPN = 8
WINDOW = 512
FORCE_BONUS = 1000.0
GLA_HEADS = 4
GLA_DK = 64
GLA_DV = 128
GLA_GATE_RANK = 16
GLA_TAU = 16.0
GLA_CHUNK = 64
MLA_HEADS = 16
MLA_NOPE = 64
MLA_ROPE = 32
MLA_V = 64
MLA_Q_LORA = 384
MLA_KV_LORA = 256
ROPE_THETA = 10000.0
NORM_EPS = 1e-5
NEG_INF = -1e30
DEEPNORM_ALPHA = (2 * DEPTH) ** 0.25

LANES = 128
MASK_BIG = 2.0 ** 100
VMEM_LIMIT = 56 * 1024 * 1024

_E_OFF = np.cumsum([0, 512, 768, 24, 512, 256, 256, 512, 16, 512])
_O_OFF = np.cumsum([0, MLA_Q_LORA, MLA_KV_LORA, MLA_ROPE, MLA_HEADS * MLA_V])


def _cparams(sem):
    return pltpu.CompilerParams(dimension_semantics=sem, vmem_limit_bytes=VMEM_LIMIT)


def _silu(x):
    return x * jax.nn.sigmoid(x)


def _dot(a, b):
    return jnp.dot(a, b, preferred_element_type=F32)


def _dot_nt(a, b):
    return lax.dot_general(a, b, (((1,), (1,)), ((), ())), preferred_element_type=F32)


_A_SEGS = (("q", 512, BF16), ("kvc", 256, F32), ("kv", 512, BF16), ("ng", 128, F32),
           ("nz", 512, BF16), ("gqk", 512, F32), ("gv", 512, BF16), ("ga", 128, F32),
           ("gz", 512, BF16))
_A_SILU = ("nz", "gz")
_A_OFF = np.cumsum([0] + [s[1] for s in _A_SEGS])


def _even_in_weight(w):
    o = _E_OFF
    w = w.astype(BF16)
    nkv = w[:, o[1]:o[2]]

    def kvcol(br, j, g):
        c = ((br * 2 + j) * NSA_KV_HEADS + g) * 64
        return nkv[:, c:c + 64]

    kv_blocks = [kvcol(br, j, g) for j in (0, 1) for g in (0, 1) for br in (1, 2)]
    gates = w[:, o[2]:o[3]].reshape(D_MODEL, 3, NSA_KV_HEADS, NSA_GROUP).transpose(0, 2, 1, 3)
    gates = jnp.pad(gates.reshape(D_MODEL, NSA_KV_HEADS, 12), ((0, 0), (0, 0), (0, 4)))
    ng = jnp.pad(gates.reshape(D_MODEL, 32), ((0, 0), (0, LANES - 32)))
    ga = jnp.pad(w[:, o[7]:o[8]], ((0, 0), (0, LANES - GLA_GATE_RANK)))
    cols = [w[:, o[0]:o[1]] * (NSA_HEAD_DIM ** -0.5), nkv[:, :256]] + kv_blocks + [
        ng, w[:, o[3]:o[4]], w[:, o[4]:o[6]], w[:, o[6]:o[7]], ga, w[:, o[8]:o[9]]]
    return jnp.concatenate(cols, axis=1)


def _nsa_k_tables():
    pos = np.arange(SEQ)
    k_al = np.zeros((SEQ, LANES), np.float32)
    k_al[:, 96] = 1.0
    k_al[:, 97] = 1.0
    k_al[:, 98] = pos // SEL_BLOCK
    k_al[:, 99] = pos % SEL_BLOCK
    k_sel = k_al.copy()
    k_sel[pos, 64 + pos // SEL_BLOCK] = 1.0
    return jnp.asarray(np.stack([k_sel, k_al]))


def _even_in_kernel(x_ref, w_ref, wg_ref, *out_refs):
    xb = x_ref[...].astype(BF16)
    for (name, width, dt), off, o_ref in zip(_A_SEGS, _A_OFF[:-1], out_refs):
        if name == "ng":
            o_ref[...] = _dot_nt(wg_ref[...], xb)
            continue
        r = _dot(xb, w_ref[:, off:off + width])
        if name in _A_SILU:
            r = _silu(r)
        o_ref[...] = r.astype(dt)


def _even_in_proj(x2, w_bf, tm=512):
    T = x2.shape[0]
    n_w = w_bf.shape[1]
    ng_off = _A_OFF[[n for n, _, _ in _A_SEGS].index("ng")]
    wg_t = w_bf[:, ng_off:ng_off + LANES].T
    out_shape = [jax.ShapeDtypeStruct((wd, T) if name == "ng" else (T, wd), dt)
                 for name, wd, dt in _A_SEGS]
    out_specs = [pl.BlockSpec((wd, tm), lambda i: (0, i)) if name == "ng"
                 else pl.BlockSpec((tm, wd), lambda i: (i, 0)) for name, wd, _ in _A_SEGS]
    return pl.pallas_call(
        _even_in_kernel,
        grid=(T // tm,),
        in_specs=[pl.BlockSpec((tm, D_MODEL), lambda i: (i, 0)),
                  pl.BlockSpec((D_MODEL, n_w), lambda i: (0, 0)),
                  pl.BlockSpec((LANES, D_MODEL), lambda i: (0, 0))],
        out_specs=out_specs,
        out_shape=out_shape,
        compiler_params=_cparams(("parallel",)),
        name="even_in_proj",
    )(x2, w_bf, wg_t)


def _compress_weights(cmp_pe, cmp_w1):
    pe_tab = jnp.concatenate([cmp_pe[0], cmp_pe[0], cmp_pe[1], cmp_pe[1]], axis=1)
    w = cmp_w1.astype(BF16).reshape(2, CMP_BLOCK, NSA_HEAD_DIM, CMP_HIDDEN)
    z = jnp.zeros_like(w)
    w_bd = jnp.concatenate([jnp.concatenate([w, z], axis=3), jnp.concatenate([z, w], axis=3)], axis=2)
    return pe_tab, w_bd


def _compress_kernel(xk_ref, xv_ref, pe_ref, w1_ref, w2_ref, w2t_ref, kc_ref, vct_ref):
    half = CMP_BLOCK // 2
    acc = [[None, None], [None, None]]
    for l in range(half):
        for j, x_ref in enumerate((xk_ref, xv_ref)):
            xs = x_ref[pl.ds(l, N_CMP_PAD, stride=CMP_STRIDE), :]
            for part in range(2):
                row = part * half + l
                xb = (xs + pe_ref[row:row + 1, j * LANES:(j + 1) * LANES]).astype(BF16)
                d = _dot(xb, w1_ref[j, row])
                acc[j][part] = d if acc[j][part] is None else acc[j][part] + d
    for j in range(2):
        h = acc[j][0] + pltpu.roll(acc[j][1], N_CMP_PAD - 1, 0)
        hs = _silu(h).astype(BF16)
        for g in range(NSA_KV_HEADS):
            hg = hs[:, g * CMP_HIDDEN:(g + 1) * CMP_HIDDEN]
            if j == 0:
                kc_ref[0, g] = _dot(hg, w2_ref[...])
            else:
                vct_ref[0, g] = _dot_nt(w2t_ref[...], hg)


def _compress(kvc, pe_tab, w1_bd, w2, B):
    w2k = w2[0].astype(BF16)
    w2vt = w2[1].T.astype(BF16)
    return pl.pallas_call(
        _compress_kernel,
        grid=(B,),
        in_specs=[pl.BlockSpec((SEQ, LANES), lambda b: (b, 0)),
                  pl.BlockSpec((SEQ, LANES), lambda b: (b, 1)),
                  pl.BlockSpec((CMP_BLOCK, 256), lambda b: (0, 0)),
                  pl.BlockSpec((2, CMP_BLOCK, LANES, 256), lambda b: (0, 0, 0, 0)),
                  pl.BlockSpec((CMP_HIDDEN, NSA_HEAD_DIM), lambda b: (0, 0)),
                  pl.BlockSpec((NSA_HEAD_DIM, CMP_HIDDEN), lambda b: (0, 0))],
        out_specs=[pl.BlockSpec((1, NSA_KV_HEADS, N_CMP_PAD, NSA_HEAD_DIM), lambda b: (b, 0, 0, 0)),
                   pl.BlockSpec((1, NSA_KV_HEADS, NSA_HEAD_DIM, N_CMP_PAD), lambda b: (b, 0, 0, 0))],
        out_shape=[jax.ShapeDtypeStruct((B, NSA_KV_HEADS, N_CMP_PAD, NSA_HEAD_DIM), F32),
                   jax.ShapeDtypeStruct((B, NSA_KV_HEADS, NSA_HEAD_DIM, N_CMP_PAD), F32)],
        compiler_params=_cparams(("parallel",)),
        name="nsa_compress",
    )(kvc, kvc, pe_tab, w1_bd, w2k, w2vt)


_TILE_GROUP = 2


def _trace_round_robin(gens):
    gens = list(gens)
    while gens:
        for g in list(gens):
            try:
                next(g)
            except StopIteration:
                gens.remove(g)


def _alibi_slopes_np():
    return np.exp2(-(8.0 / NSA_HEADS) * np.arange(1, NSA_HEADS + 1)).astype(np.float32)


def _overlap_t():
    cs = np.arange(N_CMP) * CMP_STRIDE
    ss = np.arange(N_SEL) * SEL_BLOCK
    ov = np.clip(np.minimum(cs[:, None] + CMP_BLOCK, ss[None, :] + SEL_BLOCK)
                 - np.maximum(cs[:, None], ss[None, :]), 0, None).astype(np.float32) / CMP_BLOCK
    ovt = np.zeros((N_SEL, N_CMP_PAD), np.float32)
    ovt[:, :N_CMP] = ov.T
    return jnp.asarray(ovt)


def _q_alibi_table():
    slopes = _alibi_slopes_np().reshape(NSA_KV_HEADS, NSA_GROUP)
    pos = np.arange(SEQ)
    tab = np.zeros((NSA_KV_HEADS, SEQ, LANES), np.float32)
    for g in range(NSA_KV_HEADS):
        for r in range(NSA_GROUP):
            m = slopes[g, r]
            tab[g, :, r * 32 + 0] = -m * SEL_BLOCK * (pos // SEL_BLOCK)
            tab[g, :, r * 32 + 1] = -m * (pos % SEL_BLOCK)
            tab[g, :, r * 32 + 2] = m * SEL_BLOCK
            tab[g, :, r * 32 + 3] = m
    return jnp.asarray(tab)


def _head_select_mats():
    m = np.zeros((NSA_GROUP, NSA_HEAD_DIM, 256), np.float32)
    for r in range(NSA_GROUP):
        m[r, np.arange(64), r * 64 + np.arange(64)] = 1.0
    return jnp.asarray(m, dtype=BF16)


_GATE_ROWS = 16


def _cmp_bias_table():
    slopes = jnp.asarray(_alibi_slopes_np()).reshape(NSA_KV_HEADS, NSA_GROUP, 1, 1)
    n = jnp.arange(N_CMP_PAD)[:, None]
    t = jnp.arange(SEQ)[None, :]
    dist = t - (n * CMP_STRIDE + (CMP_BLOCK - 1))
    visible = (dist >= 0) & (n < N_CMP)
    return jnp.where(visible, -slopes * dist.astype(F32), NEG_INF)


def _cmp_attn_kernel(q_ref, kc_ref, vct_ref, ovt_ref, qal_ref, hs_ref, bias_ref, gate_ref,
                     ocmp_ref, qaug_ref, *, tq):
    kc = kc_ref[0, 0].astype(BF16)
    vct = vct_ref[0, 0].astype(BF16)
    k_ext = [_dot(kc, hs_ref[r]).astype(BF16) for r in range(NSA_GROUP)]
    ri = lax.broadcasted_iota(jnp.int32, (256, 256), 0)
    ci = lax.broadcasted_iota(jnp.int32, (256, 256), 1)
    eye = jnp.where(ri == ci, 1.0, 0.0).astype(BF16)
    sub8 = lax.broadcasted_iota(jnp.int32, (8, tq), 0)
    j_idx = lax.broadcasted_iota(jnp.int32, (N_SEL, tq), 0)
    t_lane = lax.broadcasted_iota(jnp.int32, (N_SEL, tq), 1)
    n_grp = N_SEL // 8
    for i in range(SEQ // tq):
        cols = slice(i * tq, (i + 1) * tq)
        nv = min(N_CMP_PAD, (i + 1) * tq // CMP_STRIDE)
        q = q_ref[cols, :]
        lhs = jnp.concatenate([k[:nv] for k in k_ext] + [eye], axis=0)
        res = _dot_nt(lhs, q)
        q_t = res[NSA_GROUP * nv:]
        p_sum = None
        o_parts = []
        for r in range(NSA_GROUP):
            s = res[r * nv:(r + 1) * nv] + bias_ref[0, r, 0:nv, cols]
            m = jnp.max(s, axis=0, keepdims=True)
            e = jnp.exp(s - m)
            if i == 0:
                e = jnp.where(bias_ref[0, r, 0:nv, cols] > 0.5 * NEG_INF, e, 0.0)
            l = jnp.sum(e, axis=0, keepdims=True)
            p = e / jnp.where(l > 0.0, l, 1.0)
            p_sum = p if p_sum is None else p_sum + p
            gate = jax.nn.sigmoid(gate_ref[r:r + 1, cols])
            o_parts.append(_dot(vct[:, :nv], p.astype(BF16)) * gate)
        ocmp_ref[0, 0, :, cols] = jnp.concatenate(o_parts, axis=0).astype(ocmp_ref.dtype)

        imp = jnp.dot(ovt_ref[:, 0:nv], p_sum, preferred_element_type=F32,
                      precision=lax.Precision.HIGHEST)
        cur = (i * tq + t_lane) // SEL_BLOCK
        forced = (j_idx == 0) | (j_idx == cur) | (j_idx == cur - 1)
        imp = jnp.where(j_idx > cur, -1.0, imp + jnp.where(forced, FORCE_BONUS, 0.0))
        n_row = min(N_SEL, (i + 1) * tq // SEL_BLOCK)
        n_use = -(-n_row // 8)
        grp = [imp[8 * a:8 * (a + 1)] for a in range(n_use)]
        rank = [jnp.zeros((8, tq), F32) for _ in range(n_use)]
        for k in range(n_row):
            row = jnp.broadcast_to(imp[k:k + 1, :], (8, tq))
            for a in range(n_use):
                ge = jnp.where(row >= grp[a], 1.0, 0.0)
                gt = jnp.where(row > grp[a], 1.0, 0.0)
                if 8 * a > k:
                    cnt = ge
                elif 8 * a + 7 <= k:
                    cnt = gt
                else:
                    cnt = jnp.where(sub8 > k - 8 * a, ge, gt)
                rank[a] = rank[a] + cnt
        sel_bias = jnp.where(jnp.concatenate(rank, axis=0) < float(SEL_TOPN), 0.0, -MASK_BIG)
        if n_use < n_grp:
            sel_bias = jnp.concatenate(
                [sel_bias, jnp.full((8 * (n_grp - n_use), tq), -MASK_BIG, F32)], axis=0)

        qal_t = qal_ref[0, cols, :].T
        for r in range(NSA_GROUP):
            aug_t = jnp.concatenate([q_t[r * 64:(r + 1) * 64], sel_bias,
                                     qal_t[r * 32:(r + 1) * 32, :]], axis=0)
            qaug_ref[0, 0, r, :, cols] = aug_t.astype(BF16)


def _cmp_attn(q, kc, vct, ovt, qal, hs, bias, gates_t, B, tq=256):
    kern = functools.partial(_cmp_attn_kernel, tq=tq)
    return pl.pallas_call(
        kern,
        grid=(NSA_KV_HEADS, B),
        in_specs=[pl.BlockSpec((SEQ, 256), lambda g, b: (b, g)),
                  pl.BlockSpec((1, 1, N_CMP_PAD, NSA_HEAD_DIM), lambda g, b: (b, g, 0, 0)),
                  pl.BlockSpec((1, 1, NSA_HEAD_DIM, N_CMP_PAD), lambda g, b: (b, g, 0, 0)),
                  pl.BlockSpec((N_SEL, N_CMP_PAD), lambda g, b: (0, 0)),
                  pl.BlockSpec((1, SEQ, LANES), lambda g, b: (g, 0, 0)),
                  pl.BlockSpec((NSA_GROUP, NSA_HEAD_DIM, 256), lambda g, b: (0, 0, 0)),
                  pl.BlockSpec((1, NSA_GROUP, N_CMP_PAD, SEQ), lambda g, b: (g, 0, 0, 0)),
                  pl.BlockSpec((_GATE_ROWS, SEQ), lambda g, b: (g, b))],
        out_specs=[pl.BlockSpec((1, 1, 256, SEQ), lambda g, b: (b, g, 0, 0)),
                   pl.BlockSpec((1, 1, NSA_GROUP, LANES, SEQ), lambda g, b: (b, g, 0, 0, 0))],
        out_shape=[jax.ShapeDtypeStruct((B, NSA_KV_HEADS, 256, SEQ), BF16),
                   jax.ShapeDtypeStruct((B, NSA_KV_HEADS, NSA_GROUP, LANES, SEQ), BF16)],
        compiler_params=_cparams(("parallel", "parallel")),
        name="nsa_cmp_attn_select",
    )(q, kc, vct, ovt, qal, hs, bias, gates_t)


_NSA_VT_ROWS = NSA_HEAD_DIM + 16


def _sel_win_kernel(qaug_ref, k_ref, v_ref, ktab_ref, gate_ref, ocmp_ref,
                    o_ref, kx_ref, vt_ref, s_ref, bias_ref, *, tq):
    nq = SEQ // tq
    n_back = WINDOW // tq
    a_idx = lax.broadcasted_iota(jnp.int32, (tq, tq), 0)
    b_idx = lax.broadcasted_iota(jnp.int32, (tq, tq), 1)
    bias_ref[0] = jnp.where(a_idx <= b_idx, 0.0, -MASK_BIG)
    bias_ref[1] = jnp.where(a_idx > b_idx, 0.0, -MASK_BIG)
    lane = lax.broadcasted_iota(jnp.int32, (tq, LANES), 1)
    ones_rows = jnp.where(
        lax.broadcasted_iota(jnp.int32, (_NSA_VT_ROWS - NSA_HEAD_DIM, tq), 0) == 0, 1.0, 0.0)
    for i in range(nq):
        rows = slice(i * tq, (i + 1) * tq)
        kp = k_ref[rows, :].astype(F32)
        vt = v_ref[rows, :].astype(F32).T
        for br in range(2):
            k_br = kp if br == 0 else pltpu.roll(kp, NSA_HEAD_DIM, 1)
            kx_ref[br, rows, :] = jnp.where(lane < NSA_HEAD_DIM, k_br, ktab_ref[br, rows, :]).astype(BF16)
            vt_ref[br, :, rows] = jnp.concatenate(
                [vt[br * NSA_HEAD_DIM:(br + 1) * NSA_HEAD_DIM], ones_rows], axis=0).astype(BF16)

    def tiles_of(i, br):
        if br == 0:
            return [(j, None) for j in range(i)] + [(i, 0)]
        tl = [(i - d, 1 if d == n_back else None) for d in range(n_back, 0, -1) if i >= d]
        return tl + [(i, 0)]

    units = [(i, br) for i in range(nq) for br in range(2)]
    n_buf = s_ref.shape[0]

    def stage1(n):
        i, br = units[n]
        q = jnp.concatenate([qaug_ref[0, 0, r, :, i * tq:(i + 1) * tq] for r in range(NSA_GROUP)],
                            axis=1)
        mx = None
        for t_n, (j, bias) in enumerate(tiles_of(i, br)):
            s = _dot(kx_ref[br, j * tq:(j + 1) * tq, :], q)
            if bias is not None:
                s = s + jnp.concatenate([bias_ref[bias]] * NSA_GROUP, axis=1)
            s_ref[n % n_buf, t_n * tq:(t_n + 1) * tq, :] = s
            t = s[0:8]
            for c in range(1, tq // 8):
                t = jnp.maximum(t, s[c * 8:(c + 1) * 8])
            mx = t if mx is None else jnp.maximum(mx, t)
        return jnp.max(mx, axis=0, keepdims=True)

    def stage2(n, m):
        i, br = units[n]
        acc = None
        for t_n, (j, _) in enumerate(tiles_of(i, br)):
            p = jnp.exp(s_ref[n % n_buf, t_n * tq:(t_n + 1) * tq, :] - m).astype(BF16)
            pv = _dot(vt_ref[br, :, j * tq:(j + 1) * tq], p)
            acc = pv if acc is None else acc + pv
        cols = slice(i * tq, (i + 1) * tq)
        parts = []
        for r in range(NSA_GROUP):
            hl = slice(r * tq, (r + 1) * tq)
            g_row = 4 * (br + 1) + r
            scale = jax.nn.sigmoid(gate_ref[g_row:g_row + 1, cols]) / acc[NSA_HEAD_DIM:NSA_HEAD_DIM + 1, hl]
            parts.append(acc[0:NSA_HEAD_DIM, hl] * scale)
        return jnp.concatenate(parts, axis=0)

    ms = [stage1(0), stage1(1)]
    total = None
    for n in range(len(units)):
        if n + 2 < len(units):
            ms.append(stage1(n + 2))
        o_t = stage2(n, ms[n])
        i, br = units[n]
        cols = slice(i * tq, (i + 1) * tq)
        if br == 0:
            total = ocmp_ref[0, 0, :, cols].astype(F32) + o_t
        else:
            o_ref[cols, :] = (total + o_t).T.astype(o_ref.dtype)


def _sel_win_attn(qaug_t, kv, k_tabs, gates_t, ocmp_t, B, tq=256):
    kern = functools.partial(_sel_win_kernel, tq=tq)

    def kv_spec(col):
        return pl.BlockSpec((SEQ, LANES), lambda b, g: (b, col + g))

    return pl.pallas_call(
        kern,
        grid=(B, NSA_KV_HEADS),
        in_specs=[pl.BlockSpec((1, 1, NSA_GROUP, LANES, SEQ), lambda b, g: (b, g, 0, 0, 0)),
                  kv_spec(0), kv_spec(2),
                  pl.BlockSpec((2, SEQ, LANES), lambda b, g: (0, 0, 0)),
                  pl.BlockSpec((_GATE_ROWS, SEQ), lambda b, g: (g, b)),
                  pl.BlockSpec((1, 1, 256, SEQ), lambda b, g: (b, g, 0, 0))],
        out_specs=pl.BlockSpec((SEQ, 256), lambda b, g: (b, g)),
        out_shape=jax.ShapeDtypeStruct((B * SEQ, 512), BF16),
        scratch_shapes=[pltpu.VMEM((2, SEQ, LANES), BF16),
                        pltpu.VMEM((2, _NSA_VT_ROWS, SEQ), BF16),
                        pltpu.VMEM((3, SEQ, NSA_GROUP * tq), F32),
                        pltpu.VMEM((2, tq, tq), F32)],
        compiler_params=_cparams(("parallel", "parallel")),
        name="nsa_sel_win_attn",
    )(qaug_t, kv, kv, k_tabs, gates_t, ocmp_t)


_GLA_ROWS = 512


def _gla_kernel(qk_ref, v_ref, ga_ref, gz_ref, wg_ref, bg_ref, ng_ref, o_ref,
                b_ref, qd_ref, kd_ref):
    C = GLA_CHUNK
    KW = GLA_HEADS * GLA_DK
    wg = wg_ref[...]
    bg = bg_ref[...]
    norm_g = ng_ref[...]

    rin = lax.broadcasted_iota(jnp.int32, (_GLA_ROWS, KW), 0) % C
    lane = lax.broadcasted_iota(jnp.int32, (_GLA_ROWS, KW), 1)
    for blk in range(SEQ // _GLA_ROWS):
        rows = slice(blk * _GLA_ROWS, (blk + 1) * _GLA_ROWS)
        x = _dot(ga_ref[rows, :].astype(BF16), wg) + bg
        b = -(jnp.maximum(-x, 0.0) + jnp.log1p(jnp.exp(-jnp.abs(x)))) / GLA_TAU
        sh = 1
        while sh < C:
            b = b + jnp.where(rin >= sh, pltpu.roll(b, sh, 0), 0.0)
            sh *= 2
        b_ref[rows, :] = b
        q_d = qk_ref[rows, 0:KW] * (GLA_DK ** -0.5) * jnp.exp(b)
        for h in range(GLA_HEADS):
            in_head = (lane >= h * GLA_DK) & (lane < (h + 1) * GLA_DK)
            qd_ref[h, rows, :] = jnp.where(in_head, q_d, 0.0).astype(BF16)
        kd_ref[rows, :] = (qk_ref[rows, KW:2 * KW] * jnp.exp(-b)).astype(BF16)

    r_idx = lax.broadcasted_iota(jnp.int32, (GLA_HEADS * C, C), 0) % C
    c_idx = lax.broadcasted_iota(jnp.int32, (GLA_HEADS * C, C), 1)
    causal = r_idx >= c_idx

    def prep(n):
        rows = slice(n * C, (n + 1) * C)
        b = b_ref[rows, :]
        b_last = b_ref[n * C + C - 1:(n + 1) * C, :]
        k_u = (qk_ref[rows, KW:2 * KW] * jnp.exp(b_last - b)).astype(BF16)
        v = v_ref[rows, :]
        q_all = jnp.concatenate([qd_ref[h, rows, :] for h in range(GLA_HEADS)], axis=0)
        a = jnp.where(causal, _dot_nt(q_all, kd_ref[rows, :]), 0.0).astype(BF16)
        o_intra = [_dot(a[h * C:(h + 1) * C], v[:, h * GLA_DV:(h + 1) * GLA_DV])
                   for h in range(GLA_HEADS)]
        kv = lax.dot_general(k_u, v, (((0,), (0,)), ((), ())), preferred_element_type=F32)
        upd = jnp.concatenate([kv[h * GLA_DK:(h + 1) * GLA_DK, h * GLA_DV:(h + 1) * GLA_DV]
                               for h in range(GLA_HEADS)], axis=0)
        dec = jnp.exp(jnp.broadcast_to(b_last, (GLA_DV, KW)).T)
        return q_all, o_intra, upd, dec

    st = jnp.zeros((KW, GLA_DV), F32)
    n_chunks = SEQ // C
    nxt = prep(0)
    for n in range(n_chunks):
        q_all, o_intra, upd, dec = nxt
        if n + 1 < n_chunks:
            nxt = prep(n + 1)
        rows = slice(n * C, (n + 1) * C)
        o_inter = _dot(q_all, st.astype(BF16))
        for h in range(GLA_HEADS):
            vs = slice(h * GLA_DV, (h + 1) * GLA_DV)
            o = o_intra[h] + o_inter[h * C:(h + 1) * C]
            y = o * lax.rsqrt(jnp.mean(o * o, axis=-1, keepdims=True) + NORM_EPS) * norm_g
            y = y * gz_ref[rows, vs].astype(F32)
            o_ref[rows, vs] = y.astype(o_ref.dtype)
        st = st * dec + upd


def _gla(gqk, gv, ga, gz, wg, bg, norm_g, B):
    row = lambda b: (b, 0)
    const = lambda b: (0, 0)
    return pl.pallas_call(
        _gla_kernel,
        grid=(B,),
        in_specs=[pl.BlockSpec((SEQ, 512), row), pl.BlockSpec((SEQ, 512), row),
                  pl.BlockSpec((SEQ, LANES), row), pl.BlockSpec((SEQ, 512), row),
                  pl.BlockSpec((LANES, 256), const), pl.BlockSpec((1, 256), const),
                  pl.BlockSpec((1, GLA_DV), const)],
        out_specs=pl.BlockSpec((SEQ, 512), row),
        out_shape=jax.ShapeDtypeStruct((B * SEQ, 512), BF16),
        scratch_shapes=[pltpu.VMEM((SEQ, GLA_HEADS * GLA_DK), F32),
                        pltpu.VMEM((GLA_HEADS, SEQ, GLA_HEADS * GLA_DK), BF16),
                        pltpu.VMEM((SEQ, GLA_HEADS * GLA_DK), BF16)],
        compiler_params=_cparams(("parallel",)),
        name="gla_chunked",
    )(gqk, gv, ga, gz, wg, bg, norm_g)


def _deepnorm_ln(x, y, g, b):
    r = DEEPNORM_ALPHA * x + y
    mu = jnp.mean(r, axis=-1, keepdims=True)
    d = r - mu
    var = jnp.mean(d * d, axis=-1, keepdims=True)
    return d * lax.rsqrt(var + NORM_EPS) * g + b


def _odd_out_kernel(o_ref_in, z_ref, x_ref, w_ref, g_ref, b_ref, o_ref):
    o = (o_ref_in[...].astype(F32) * z_ref[...].astype(F32)).astype(BF16)
    y = _dot(o, w_ref[...])
    o_ref[...] = _deepnorm_ln(x_ref[...], y, g_ref[...], b_ref[...])


def _odd_out(o, z, x2, w_bf, ln_g, ln_b, tm=512):
    T = x2.shape[0]
    row = lambda i: (i, 0)
    const = lambda i: (0, 0)
    return pl.pallas_call(
        _odd_out_kernel,
        grid=(T // tm,),
        in_specs=[pl.BlockSpec((tm, D_MODEL), row), pl.BlockSpec((tm, D_MODEL), row),
                  pl.BlockSpec((tm, D_MODEL), row), pl.BlockSpec((D_MODEL, D_MODEL), const),
                  pl.BlockSpec((1, D_MODEL), const), pl.BlockSpec((1, D_MODEL), const)],
        out_specs=pl.BlockSpec((tm, D_MODEL), row),
        out_shape=jax.ShapeDtypeStruct((T, D_MODEL), F32),
        compiler_params=_cparams(("parallel",)),
        name="odd_out_proj_ln",
    )(o, z, x2, w_bf, ln_g, ln_b)


_HEAD_W = LANES
_QK_W = MLA_HEADS * _HEAD_W
_ROPE_HALF = MLA_ROPE // 2
_W1_KV0 = MLA_Q_LORA
_W1_KR0 = _W1_KV0 + MLA_KV_LORA
_W1_Z0 = _W1_KR0 + _HEAD_W
_Z_CHUNK = MLA_HEADS * MLA_V // 4


def _odd_weights(w_in, w_uq, w_ukv):
    o = _O_OFF
    w_in, w_uq, w_ukv = (w.astype(BF16) for w in (w_in, w_uq, w_ukv))
    z_nope = jnp.zeros((D_MODEL, MLA_NOPE), BF16)
    z_tail = jnp.zeros((D_MODEL, _HEAD_W - MLA_NOPE - MLA_ROPE), BF16)
    w1 = jnp.concatenate([w_in[:, o[0]:o[2]], z_nope, w_in[:, o[2]:o[3]], z_tail,
                          w_in[:, o[3]:o[4]]], axis=1)
    uq = w_uq.reshape(MLA_Q_LORA, MLA_HEADS, MLA_NOPE + MLA_ROPE)
    zq = jnp.zeros((MLA_Q_LORA, MLA_HEADS, _HEAD_W - MLA_NOPE - MLA_ROPE), BF16)
    wq = jnp.concatenate([uq, zq], axis=-1).reshape(MLA_Q_LORA, _QK_W)
    ukv = w_ukv.reshape(MLA_KV_LORA, MLA_HEADS, MLA_NOPE + MLA_V)
    wk = ukv[..., :MLA_NOPE].reshape(MLA_KV_LORA, MLA_HEADS * MLA_NOPE)
    wv = ukv[..., MLA_NOPE:].reshape(MLA_KV_LORA, MLA_HEADS * MLA_V)
    return w1, wq, wk, wv


def _rope_tables():
    freqs = jnp.exp(-math.log(ROPE_THETA) * jnp.arange(_ROPE_HALF, dtype=F32) * 2.0 / MLA_ROPE)
    ang = jnp.arange(SEQ, dtype=F32)[:, None] * freqs[None, :]
    cos, sin = jnp.cos(ang), jnp.sin(ang)
    z_half = jnp.zeros((SEQ, _ROPE_HALF), F32)
    z_tail = jnp.zeros((SEQ, _HEAD_W - MLA_NOPE - MLA_ROPE), F32)
    z_nope = jnp.zeros((SEQ, MLA_NOPE), F32)

    def tables(scale, nope_gain):
        nope = jnp.full((SEQ, MLA_NOPE), nope_gain, F32)
        c = jnp.concatenate([nope, cos * scale, cos * scale, z_tail], axis=1)
        s1 = jnp.concatenate([z_nope, -sin * scale, z_half, z_tail], axis=1)
        s2 = jnp.concatenate([z_nope, z_half, sin * scale, z_tail], axis=1)
        return c, s1, s2

    q_scale = (MLA_NOPE + MLA_ROPE) ** -0.5 * math.log2(math.e)
    return tables(q_scale, q_scale), tables(1.0, 0.0)


def _rope_block(x, c, s1, s2):
    return x * c + pltpu.roll(x, LANES - _ROPE_HALF, 1) * s1 + pltpu.roll(x, _ROPE_HALF, 1) * s2


def _rms(x, g):
    return x * lax.rsqrt(jnp.mean(x * x, axis=-1, keepdims=True) + NORM_EPS) * g


def _even_out_odd_in_kernel(onsa_ref, nz_ref, ogla_ref, x_ref, wo_ref, g_ref, b_ref,
                            w1_ref, wq_ref, wk_ref, wv_ref, qn_ref, kn_ref,
                            qc_ref, qs1_ref, qs2_ref, kc_ref, ks1_ref, ks2_ref,
                            x1_ref, q_ref, k_ref, kr_ref, v_ref, z_ref):
    tm = x_ref.shape[0]
    halves = [slice(0, tm // 2), slice(tm // 2, tm)]
    qc_all, qs1_all, qs2_all = qc_ref[...], qs1_ref[...], qs2_ref[...]

    def out_proj(r):
        o_nsa = (onsa_ref[r, :].astype(F32) * nz_ref[r, :].astype(F32)).astype(BF16)
        return _dot(o_nsa, wo_ref[0:512, :]) + _dot(ogla_ref[r, :], wo_ref[512:1024, :])

    def layer_norm(r, y):
        x1 = _deepnorm_ln(x_ref[r, :], y, g_ref[...], b_ref[...])
        x1_ref[r, :] = x1
        return x1.astype(BF16)

    def latents(r, xb):
        c_q = _dot(xb, w1_ref[:, 0:MLA_Q_LORA])
        c_kv = _dot(xb, w1_ref[:, _W1_KV0:_W1_KR0])
        kr = _dot(xb, w1_ref[:, _W1_KR0:_W1_Z0])
        z_ref[r, 0:2 * _Z_CHUNK] = _silu(_dot(xb, w1_ref[:, _W1_Z0:_W1_Z0 + 2 * _Z_CHUNK])).astype(BF16)
        return c_q, c_kv, kr

    def up_proj(r, xb, c_q, c_kv, kr):
        cqn = _rms(c_q, qn_ref[...]).astype(BF16)
        ckvn = _rms(c_kv, kn_ref[...]).astype(BF16)
        kr_ref[r, :] = _rope_block(kr, kc_ref[r, :], ks1_ref[r, :], ks2_ref[r, :]).astype(BF16)
        k_ref[r, :] = _dot(ckvn, wk_ref[...]).astype(BF16)
        v_ref[r, :] = _dot(ckvn, wv_ref[...]).astype(BF16)
        qc, qs1, qs2 = qc_all[r], qs1_all[r], qs2_all[r]
        grp = 4
        for g0 in range(0, MLA_HEADS, grp):
            qa = _dot(cqn, wq_ref[:, g0 * _HEAD_W:(g0 + grp) * _HEAD_W])
            for h in range(grp):
                sl = slice(h * _HEAD_W, (h + 1) * _HEAD_W)
                q_ref[r, (g0 + h) * _HEAD_W:(g0 + h + 1) * _HEAD_W] = (
                    _rope_block(qa[:, sl], qc, qs1, qs2).astype(BF16))
            if g0 >= 2 * grp:
                zc = slice((g0 // grp) * _Z_CHUNK, (g0 // grp + 1) * _Z_CHUNK)
                z_ref[r, zc] = _silu(
                    _dot(xb, w1_ref[:, _W1_Z0 + zc.start:_W1_Z0 + zc.stop])).astype(BF16)

    ra, rb = halves
    ya = out_proj(ra)
    yb = out_proj(rb)
    xa = layer_norm(ra, ya)
    la = latents(ra, xa)
    xbb = layer_norm(rb, yb)
    lb = latents(rb, xbb)
    up_proj(ra, xa, *la)
    up_proj(rb, xbb, *lb)


def _even_out_odd_in(onsa, nz, ogla, x2, wo, ln_g, ln_b, w1, wq, wk, wv, qn, kn, q_tabs, k_tabs,
                     tm=512):
    T = x2.shape[0]
    s_tiles = SEQ // tm
    row = lambda i: (i, 0)
    const = lambda i: (0, 0)
    pos = lambda i: (i % s_tiles, 0)

    def full(a):
        return pl.BlockSpec(a.shape, const)

    tab = pl.BlockSpec((tm, LANES), pos)
    n_kv = MLA_HEADS * MLA_NOPE
    return pl.pallas_call(
        _even_out_odd_in_kernel,
        grid=(T // tm,),
        in_specs=[pl.BlockSpec((tm, 512), row), pl.BlockSpec((tm, 512), row),
                  pl.BlockSpec((tm, 512), row), pl.BlockSpec((tm, D_MODEL), row),
                  full(wo), full(ln_g), full(ln_b),
                  full(w1), full(wq), full(wk), full(wv), full(qn), full(kn),
                  tab, tab, tab, tab, tab, tab],
        out_specs=[pl.BlockSpec((tm, D_MODEL), row),
                   pl.BlockSpec((tm, _QK_W), row), pl.BlockSpec((tm, n_kv), row),
                   pl.BlockSpec((tm, LANES), row), pl.BlockSpec((tm, n_kv), row),
                   pl.BlockSpec((tm, 1024), row)],
        out_shape=[jax.ShapeDtypeStruct((T, D_MODEL), F32),
                   jax.ShapeDtypeStruct((T, _QK_W), BF16), jax.ShapeDtypeStruct((T, n_kv), BF16),
                   jax.ShapeDtypeStruct((T, LANES), BF16), jax.ShapeDtypeStruct((T, n_kv), BF16),
                   jax.ShapeDtypeStruct((T, 1024), BF16)],
        compiler_params=_cparams(("parallel",)),
        name="even_out_odd_in_proj",
    )(onsa, nz, ogla, x2, wo, ln_g, ln_b, w1, wq, wk, wv, qn, kn, *q_tabs, *k_tabs)


_VT_ROWS = MLA_V + 16


def _mla_kernel(q_ref, kn_ref, kr_ref, v_ref, o_ref, kx_ref, qt_ref, vt_ref, s_ref, bias_ref, *, tq):
    nq = SEQ // tq
    a_idx = lax.broadcasted_iota(jnp.int32, (tq, tq), 0)
    b_idx = lax.broadcasted_iota(jnp.int32, (tq, tq), 1)
    bias_ref[...] = jnp.where(a_idx <= b_idx, 0.0, NEG_INF)
    lane = lax.broadcasted_iota(jnp.int32, (tq, LANES), 1)
    ones_rows = jnp.where(lax.broadcasted_iota(jnp.int32, (_VT_ROWS - MLA_V, tq), 0) == 0, 1.0, 0.0)

    for i in range(nq):
        rows = slice(i * tq, (i + 1) * tq)
        knp = kn_ref[rows, :].astype(F32)
        krb = kr_ref[rows, :].astype(F32)
        vt = v_ref[rows, :].astype(F32).T
        for h in range(2):
            kn_h = knp if h == 0 else pltpu.roll(knp, MLA_NOPE, 1)
            kx_ref[h, rows, :] = jnp.where(lane < MLA_NOPE, kn_h, krb).astype(BF16)
            qt_ref[h, :, rows] = q_ref[rows, h * _HEAD_W:(h + 1) * _HEAD_W].astype(F32).T.astype(BF16)
            vt_ref[h, :, rows] = jnp.concatenate(
                [vt[h * MLA_V:(h + 1) * MLA_V], ones_rows], axis=0).astype(BF16)

    units = [(i, h) for i in range(nq) for h in range(2)]
    n_buf = s_ref.shape[0]

    def stage1(n, ms):
        i, h = units[n]
        qt = qt_ref[h, :, i * tq:(i + 1) * tq]
        mx = None
        for j in range(i + 1):
            cols = slice(j * tq, (j + 1) * tq)
            s = _dot(kx_ref[h, cols, :], qt)
            if j == i:
                s = s + bias_ref[...]
            s_ref[n % n_buf, cols, :] = s
            t = s[0:8]
            for c in range(1, tq // 8):
                t = jnp.maximum(t, s[c * 8:(c + 1) * 8])
            mx = t if mx is None else jnp.maximum(mx, t)
            if j % _TILE_GROUP == _TILE_GROUP - 1:
                yield
        ms[n] = jnp.max(mx, axis=0, keepdims=True)

    def stage2(n, m, outs):
        i, h = units[n]
        acc = None
        for j in range(i + 1):
            cols = slice(j * tq, (j + 1) * tq)
            p = jnp.exp2(s_ref[n % n_buf, cols, :] - m).astype(BF16)
            pv = _dot(vt_ref[h, :, cols], p)
            acc = pv if acc is None else acc + pv
            if j % _TILE_GROUP == _TILE_GROUP - 1:
                yield
        outs.append(acc[0:MLA_V] / acc[MLA_V:MLA_V + 1])

    ms = {}
    _trace_round_robin([stage1(0, ms)])
    _trace_round_robin([stage1(1, ms)])
    outs = []
    for n in range(len(units)):
        gens = [stage2(n, ms[n], outs)]
        if n + 2 < len(units):
            gens.insert(0, stage1(n + 2, ms))
        _trace_round_robin(gens)
        if len(outs) == 2:
            i = units[n][0]
            o_ref[i * tq:(i + 1) * tq, :] = jnp.concatenate(outs, axis=0).T.astype(o_ref.dtype)
            outs = []


def _mla_attn(q, kn, kr, v, B, tq=256):
    kern = functools.partial(_mla_kernel, tq=tq)
    pair = pl.BlockSpec((SEQ, LANES), lambda b, h: (b, h))
    return pl.pallas_call(
        kern,
        grid=(B, MLA_HEADS // 2),
        in_specs=[pl.BlockSpec((SEQ, 2 * _HEAD_W), lambda b, h: (b, h)), pair,
                  pl.BlockSpec((SEQ, LANES), lambda b, h: (b, 0)), pair],
        out_specs=pair,
        out_shape=jax.ShapeDtypeStruct((B * SEQ, MLA_HEADS * MLA_V), BF16),
        scratch_shapes=[pltpu.VMEM((2, SEQ, _HEAD_W), BF16), pltpu.VMEM((2, _HEAD_W, SEQ), BF16),
                        pltpu.VMEM((2, _VT_ROWS, SEQ), BF16), pltpu.VMEM((4, SEQ, tq), F32),
                        pltpu.VMEM((tq, tq), F32)],
        compiler_params=_cparams(("parallel", "parallel")),
        name="mla_attn",
    )(q, kn, kr, v)


def _even_mixers(x2, B, w_in, cmp_pe, cmp_w1, cmp_w2, gla_w_gate, gla_b_gate, gla_norm):
    q, kvc, kv, ng, nz, gqk, gv, ga, gz = _even_in_proj(x2, _even_in_weight(w_in))

    pe_tab, w1_bd = _compress_weights(cmp_pe, cmp_w1)
    kc, vct = _compress(kvc, pe_tab, w1_bd, cmp_w2, B)

    ocmp_t, qaug_t = _cmp_attn(q, kc, vct, _overlap_t(), _q_alibi_table(), _head_select_mats(),
                               _cmp_bias_table(), ng, B)
    onsa = _sel_win_attn(qaug_t, kv, _nsa_k_tables(), ng, ocmp_t, B)

    wg = jnp.pad(gla_w_gate, ((0, LANES - GLA_GATE_RANK), (0, 0))).astype(BF16)
    ogla = _gla(gqk, gv, ga, gz, wg, gla_b_gate.reshape(1, -1), gla_norm.reshape(1, -1), B)
    return onsa, nz, ogla


def kernel(x, e_w_in, e_cmp_pe, e_cmp_w1, e_cmp_w2, e_gla_w_gate, e_gla_b_gate, e_gla_norm,
           e_w_out, e_ln_g, e_ln_b, o_w_in, o_q_norm, o_w_uq, o_kv_norm, o_w_ukv, o_w_out,
           o_ln_g, o_ln_b):
    assert DEPTH == 2
    B, S, D = x.shape
    x2 = x.reshape(B * S, D)
    onsa, nz, ogla = _even_mixers(x2, B, e_w_in[0], e_cmp_pe[0], e_cmp_w1[0], e_cmp_w2[0],
                                  e_gla_w_gate[0], e_gla_b_gate[0], e_gla_norm[0])
    w1, wq, wk, wv = _odd_weights(o_w_in[0], o_w_uq[0], o_w_ukv[0])
    q_tabs, k_tabs = _rope_tables()
    x1, q, kn, kr, v, z = _even_out_odd_in(
        onsa, nz, ogla, x2, e_w_out[0].astype(BF16), e_ln_g[0].reshape(1, -1), e_ln_b[0].reshape(1, -1),
        w1, wq, wk, wv, o_q_norm[0].reshape(1, -1), o_kv_norm[0].reshape(1, -1), q_tabs, k_tabs)
    o = _mla_attn(q, kn, kr, v, B)
    out = _odd_out(o, z, x1, o_w_out[0].astype(BF16), o_ln_g[0].reshape(1, -1), o_ln_b[0].reshape(1, -1))
    return out.reshape(B, S, D)
```

```python
import functools
import math

import jax
import jax.numpy as jnp
import numpy as np
from jax import lax
from jax.experimental import pallas as pl
from jax.experimental.pallas import tpu as pltpu

F32 = jnp.float32
BF16 = jnp.bfloat16

D_MODEL = 1024
SEQ = 2048
DEPTH = 2
NSA_HEADS = 8
NSA_KV_HEADS = 2
NSA_GROUP = 4
NSA_HEAD_DIM = 64
CMP_BLOCK = 32
CMP_STRIDE = 16
CMP_HIDDEN = 128
N_CMP = (SEQ - CMP_BLOCK) // CMP_STRIDE + 1
N_CMP_PAD = 128
SEL_BLOCK = 64
N_SEL = SEQ // SEL_BLOCK
SEL_TOPN = 8
WINDOW = 512
FORCE_BONUS = 1000.0
GLA_HEADS = 4
GLA_DK = 64
GLA_DV = 128
GLA_GATE_RANK = 16
GLA_TAU = 16.0
GLA_CHUNK = 64
MLA_HEADS = 16
MLA_NOPE = 64
MLA_ROPE = 32
MLA_V = 64
MLA_Q_LORA = 384
MLA_KV_LORA = 256
ROPE_THETA = 10000.0
NORM_EPS = 1e-5
NEG_INF = -1e30
DEEPNORM_ALPHA = (2 * DEPTH) ** 0.25

LANES = 128
MASK_BIG = 2.0 ** 100
VMEM_LIMIT = 56 * 1024 * 1024
_STREAM_BUFFERS = 3

_E_OFF = np.cumsum([0, 512, 768, 24, 512, 256, 256, 512, 16, 512])
_O_OFF = np.cumsum([0, MLA_Q_LORA, MLA_KV_LORA, MLA_ROPE, MLA_HEADS * MLA_V])


def _cparams(sem):
    return pltpu.CompilerParams(dimension_semantics=sem, vmem_limit_bytes=VMEM_LIMIT)


def _silu(x):
    return x * jax.nn.sigmoid(x)


def _dot(a, b):
    return jnp.dot(a, b, preferred_element_type=F32)


def _dot_nt(a, b):
    return lax.dot_general(a, b, (((1,), (1,)), ((), ())), preferred_element_type=F32)


_A_SEGS = (("q", 512, BF16), ("kvc", 256, F32), ("kv", 512, BF16), ("ng", 128, F32),
           ("nz", 512, BF16), ("gqk", 512, F32), ("gv", 512, BF16), ("ga", 128, F32),
           ("gz", 512, BF16))
_A_SILU = ("nz", "gz")
_A_OFF = np.cumsum([0] + [s[1] for s in _A_SEGS])


def _even_in_weight(w):
    o = _E_OFF
    w = w.astype(BF16)
    nkv = w[:, o[1]:o[2]]

    def kvcol(br, j, g):
        c = ((br * 2 + j) * NSA_KV_HEADS + g) * 64
        return nkv[:, c:c + 64]

    kv_blocks = [kvcol(br, j, g) for j in (0, 1) for g in (0, 1) for br in (1, 2)]
    gates = w[:, o[2]:o[3]].reshape(D_MODEL, 3, NSA_KV_HEADS, NSA_GROUP).transpose(0, 2, 1, 3)
    gates = jnp.pad(gates.reshape(D_MODEL, NSA_KV_HEADS, 12), ((0, 0), (0, 0), (0, 4)))
    ng = jnp.pad(gates.reshape(D_MODEL, 32), ((0, 0), (0, LANES - 32)))
    ga = jnp.pad(w[:, o[7]:o[8]], ((0, 0), (0, LANES - GLA_GATE_RANK)))
    cols = [w[:, o[0]:o[1]] * (NSA_HEAD_DIM ** -0.5), nkv[:, :256]] + kv_blocks + [
        ng, w[:, o[3]:o[4]], w[:, o[4]:o[6]], w[:, o[6]:o[7]], ga, w[:, o[8]:o[9]]]
    return jnp.concatenate(cols, axis=1)


def _nsa_k_tables():
    pos = np.arange(SEQ)
    k_al = np.zeros((SEQ, LANES), np.float32)
    k_al[:, 96] = 1.0
    k_al[:, 97] = 1.0
    k_al[:, 98] = pos // SEL_BLOCK
    k_al[:, 99] = pos % SEL_BLOCK
    k_sel = k_al.copy()
    k_sel[pos, 64 + pos // SEL_BLOCK] = 1.0
    return jnp.asarray(np.stack([k_sel, k_al]))


def _even_in_kernel(x_ref, w_ref, wg_ref, *out_refs):
    xb = x_ref[...].astype(BF16)
    for (name, width, dt), off, o_ref in zip(_A_SEGS, _A_OFF[:-1], out_refs):
        if name == "ng":
            o_ref[...] = _dot_nt(wg_ref[...], xb)
            continue
        r = _dot(xb, w_ref[:, off:off + width])
        if name in _A_SILU:
            r = _silu(r)
        o_ref[...] = r.astype(dt)


def _even_in_proj(x2, w_bf, tm=512):
    T = x2.shape[0]
    n_w = w_bf.shape[1]
    ng_off = _A_OFF[[n for n, _, _ in _A_SEGS].index("ng")]
    wg_t = w_bf[:, ng_off:ng_off + LANES].T
    out_shape = [jax.ShapeDtypeStruct((wd, T) if name == "ng" else (T, wd), dt)
                 for name, wd, dt in _A_SEGS]
    out_specs = [pl.BlockSpec((wd, tm), lambda i: (0, i)) if name == "ng"
                 else pl.BlockSpec((tm, wd), lambda i: (i, 0)) for name, wd, _ in _A_SEGS]
    return pl.pallas_call(
        _even_in_kernel,
        grid=(T // tm,),
        in_specs=[pl.BlockSpec((tm, D_MODEL), lambda i: (i, 0)),
                  pl.BlockSpec((D_MODEL, n_w), lambda i: (0, 0)),
                  pl.BlockSpec((LANES, D_MODEL), lambda i: (0, 0))],
        out_specs=out_specs,
        out_shape=out_shape,
        compiler_params=_cparams(("parallel",)),
        name="even_in_proj",
    )(x2, w_bf, wg_t)


def _compress_weights(cmp_pe, cmp_w1):
    pe_tab = jnp.concatenate([cmp_pe[0], cmp_pe[0], cmp_pe[1], cmp_pe[1]], axis=1)
    w = cmp_w1.astype(BF16).reshape(2, CMP_BLOCK, NSA_HEAD_DIM, CMP_HIDDEN)
    z = jnp.zeros_like(w)
    w_bd = jnp.concatenate([jnp.concatenate([w, z], axis=3), jnp.concatenate([z, w], axis=3)], axis=2)
    return pe_tab, w_bd


def _compress_kernel(xk_ref, xv_ref, pe_ref, w1_ref, w2_ref, w2t_ref, kc_ref, vct_ref):
    half = CMP_BLOCK // 2
    acc = [[None, None], [None, None]]
    for l in range(half):
        for j, x_ref in enumerate((xk_ref, xv_ref)):
            xs = x_ref[pl.ds(l, N_CMP_PAD, stride=CMP_STRIDE), :]
            for part in range(2):
                row = part * half + l
                xb = (xs + pe_ref[row:row + 1, j * LANES:(j + 1) * LANES]).astype(BF16)
                d = _dot(xb, w1_ref[j, row])
                acc[j][part] = d if acc[j][part] is None else acc[j][part] + d
    for j in range(2):
        h = acc[j][0] + pltpu.roll(acc[j][1], N_CMP_PAD - 1, 0)
        hs = _silu(h).astype(BF16)
        for g in range(NSA_KV_HEADS):
            hg = hs[:, g * CMP_HIDDEN:(g + 1) * CMP_HIDDEN]
            if j == 0:
                kc_ref[0, g] = _dot(hg, w2_ref[...])
            else:
                vct_ref[0, g] = _dot_nt(w2t_ref[...], hg)


def _compress(kvc, pe_tab, w1_bd, w2, B):
    w2k = w2[0].astype(BF16)
    w2vt = w2[1].T.astype(BF16)
    return pl.pallas_call(
        _compress_kernel,
        grid=(B,),
        in_specs=[pl.BlockSpec((SEQ, LANES), lambda b: (b, 0)),
                  pl.BlockSpec((SEQ, LANES), lambda b: (b, 1)),
                  pl.BlockSpec((CMP_BLOCK, 256), lambda b: (0, 0)),
                  pl.BlockSpec((2, CMP_BLOCK, LANES, 256), lambda b: (0, 0, 0, 0)),
                  pl.BlockSpec((CMP_HIDDEN, NSA_HEAD_DIM), lambda b: (0, 0)),
                  pl.BlockSpec((NSA_HEAD_DIM, CMP_HIDDEN), lambda b: (0, 0))],
        out_specs=[pl.BlockSpec((1, NSA_KV_HEADS, N_CMP_PAD, NSA_HEAD_DIM), lambda b: (b, 0, 0, 0)),
                   pl.BlockSpec((1, NSA_KV_HEADS, NSA_HEAD_DIM, N_CMP_PAD), lambda b: (b, 0, 0, 0))],
        out_shape=[jax.ShapeDtypeStruct((B, NSA_KV_HEADS, N_CMP_PAD, NSA_HEAD_DIM), F32),
                   jax.ShapeDtypeStruct((B, NSA_KV_HEADS, NSA_HEAD_DIM, N_CMP_PAD), F32)],
        compiler_params=_cparams(("parallel",)),
        name="nsa_compress",
    )(kvc, kvc, pe_tab, w1_bd, w2k, w2vt)


_TILE_GROUP = 2


def _trace_round_robin(gens):
    gens = list(gens)
    while gens:
        for g in list(gens):
            try:
                next(g)
            except StopIteration:
                gens.remove(g)


def _alibi_slopes_np():
    return np.exp2(-(8.0 / NSA_HEADS) * np.arange(1, NSA_HEADS + 1)).astype(np.float32)


def _overlap_t():
    cs = np.arange(N_CMP) * CMP_STRIDE
    ss = np.arange(N_SEL) * SEL_BLOCK
    ov = np.clip(np.minimum(cs[:, None] + CMP_BLOCK, ss[None, :] + SEL_BLOCK)
                 - np.maximum(cs[:, None], ss[None, :]), 0, None).astype(np.float32) / CMP_BLOCK
    ovt = np.zeros((N_SEL, N_CMP_PAD), np.float32)
    ovt[:, :N_CMP] = ov.T
    return jnp.asarray(ovt)


def _q_alibi_table():
    slopes = _alibi_slopes_np().reshape(NSA_KV_HEADS, NSA_GROUP)
    pos = np.arange(SEQ)
    tab = np.zeros((NSA_KV_HEADS, SEQ, LANES), np.float32)
    for g in range(NSA_KV_HEADS):
        for r in range(NSA_GROUP):
            m = slopes[g, r]
            tab[g, :, r * 32 + 0] = -m * SEL_BLOCK * (pos // SEL_BLOCK)
            tab[g, :, r * 32 + 1] = -m * (pos % SEL_BLOCK)
            tab[g, :, r * 32 + 2] = m * SEL_BLOCK
            tab[g, :, r * 32 + 3] = m
    return jnp.asarray(tab)


def _head_select_mats():
    m = np.zeros((NSA_GROUP, NSA_HEAD_DIM, 256), np.float32)
    for r in range(NSA_GROUP):
        m[r, np.arange(64), r * 64 + np.arange(64)] = 1.0
    return jnp.asarray(m, dtype=BF16)


_GATE_ROWS = 16


def _cmp_bias_table():
    slopes = jnp.asarray(_alibi_slopes_np()).reshape(NSA_KV_HEADS, NSA_GROUP, 1, 1)
    n = jnp.arange(N_CMP_PAD)[:, None]
    t = jnp.arange(SEQ)[None, :]
    dist = t - (n * CMP_STRIDE + (CMP_BLOCK - 1))
    visible = (dist >= 0) & (n < N_CMP)
    return jnp.where(visible, -slopes * dist.astype(F32), NEG_INF)


def _cmp_attn_kernel(q_ref, kc_ref, vct_ref, ovt_ref, qal_ref, hs_ref, bias_ref, gate_ref,
                     ocmp_ref, qaug_ref, *, tq):
    kc = kc_ref[0, 0].astype(BF16)
    vct = vct_ref[0, 0].astype(BF16)
    k_ext = [_dot(kc, hs_ref[r]).astype(BF16) for r in range(NSA_GROUP)]
    ri = lax.broadcasted_iota(jnp.int32, (256, 256), 0)
    ci = lax.broadcasted_iota(jnp.int32, (256, 256), 1)
    eye = jnp.where(ri == ci, 1.0, 0.0).astype(BF16)
    sub8 = lax.broadcasted_iota(jnp.int32, (8, tq), 0)
    j_idx = lax.broadcasted_iota(jnp.int32, (N_SEL, tq), 0)
    t_lane = lax.broadcasted_iota(jnp.int32, (N_SEL, tq), 1)
    n_grp = N_SEL // 8
    for i in range(SEQ // tq):
        cols = slice(i * tq, (i + 1) * tq)
        nv = min(N_CMP_PAD, (i + 1) * tq // CMP_STRIDE)
        q = q_ref[cols, :]
        lhs = jnp.concatenate([k[:nv] for k in k_ext] + [eye], axis=0)
        res = _dot_nt(lhs, q)
        q_t = res[NSA_GROUP * nv:]
        p_sum = None
        o_parts = []
        for r in range(NSA_GROUP):
            s = res[r * nv:(r + 1) * nv] + bias_ref[0, r, 0:nv, cols]
            m = jnp.max(s, axis=0, keepdims=True)
            e = jnp.exp(s - m)
            if i == 0:
                e = jnp.where(bias_ref[0, r, 0:nv, cols] > 0.5 * NEG_INF, e, 0.0)
            l = jnp.sum(e, axis=0, keepdims=True)
            p = e / jnp.where(l > 0.0, l, 1.0)
            p_sum = p if p_sum is None else p_sum + p
            gate = jax.nn.sigmoid(gate_ref[r:r + 1, cols])
            o_parts.append(_dot(vct[:, :nv], p.astype(BF16)) * gate)
        ocmp_ref[0, 0, :, cols] = jnp.concatenate(o_parts, axis=0).astype(ocmp_ref.dtype)

        imp = jnp.dot(ovt_ref[:, 0:nv], p_sum, preferred_element_type=F32,
                      precision=lax.Precision.HIGHEST)
        cur = (i * tq + t_lane) // SEL_BLOCK
        forced = (j_idx == 0) | (j_idx == cur) | (j_idx == cur - 1)
        imp = jnp.where(j_idx > cur, -1.0, imp + jnp.where(forced, FORCE_BONUS, 0.0))
        n_row = min(N_SEL, (i + 1) * tq // SEL_BLOCK)
        n_use = -(-n_row // 8)
        grp = [imp[8 * a:8 * (a + 1)] for a in range(n_use)]
        rank = [jnp.zeros((8, tq), F32) for _ in range(n_use)]
        for k in range(n_row):
            row = jnp.broadcast_to(imp[k:k + 1, :], (8, tq))
            for a in range(n_use):
                ge = jnp.where(row >= grp[a], 1.0, 0.0)
                gt = jnp.where(row > grp[a], 1.0, 0.0)
                if 8 * a > k:
                    cnt = ge
                elif 8 * a + 7 <= k:
                    cnt = gt
                else:
                    cnt = jnp.where(sub8 > k - 8 * a, ge, gt)
                rank[a] = rank[a] + cnt
        sel_bias = jnp.where(jnp.concatenate(rank, axis=0) < float(SEL_TOPN), 0.0, -MASK_BIG)
        if n_use < n_grp:
            sel_bias = jnp.concatenate(
                [sel_bias, jnp.full((8 * (n_grp - n_use), tq), -MASK_BIG, F32)], axis=0)

        qal_t = qal_ref[0, cols, :].T
        for r in range(NSA_GROUP):
            aug_t = jnp.concatenate([q_t[r * 64:(r + 1) * 64], sel_bias,
                                     qal_t[r * 32:(r + 1) * 32, :]], axis=0)
            qaug_ref[0, 0, r, :, cols] = aug_t.astype(BF16)


def _cmp_attn(q, kc, vct, ovt, qal, hs, bias, gates_t, B, tq=256):
    kern = functools.partial(_cmp_attn_kernel, tq=tq)
    return pl.pallas_call(
        kern,
        grid=(NSA_KV_HEADS, B),
        in_specs=[pl.BlockSpec((SEQ, 256), lambda g, b: (b, g)),
                  pl.BlockSpec((1, 1, N_CMP_PAD, NSA_HEAD_DIM), lambda g, b: (b, g, 0, 0)),
                  pl.BlockSpec((1, 1, NSA_HEAD_DIM, N_CMP_PAD), lambda g, b: (b, g, 0, 0)),
                  pl.BlockSpec((N_SEL, N_CMP_PAD), lambda g, b: (0, 0)),
                  pl.BlockSpec((1, SEQ, LANES), lambda g, b: (g, 0, 0)),
                  pl.BlockSpec((NSA_GROUP, NSA_HEAD_DIM, 256), lambda g, b: (0, 0, 0)),
                  pl.BlockSpec((1, NSA_GROUP, N_CMP_PAD, SEQ), lambda g, b: (g, 0, 0, 0)),
                  pl.BlockSpec((_GATE_ROWS, SEQ), lambda g, b: (g, b))],
        out_specs=[pl.BlockSpec((1, 1, 256, SEQ), lambda g, b: (b, g, 0, 0)),
                   pl.BlockSpec((1, 1, NSA_GROUP, LANES, SEQ), lambda g, b: (b, g, 0, 0, 0))],
        out_shape=[jax.ShapeDtypeStruct((B, NSA_KV_HEADS, 256, SEQ), BF16),
                   jax.ShapeDtypeStruct((B, NSA_KV_HEADS, NSA_GROUP, LANES, SEQ), BF16)],
        compiler_params=_cparams(("parallel", "parallel")),
        name="nsa_cmp_attn_select",
    )(q, kc, vct, ovt, qal, hs, bias, gates_t)


_NSA_VT_ROWS = NSA_HEAD_DIM + 16


def _sel_win_kernel(qaug_ref, k_ref, v_ref, ktab_ref, gate_ref, ocmp_ref,
                    o_ref, kx_ref, vt_ref, s_ref, bias_ref, *, tq):
    nq = SEQ // tq
    n_back = WINDOW // tq
    a_idx = lax.broadcasted_iota(jnp.int32, (tq, tq), 0)
    b_idx = lax.broadcasted_iota(jnp.int32, (tq, tq), 1)
    bias_ref[0] = jnp.where(a_idx <= b_idx, 0.0, -MASK_BIG)
    bias_ref[1] = jnp.where(a_idx > b_idx, 0.0, -MASK_BIG)
    lane = lax.broadcasted_iota(jnp.int32, (tq, LANES), 1)
    ones_rows = jnp.where(
        lax.broadcasted_iota(jnp.int32, (_NSA_VT_ROWS - NSA_HEAD_DIM, tq), 0) == 0, 1.0, 0.0)
    for i in range(nq):
        rows = slice(i * tq, (i + 1) * tq)
        kp = k_ref[rows, :].astype(F32)
        vt = v_ref[rows, :].astype(F32).T
        for br in range(2):
            k_br = kp if br == 0 else pltpu.roll(kp, NSA_HEAD_DIM, 1)
            kx_ref[br, rows, :] = jnp.where(lane < NSA_HEAD_DIM, k_br, ktab_ref[br, rows, :]).astype(BF16)
            vt_ref[br, :, rows] = jnp.concatenate(
                [vt[br * NSA_HEAD_DIM:(br + 1) * NSA_HEAD_DIM], ones_rows], axis=0).astype(BF16)

    def tiles_of(i, br):
        if br == 0:
            return [(j, None) for j in range(i)] + [(i, 0)]
        tl = [(i - d, 1 if d == n_back else None) for d in range(n_back, 0, -1) if i >= d]
        return tl + [(i, 0)]

    units = [(i, br) for i in range(nq) for br in range(2)]
    n_buf = s_ref.shape[0]

    def stage1(n):
        i, br = units[n]
        q = jnp.concatenate([qaug_ref[0, 0, r, :, i * tq:(i + 1) * tq] for r in range(NSA_GROUP)],
                            axis=1)
        mx = None
        for t_n, (j, bias) in enumerate(tiles_of(i, br)):
            s = _dot(kx_ref[br, j * tq:(j + 1) * tq, :], q)
            if bias is not None:
                s = s + jnp.concatenate([bias_ref[bias]] * NSA_GROUP, axis=1)
            s_ref[n % n_buf, t_n * tq:(t_n + 1) * tq, :] = s
            t = s[0:8]
            for c in range(1, tq // 8):
                t = jnp.maximum(t, s[c * 8:(c + 1) * 8])
            mx = t if mx is None else jnp.maximum(mx, t)
        return jnp.max(mx, axis=0, keepdims=True)

    def stage2(n, m):
        i, br = units[n]
        acc = None
        for t_n, (j, _) in enumerate(tiles_of(i, br)):
            p = jnp.exp(s_ref[n % n_buf, t_n * tq:(t_n + 1) * tq, :] - m).astype(BF16)
            pv = _dot(vt_ref[br, :, j * tq:(j + 1) * tq], p)
            acc = pv if acc is None else acc + pv
        cols = slice(i * tq, (i + 1) * tq)
        parts = []
        for r in range(NSA_GROUP):
            hl = slice(r * tq, (r + 1) * tq)
            g_row = 4 * (br + 1) + r
            scale = jax.nn.sigmoid(gate_ref[g_row:g_row + 1, cols]) / acc[NSA_HEAD_DIM:NSA_HEAD_DIM + 1, hl]
            parts.append(acc[0:NSA_HEAD_DIM, hl] * scale)
        return jnp.concatenate(parts, axis=0)

    ms = [stage1(0), stage1(1)]
    total = None
    for n in range(len(units)):
        if n + 2 < len(units):
            ms.append(stage1(n + 2))
        o_t = stage2(n, ms[n])
        i, br = units[n]
        cols = slice(i * tq, (i + 1) * tq)
        if br == 0:
            total = ocmp_ref[0, 0, :, cols].astype(F32) + o_t
        else:
            o_ref[cols, :] = (total + o_t).T.astype(o_ref.dtype)


def _sel_win_attn(qaug_t, kv, k_tabs, gates_t, ocmp_t, B, tq=256):
    kern = functools.partial(_sel_win_kernel, tq=tq)

    def kv_spec(col):
        return pl.BlockSpec((SEQ, LANES), lambda b, g: (b, col + g))

    return pl.pallas_call(
        kern,
        grid=(B, NSA_KV_HEADS),
        in_specs=[pl.BlockSpec((1, 1, NSA_GROUP, LANES, SEQ), lambda b, g: (b, g, 0, 0, 0)),
                  kv_spec(0), kv_spec(2),
                  pl.BlockSpec((2, SEQ, LANES), lambda b, g: (0, 0, 0)),
                  pl.BlockSpec((_GATE_ROWS, SEQ), lambda b, g: (g, b)),
                  pl.BlockSpec((1, 1, 256, SEQ), lambda b, g: (b, g, 0, 0))],
        out_specs=pl.BlockSpec((SEQ, 256), lambda b, g: (b, g)),
        out_shape=jax.ShapeDtypeStruct((B * SEQ, 512), BF16),
        scratch_shapes=[pltpu.VMEM((2, SEQ, LANES), BF16),
                        pltpu.VMEM((2, _NSA_VT_ROWS, SEQ), BF16),
                        pltpu.VMEM((3, SEQ, NSA_GROUP * tq), F32),
                        pltpu.VMEM((2, tq, tq), F32)],
        compiler_params=_cparams(("parallel", "parallel")),
        name="nsa_sel_win_attn",
    )(qaug_t, kv, kv, k_tabs, gates_t, ocmp_t)


_GLA_ROWS = 512


def _gla_kernel(qk_ref, v_ref, ga_ref, gz_ref, wg_ref, bg_ref, ng_ref, o_ref,
                b_ref, qd_ref, kd_ref):
    C = GLA_CHUNK
    KW = GLA_HEADS * GLA_DK
    wg = wg_ref[...]
    bg = bg_ref[...]
    norm_g = ng_ref[...]

    rin = lax.broadcasted_iota(jnp.int32, (_GLA_ROWS, KW), 0) % C
    lane = lax.broadcasted_iota(jnp.int32, (_GLA_ROWS, KW), 1)
    for blk in range(SEQ // _GLA_ROWS):
        rows = slice(blk * _GLA_ROWS, (blk + 1) * _GLA_ROWS)
        x = _dot(ga_ref[rows, :].astype(BF16), wg) + bg
        b = -(jnp.maximum(-x, 0.0) + jnp.log1p(jnp.exp(-jnp.abs(x)))) / GLA_TAU
        sh = 1
        while sh < C:
            b = b + jnp.where(rin >= sh, pltpu.roll(b, sh, 0), 0.0)
            sh *= 2
        b_ref[rows, :] = b
        q_d = qk_ref[rows, 0:KW] * (GLA_DK ** -0.5) * jnp.exp(b)
        for h in range(GLA_HEADS):
            in_head = (lane >= h * GLA_DK) & (lane < (h + 1) * GLA_DK)
            qd_ref[h, rows, :] = jnp.where(in_head, q_d, 0.0).astype(BF16)
        kd_ref[rows, :] = (qk_ref[rows, KW:2 * KW] * jnp.exp(-b)).astype(BF16)

    r_idx = lax.broadcasted_iota(jnp.int32, (GLA_HEADS * C, C), 0) % C
    c_idx = lax.broadcasted_iota(jnp.int32, (GLA_HEADS * C, C), 1)
    causal = r_idx >= c_idx

    def prep(n):
        rows = slice(n * C, (n + 1) * C)
        b = b_ref[rows, :]
        b_last = b_ref[n * C + C - 1:(n + 1) * C, :]
        k_u = (qk_ref[rows, KW:2 * KW] * jnp.exp(b_last - b)).astype(BF16)
        v = v_ref[rows, :]
        q_all = jnp.concatenate([qd_ref[h, rows, :] for h in range(GLA_HEADS)], axis=0)
        a = jnp.where(causal, _dot_nt(q_all, kd_ref[rows, :]), 0.0).astype(BF16)
        o_intra = [_dot(a[h * C:(h + 1) * C], v[:, h * GLA_DV:(h + 1) * GLA_DV])
                   for h in range(GLA_HEADS)]
        kv = lax.dot_general(k_u, v, (((0,), (0,)), ((), ())), preferred_element_type=F32)
        upd = jnp.concatenate([kv[h * GLA_DK:(h + 1) * GLA_DK, h * GLA_DV:(h + 1) * GLA_DV]
                               for h in range(GLA_HEADS)], axis=0)
        dec = jnp.exp(jnp.broadcast_to(b_last, (GLA_DV, KW)).T)
        return q_all, o_intra, upd, dec

    st = jnp.zeros((KW, GLA_DV), F32)
    n_chunks = SEQ // C
    nxt = prep(0)
    for n in range(n_chunks):
        q_all, o_intra, upd, dec = nxt
        if n + 1 < n_chunks:
            nxt = prep(n + 1)
        rows = slice(n * C, (n + 1) * C)
        o_inter = _dot(q_all, st.astype(BF16))
        for h in range(GLA_HEADS):
            vs = slice(h * GLA_DV, (h + 1) * GLA_DV)
            o = o_intra[h] + o_inter[h * C:(h + 1) * C]
            y = o * lax.rsqrt(jnp.mean(o * o, axis=-1, keepdims=True) + NORM_EPS) * norm_g
            y = y * gz_ref[rows, vs].astype(F32)
            o_ref[rows, vs] = y.astype(o_ref.dtype)
        st = st * dec + upd


def _gla(gqk, gv, ga, gz, wg, bg, norm_g, B):
    row = lambda b: (b, 0)
    const = lambda b: (0, 0)
    return pl.pallas_call(
        _gla_kernel,
        grid=(B,),
        in_specs=[pl.BlockSpec((SEQ, 512), row), pl.BlockSpec((SEQ, 512), row),
                  pl.BlockSpec((SEQ, LANES), row), pl.BlockSpec((SEQ, 512), row),
                  pl.BlockSpec((LANES, 256), const), pl.BlockSpec((1, 256), const),
                  pl.BlockSpec((1, GLA_DV), const)],
        out_specs=pl.BlockSpec((SEQ, 512), row),
        out_shape=jax.ShapeDtypeStruct((B * SEQ, 512), BF16),
        scratch_shapes=[pltpu.VMEM((SEQ, GLA_HEADS * GLA_DK), F32),
                        pltpu.VMEM((GLA_HEADS, SEQ, GLA_HEADS * GLA_DK), BF16),
                        pltpu.VMEM((SEQ, GLA_HEADS * GLA_DK), BF16)],
        compiler_params=_cparams(("parallel",)),
        name="gla_chunked",
    )(gqk, gv, ga, gz, wg, bg, norm_g)


def _deepnorm_ln(x, y, g, b):
    r = DEEPNORM_ALPHA * x + y
    mu = jnp.mean(r, axis=-1, keepdims=True)
    d = r - mu
    var = jnp.mean(d * d, axis=-1, keepdims=True)
    return d * lax.rsqrt(var + NORM_EPS) * g + b


def _odd_out_kernel(o_ref_in, z_ref, x_ref, w_ref, g_ref, b_ref, o_ref):
    tm = x_ref.shape[0]
    halves = (slice(0, tm // 2), slice(tm // 2, tm))
    ys = [_dot((o_ref_in[r, :].astype(F32) * z_ref[r, :].astype(F32)).astype(BF16), w_ref[...])
          for r in halves]
    for r, y in zip(halves, ys):
        o_ref[r, :] = _deepnorm_ln(x_ref[r, :], y, g_ref[...], b_ref[...])


def _odd_out(o, z, x2, w_bf, ln_g, ln_b, tm=512):
    T = x2.shape[0]
    row = lambda i: (i, 0)
    deep = pl.Buffered(_STREAM_BUFFERS)

    def outer(o_hbm, z_hbm, x_hbm, w_ref, g_ref, b_ref, out_hbm):
        def tile(o_blk, z_blk, x_blk, out_blk):
            _odd_out_kernel(o_blk, z_blk, x_blk, w_ref, g_ref, b_ref, out_blk)

        pltpu.emit_pipeline(
            tile,
            grid=(T // tm,),
            in_specs=[pl.BlockSpec((tm, D_MODEL), row, pipeline_mode=deep),
                      pl.BlockSpec((tm, D_MODEL), row, pipeline_mode=deep),
                      pl.BlockSpec((tm, D_MODEL), row, pipeline_mode=deep)],
            out_specs=[pl.BlockSpec((tm, D_MODEL), row)],
        )(o_hbm, z_hbm, x_hbm, out_hbm)

    hbm = pl.BlockSpec(memory_space=pl.ANY)
    vmem = pl.BlockSpec(memory_space=pltpu.VMEM)
    return pl.pallas_call(
        outer,
        in_specs=[hbm, hbm, hbm, vmem, vmem, vmem],
        out_specs=hbm,
        out_shape=jax.ShapeDtypeStruct((T, D_MODEL), F32),
        compiler_params=pltpu.CompilerParams(vmem_limit_bytes=VMEM_LIMIT),
        name="odd_out_proj_ln",
    )(o, z, x2, w_bf, ln_g, ln_b)


_HEAD_W = LANES
_QK_W = MLA_HEADS * _HEAD_W
_ROPE_HALF = MLA_ROPE // 2
_W1_KV0 = MLA_Q_LORA
_W1_KR0 = _W1_KV0 + MLA_KV_LORA
_W1_Z0 = _W1_KR0 + _HEAD_W
_Z_CHUNK = MLA_HEADS * MLA_V // 4


def _odd_weights(w_in, w_uq, w_ukv):
    o = _O_OFF
    w_in, w_uq, w_ukv = (w.astype(BF16) for w in (w_in, w_uq, w_ukv))
    z_nope = jnp.zeros((D_MODEL, MLA_NOPE), BF16)
    z_tail = jnp.zeros((D_MODEL, _HEAD_W - MLA_NOPE - MLA_ROPE), BF16)
    w1 = jnp.concatenate([w_in[:, o[0]:o[2]], z_nope, w_in[:, o[2]:o[3]], z_tail,
                          w_in[:, o[3]:o[4]]], axis=1)
    uq = w_uq.reshape(MLA_Q_LORA, MLA_HEADS, MLA_NOPE + MLA_ROPE)
    zq = jnp.zeros((MLA_Q_LORA, MLA_HEADS, _HEAD_W - MLA_NOPE - MLA_ROPE), BF16)
    wq = jnp.concatenate([uq, zq], axis=-1).reshape(MLA_Q_LORA, _QK_W)
    ukv = w_ukv.reshape(MLA_KV_LORA, MLA_HEADS, MLA_NOPE + MLA_V)
    wk = ukv[..., :MLA_NOPE].reshape(MLA_KV_LORA, MLA_HEADS * MLA_NOPE)
    wv = ukv[..., MLA_NOPE:].reshape(MLA_KV_LORA, MLA_HEADS * MLA_V)
    return w1, wq, wk, wv


def _rope_tables():
    freqs = jnp.exp(-math.log(ROPE_THETA) * jnp.arange(_ROPE_HALF, dtype=F32) * 2.0 / MLA_ROPE)
    ang = jnp.arange(SEQ, dtype=F32)[:, None] * freqs[None, :]
    cos, sin = jnp.cos(ang), jnp.sin(ang)
    z_half = jnp.zeros((SEQ, _ROPE_HALF), F32)
    z_tail = jnp.zeros((SEQ, _HEAD_W - MLA_NOPE - MLA_ROPE), F32)
    z_nope = jnp.zeros((SEQ, MLA_NOPE), F32)

    def tables(scale, nope_gain):
        nope = jnp.full((SEQ, MLA_NOPE), nope_gain, F32)
        c = jnp.concatenate([nope, cos * scale, cos * scale, z_tail], axis=1)
        s1 = jnp.concatenate([z_nope, -sin * scale, z_half, z_tail], axis=1)
        s2 = jnp.concatenate([z_nope, z_half, sin * scale, z_tail], axis=1)
        return c, s1, s2

    q_scale = (MLA_NOPE + MLA_ROPE) ** -0.5 * math.log2(math.e)
    return tables(q_scale, q_scale), tables(1.0, 0.0)


def _rope_block(x, c, s1, s2):
    return x * c + pltpu.roll(x, LANES - _ROPE_HALF, 1) * s1 + pltpu.roll(x, _ROPE_HALF, 1) * s2


def _rms(x, g):
    return x * lax.rsqrt(jnp.mean(x * x, axis=-1, keepdims=True) + NORM_EPS) * g


def _even_out_odd_in_kernel(onsa_ref, nz_ref, ogla_ref, x_ref, wo_ref, g_ref, b_ref,
                            w1_ref, wq_ref, wk_ref, wv_ref, qn_ref, kn_ref,
                            qc_ref, qs1_ref, qs2_ref, kc_ref, ks1_ref, ks2_ref,
                            x1_ref, q_ref, k_ref, kr_ref, v_ref, z_ref):
    tm = x_ref.shape[0]
    halves = [slice(0, tm // 2), slice(tm // 2, tm)]
    qc_all, qs1_all, qs2_all = qc_ref[...], qs1_ref[...], qs2_ref[...]

    def out_proj(r):
        o_nsa = (onsa_ref[r, :].astype(F32) * nz_ref[r, :].astype(F32)).astype(BF16)
        return _dot(o_nsa, wo_ref[0:512, :]) + _dot(ogla_ref[r, :], wo_ref[512:1024, :])

    def layer_norm(r, y):
        x1 = _deepnorm_ln(x_ref[r, :], y, g_ref[...], b_ref[...])
        x1_ref[r, :] = x1
        return x1.astype(BF16)

    def latents(r, xb):
        c_q = _dot(xb, w1_ref[:, 0:MLA_Q_LORA])
        c_kv = _dot(xb, w1_ref[:, _W1_KV0:_W1_KR0])
        kr = _dot(xb, w1_ref[:, _W1_KR0:_W1_Z0])
        z_ref[r, 0:2 * _Z_CHUNK] = _silu(_dot(xb, w1_ref[:, _W1_Z0:_W1_Z0 + 2 * _Z_CHUNK])).astype(BF16)
        return c_q, c_kv, kr

    def up_proj(r, xb, c_q, c_kv, kr):
        cqn = _rms(c_q, qn_ref[...]).astype(BF16)
        ckvn = _rms(c_kv, kn_ref[...]).astype(BF16)
        kr_ref[r, :] = _rope_block(kr, kc_ref[r, :], ks1_ref[r, :], ks2_ref[r, :]).astype(BF16)
        k_ref[r, :] = _dot(ckvn, wk_ref[...]).astype(BF16)
        v_ref[r, :] = _dot(ckvn, wv_ref[...]).astype(BF16)
        qc, qs1, qs2 = qc_all[r], qs1_all[r], qs2_all[r]
        grp = 4
        for g0 in range(0, MLA_HEADS, grp):
            qa = _dot(cqn, wq_ref[:, g0 * _HEAD_W:(g0 + grp) * _HEAD_W])
            for h in range(grp):
                sl = slice(h * _HEAD_W, (h + 1) * _HEAD_W)
                q_ref[r, (g0 + h) * _HEAD_W:(g0 + h + 1) * _HEAD_W] = (
                    _rope_block(qa[:, sl], qc, qs1, qs2).astype(BF16))
            if g0 >= 2 * grp:
                zc = slice((g0 // grp) * _Z_CHUNK, (g0 // grp + 1) * _Z_CHUNK)
                z_ref[r, zc] = _silu(
                    _dot(xb, w1_ref[:, _W1_Z0 + zc.start:_W1_Z0 + zc.stop])).astype(BF16)

    ra, rb = halves
    ya = out_proj(ra)
    yb = out_proj(rb)
    xa = layer_norm(ra, ya)
    la = latents(ra, xa)
    xbb = layer_norm(rb, yb)
    lb = latents(rb, xbb)
    up_proj(ra, xa, *la)
    up_proj(rb, xbb, *lb)


def _even_out_odd_in(onsa, nz, ogla, x2, wo, ln_g, ln_b, w1, wq, wk, wv, qn, kn, q_tabs, k_tabs,
                     tm=512):
    T = x2.shape[0]
    s_tiles = SEQ // tm
    row = lambda i: (i, 0)
    const = lambda i: (0, 0)
    pos = lambda i: (i % s_tiles, 0)

    def full(a):
        return pl.BlockSpec(a.shape, const)

    tab = pl.BlockSpec((tm, LANES), pos)
    n_kv = MLA_HEADS * MLA_NOPE
    return pl.pallas_call(
        _even_out_odd_in_kernel,
        grid=(T // tm,),
        in_specs=[pl.BlockSpec((tm, 512), row), pl.BlockSpec((tm, 512), row),
                  pl.BlockSpec((tm, 512), row), pl.BlockSpec((tm, D_MODEL), row),
                  full(wo), full(ln_g), full(ln_b),
                  full(w1), full(wq), full(wk), full(wv), full(qn), full(kn),
                  tab, tab, tab, tab, tab, tab],
        out_specs=[pl.BlockSpec((tm, D_MODEL), row),
                   pl.BlockSpec((tm, _QK_W), row), pl.BlockSpec((tm, n_kv), row),
                   pl.BlockSpec((tm, LANES), row), pl.BlockSpec((tm, n_kv), row),
                   pl.BlockSpec((tm, 1024), row)],
        out_shape=[jax.ShapeDtypeStruct((T, D_MODEL), F32),
                   jax.ShapeDtypeStruct((T, _QK_W), BF16), jax.ShapeDtypeStruct((T, n_kv), BF16),
                   jax.ShapeDtypeStruct((T, LANES), BF16), jax.ShapeDtypeStruct((T, n_kv), BF16),
                   jax.ShapeDtypeStruct((T, 1024), BF16)],
        compiler_params=_cparams(("parallel",)),
        name="even_out_odd_in_proj",
    )(onsa, nz, ogla, x2, wo, ln_g, ln_b, w1, wq, wk, wv, qn, kn, *q_tabs, *k_tabs)


_VT_ROWS = MLA_V + 16


def _mla_kernel(q_ref, kn_ref, kr_ref, v_ref, o_ref, kx_ref, qt_ref, vt_ref, s_ref, bias_ref, *, tq):
    nq = SEQ // tq
    a_idx = lax.broadcasted_iota(jnp.int32, (tq, tq), 0)
    b_idx = lax.broadcasted_iota(jnp.int32, (tq, tq), 1)
    bias_ref[...] = jnp.where(a_idx <= b_idx, 0.0, NEG_INF)
    lane = lax.broadcasted_iota(jnp.int32, (tq, LANES), 1)
    ones_rows = jnp.where(lax.broadcasted_iota(jnp.int32, (_VT_ROWS - MLA_V, tq), 0) == 0, 1.0, 0.0)

    for i in range(nq):
        rows = slice(i * tq, (i + 1) * tq)
        knp = kn_ref[rows, :].astype(F32)
        krb = kr_ref[rows, :].astype(F32)
        vt = v_ref[rows, :].astype(F32).T
        for h in range(2):
            kn_h = knp if h == 0 else pltpu.roll(knp, MLA_NOPE, 1)
            kx_ref[h, rows, :] = jnp.where(lane < MLA_NOPE, kn_h, krb).astype(BF16)
            qt_ref[h, :, rows] = q_ref[rows, h * _HEAD_W:(h + 1) * _HEAD_W].astype(F32).T.astype(BF16)
            vt_ref[h, :, rows] = jnp.concatenate(
                [vt[h * MLA_V:(h + 1) * MLA_V], ones_rows], axis=0).astype(BF16)

    units = [(i, h) for i in range(nq) for h in range(2)]
    n_buf = s_ref.shape[0]

    def stage1(n, ms):
        i, h = units[n]
        qt = qt_ref[h, :, i * tq:(i + 1) * tq]
        mx = None
        for j in range(i + 1):
            cols = slice(j * tq, (j + 1) * tq)
            s = _dot(kx_ref[h, cols, :], qt)
            if j == i:
                s = s + bias_ref[...]
            s_ref[n % n_buf, cols, :] = s
            t = s[0:8]
            for c in range(1, tq // 8):
                t = jnp.maximum(t, s[c * 8:(c + 1) * 8])
            mx = t if mx is None else jnp.maximum(mx, t)
            if j % _TILE_GROUP == _TILE_GROUP - 1:
                yield
        ms[n] = jnp.max(mx, axis=0, keepdims=True)

    def stage2(n, m, outs):
        i, h = units[n]
        acc = None
        for j in range(i + 1):
            cols = slice(j * tq, (j + 1) * tq)
            p = jnp.exp2(s_ref[n % n_buf, cols, :] - m).astype(BF16)
            pv = _dot(vt_ref[h, :, cols], p)
            acc = pv if acc is None else acc + pv
            if j % _TILE_GROUP == _TILE_GROUP - 1:
                yield
        outs.append(acc[0:MLA_V] / acc[MLA_V:MLA_V + 1])

    ms = {}
    _trace_round_robin([stage1(0, ms)])
    _trace_round_robin([stage1(1, ms)])
    outs = []
    for n in range(len(units)):
        gens = [stage2(n, ms[n], outs)]
        if n + 2 < len(units):
            gens.insert(0, stage1(n + 2, ms))
        _trace_round_robin(gens)
        if len(outs) == 2:
            i = units[n][0]
            o_ref[i * tq:(i + 1) * tq, :] = jnp.concatenate(outs, axis=0).T.astype(o_ref.dtype)
            outs = []


def _mla_attn(q, kn, kr, v, B, tq=256):
    kern = functools.partial(_mla_kernel, tq=tq)
    pair = pl.BlockSpec((SEQ, LANES), lambda b, h: (b, h))
    return pl.pallas_call(
        kern,
        grid=(B, MLA_HEADS // 2),
        in_specs=[pl.BlockSpec((SEQ, 2 * _HEAD_W), lambda b, h: (b, h)), pair,
                  pl.BlockSpec((SEQ, LANES), lambda b, h: (b, 0)), pair],
        out_specs=pair,
        out_shape=jax.ShapeDtypeStruct((B * SEQ, MLA_HEADS * MLA_V), BF16),
        scratch_shapes=[pltpu.VMEM((2, SEQ, _HEAD_W), BF16), pltpu.VMEM((2, _HEAD_W, SEQ), BF16),
                        pltpu.VMEM((2, _VT_ROWS, SEQ), BF16), pltpu.VMEM((4, SEQ, tq), F32),
                        pltpu.VMEM((tq, tq), F32)],
        compiler_params=_cparams(("parallel", "parallel")),
        name="mla_attn",
    )(q, kn, kr, v)


def _even_mixers(x2, B, w_in, cmp_pe, cmp_w1, cmp_w2, gla_w_gate, gla_b_gate, gla_norm):
    q, kvc, kv, ng, nz, gqk, gv, ga, gz = _even_in_proj(x2, _even_in_weight(w_in))

    pe_tab, w1_bd = _compress_weights(cmp_pe, cmp_w1)
    kc, vct = _compress(kvc, pe_tab, w1_bd, cmp_w2, B)

    ocmp_t, qaug_t = _cmp_attn(q, kc, vct, _overlap_t(), _q_alibi_table(), _head_select_mats(),
                               _cmp_bias_table(), ng, B)
    onsa = _sel_win_attn(qaug_t, kv, _nsa_k_tables(), ng, ocmp_t, B)

    wg = jnp.pad(gla_w_gate, ((0, LANES - GLA_GATE_RANK), (0, 0))).astype(BF16)
    ogla = _gla(gqk, gv, ga, gz, wg, gla_b_gate.reshape(1, -1), gla_norm.reshape(1, -1), B)
    return onsa, nz, ogla


def kernel(x, e_w_in, e_cmp_pe, e_cmp_w1, e_cmp_w2, e_gla_w_gate, e_gla_b_gate, e_gla_norm,
           e_w_out, e_ln_g, e_ln_b, o_w_in, o_q_norm, o_w_uq, o_kv_norm, o_w_ukv, o_w_out,
           o_ln_g, o_ln_b):
    assert DEPTH == 2
    B, S, D = x.shape
    x2 = x.reshape(B * S, D)
    onsa, nz, ogla = _even_mixers(x2, B, e_w_in[0], e_cmp_pe[0], e_cmp_w1[0], e_cmp_w2[0],
                                  e_gla_w_gate[0], e_gla_b_gate[0], e_gla_norm[0])
    w1, wq, wk, wv = _odd_weights(o_w_in[0], o_w_uq[0], o_w_ukv[0])
    q_tabs, k_tabs = _rope_tables()
    x1, q, kn, kr, v, z = _even_out_odd_in(
        onsa, nz, ogla, x2, e_w_out[0].astype(BF16), e_ln_g[0].reshape(1, -1), e_ln_b[0].reshape(1, -1),
        w1, wq, wk, wv, o_q_norm[0].reshape(1, -1), o_kv_norm[0].reshape(1, -1), q_tabs, k_tabs)
    o = _mla_attn(q, kn, kr, v, B)
    out = _odd_out(o, z, x1, o_w_out[0].astype(BF16), o_ln_g[0].reshape(1, -1), o_ln_b[0].reshape(1, -1))
    return out.reshape(B, S, D)
```
